```python
import jax, jax.numpy as jnp
from jax import lax
import numpy as np

D_MODEL = 2048
BATCH = 4
SEQ = 2048
DEPTH = 1
DEC_BATCH = 128
DEC_SEQ = 4
PAST_LEN = 16384
PAGE_SIZE = 128

MIX_WIDTH = D_MODEL
M_WIDTH = MIX_WIDTH // 2
G_WIDTH = MIX_WIDTH - M_WIDTH
M_HEADS = 4
M_HEAD_DIM = M_WIDTH // M_HEADS
G_HEADS = 8
G_HEAD_DIM = G_WIDTH // G_HEADS
CONV_W = 4
CHUNK = 64
D_FF = -(-8 * D_MODEL // (3 * 256)) * 256
DEEPNORM_ALPHA = (2.0 * DEPTH) ** 0.25
DEEPNORM_BETA = (8.0 * DEPTH) ** -0.25
LN_EPS = 1e-5
RMS_EPS = 1e-6
IN_SIZES = (M_WIDTH, M_WIDTH, M_WIDTH, M_WIDTH, 3 * G_WIDTH, G_WIDTH, M_HEADS, M_HEADS, G_HEADS, G_HEADS)
IN_COLS = sum(IN_SIZES)

kernel_name = 'hybrid_mlstm_gdn_step'


def layer_norm(x, g, b):
    xf = x.astype(jnp.float32)
    mu = jnp.mean(xf, -1, keepdims=True)
    var = jnp.mean(jnp.square(xf - mu), -1, keepdims=True)
    return ((xf - mu) * lax.rsqrt(var + LN_EPS) * g.astype(jnp.float32) + b.astype(jnp.float32)).astype(x.dtype)


def head_rms_norm(x, g):
    H, d = x.shape[-2:]
    return x * lax.rsqrt(jnp.mean(jnp.square(x), -1, keepdims=True) + RMS_EPS) * g.astype(jnp.float32).reshape(H, d)


def l2norm(x):
    x = x.astype(jnp.float32)
    return x * lax.rsqrt(jnp.sum(jnp.square(x), -1, keepdims=True) + RMS_EPS)


def chunk_len(T):
    return CHUNK if T % CHUNK == 0 else T


def to_chunks(a, L):
    B, T = a.shape[:2]
    return jnp.moveaxis(a.reshape(B, T // L, L, *a.shape[2:]), 1, 0)


def from_chunks(a):
    NC, B, L = a.shape[:3]
    return jnp.moveaxis(a, 0, 1).reshape(B, NC * L, *a.shape[3:])


def mlstm_scan(q, k, v, i_pre, f_pre, C0, n0, m0):
    B, T, H, dk = q.shape
    L = chunk_len(T)
    f32 = jnp.float32
    q = q.astype(f32)
    k = k.astype(f32) * (dk ** -0.5)
    v = v.astype(f32)
    ig = i_pre.astype(f32)
    lf = jax.nn.log_sigmoid(f_pre.astype(f32))
    causal = jnp.tril(jnp.ones((L, L), bool))

    def step(carry, xs):
        C, n, m = carry
        q_, k_, v_, ig_, lf_ = xs
        bt = jnp.moveaxis(jnp.cumsum(lf_, axis=1), 1, -1)
        it = jnp.moveaxis(ig_, 1, -1)
        logD = jnp.where(causal, bt[..., :, None] - bt[..., None, :] + it[..., None, :], -jnp.inf)
        inter = bt + m[..., None]
        m_t = jnp.maximum(inter, jnp.max(logD, -1))
        inter_w = jnp.exp(inter - m_t)
        s = jnp.einsum('blhd,bshd->bhls', q_, k_) * jnp.exp(logD - m_t[..., None])
        num = inter_w[..., None] * jnp.einsum('blhd,bhde->bhle', q_, C) + jnp.einsum('bhls,bshe->bhle', s, v_)
        den = inter_w * jnp.einsum('blhd,bhd->bhl', q_, n) + jnp.sum(s, -1)
        h = num / jnp.maximum(jnp.abs(den), jnp.exp(-m_t))[..., None]
        bL = bt[..., -1]
        wlog = bL[..., None] - bt + it
        m_new = jnp.maximum(bL + m, jnp.max(wlog, -1))
        w = jnp.moveaxis(jnp.exp(wlog - m_new[..., None]), -1, 1)
        decay = jnp.exp(bL + m - m_new)
        C_new = decay[..., None, None] * C + jnp.einsum('bshd,bshe->bhde', k_ * w[..., None], v_)
        n_new = decay[..., None] * n + jnp.einsum('bsh,bshd->bhd', w, k_)
        return (C_new, n_new, m_new), jnp.moveaxis(h, 2, 1)

    xs = tuple(to_chunks(a, L) for a in (q, k, v, ig, lf))
    (C1, n1, m1), hs = lax.scan(step, (C0.astype(f32), n0.astype(f32), m0.astype(f32)), xs)
    return from_chunks(hs), C1, n1, m1


def gdn_scan(q, k, v, log_a, beta, S0):
    B, T, H, dk = q.shape
    L = chunk_len(T)
    f32 = jnp.float32
    q = q.astype(f32) * (dk ** -0.5)
    k = k.astype(f32)
    v = v.astype(f32)
    causal = jnp.tril(jnp.ones((L, L), bool))
    strict = jnp.tril(jnp.ones((L, L), f32), -1)
    eye = jnp.eye(L, dtype=f32)

    def step(S, xs):
        q_, k_, v_, la_, be_ = xs
        b = jnp.moveaxis(jnp.cumsum(la_, axis=1), 1, -1)
        bet = jnp.moveaxis(be_, 1, -1)
        decay = jnp.exp(jnp.where(causal, b[..., :, None] - b[..., None, :], -jnp.inf))
        kh = jnp.moveaxis(k_, 1, 2)
        A = bet[..., :, None] * jnp.einsum('bhld,bhsd->bhls', kh, kh) * decay * strict
        rhs = jnp.concatenate([bet[..., None] * jnp.moveaxis(v_, 1, 2),
                               (bet * jnp.exp(b))[..., None] * kh], -1)
        sol = lax.linalg.triangular_solve(eye + A, rhs, left_side=True, lower=True, unit_diagonal=True)
        U0, W = sol[..., :v_.shape[-1]], sol[..., v_.shape[-1]:]
        U = U0 - jnp.einsum('bhld,bhde->bhle', W, S)
        qk = jnp.einsum('blhd,bhsd->bhls', q_, kh) * decay
        o = jnp.exp(b)[..., None] * jnp.einsum('blhd,bhde->bhle', q_, S) + jnp.einsum('bhls,bhse->bhle', qk, U)
        bL = b[..., -1]
        wk = jnp.exp(bL[..., None] - b)[..., None] * kh
        S_new = jnp.exp(bL)[..., None, None] * S + jnp.einsum('bhld,bhle->bhde', wk, U)
        return S_new, jnp.moveaxis(o, 2, 1)

    xs = tuple(to_chunks(a, L) for a in (q, k, v, log_a.astype(f32), beta.astype(f32)))
    S1, os_ = lax.scan(step, S0.astype(f32), xs)
    return from_chunks(os_), S1


def causal_conv(x, buf, w):
    xp = jnp.concatenate([buf.astype(x.dtype), x], 1)
    T = x.shape[1]
    y = sum(w[j] * xp[:, j:j + T] for j in range(CONV_W))
    return jax.nn.silu(y), xp[:, -(CONV_W - 1):]


def hybrid_layer(x, c, C0, n0, m0, S0, conv0, w_ada, b_ada, w_in, m_i_bias, m_f_bias, m_norm_g,
                 conv_w, g_dt_bias, g_A_log, g_norm_g, w_out, ln1_g, ln1_b, w_gu, w_down, ln2_g, ln2_b):
    B, T, _ = x.shape
    dt = x.dtype
    f32 = jnp.float32
    ada = jax.nn.silu(c) @ w_ada + b_ada
    sh1, sc1, gt1, sh2, sc2, gt2 = [a[:, None, :] for a in jnp.split(ada, 6, axis=-1)]
    h = x * (1 + sc1) + sh1
    proj = h @ w_in
    mq, mk, mv, mo, g_qkv, gz, mi, mf, gb, g_a = jnp.split(proj, np.cumsum(IN_SIZES)[:-1].tolist(), axis=-1)
    hm, C1, n1, m1 = mlstm_scan(mq.reshape(B, T, M_HEADS, M_HEAD_DIM), mk.reshape(B, T, M_HEADS, M_HEAD_DIM),
                                mv.reshape(B, T, M_HEADS, M_HEAD_DIM), mi + m_i_bias, mf + m_f_bias, C0, n0, m0)
    hm = head_rms_norm(hm, m_norm_g) * jax.nn.sigmoid(mo.astype(f32)).reshape(B, T, M_HEADS, M_HEAD_DIM)
    g_conv, conv1 = causal_conv(g_qkv, conv0, conv_w)
    gq, gk, gv = jnp.split(g_conv, 3, axis=-1)
    log_a = -jnp.exp(g_A_log.astype(f32)) * jax.nn.softplus((g_a + g_dt_bias).astype(f32))
    beta = jax.nn.sigmoid(gb.astype(f32))
    hg, S1 = gdn_scan(l2norm(gq.reshape(B, T, G_HEADS, G_HEAD_DIM)), l2norm(gk.reshape(B, T, G_HEADS, G_HEAD_DIM)),
                      gv.reshape(B, T, G_HEADS, G_HEAD_DIM), log_a, beta, S0)
    hg = head_rms_norm(hg, g_norm_g) * jax.nn.silu(gz.astype(f32)).reshape(B, T, G_HEADS, G_HEAD_DIM)
    mix = jnp.concatenate([hm.reshape(B, T, M_WIDTH), hg.reshape(B, T, G_WIDTH)], -1).astype(dt) @ w_out
    x = layer_norm(DEEPNORM_ALPHA * x + (1 + gt1) * mix, ln1_g, ln1_b)
    h2 = x * (1 + sc2) + sh2
    gate, up = jnp.split(h2 @ w_gu, 2, axis=-1)
    ffn = (jax.nn.silu(gate) * up) @ w_down
    x = layer_norm(DEEPNORM_ALPHA * x + (1 + gt2) * ffn, ln2_g, ln2_b)
    return x, C1.astype(dt), n1.astype(dt), m1.astype(dt), S1.astype(dt), conv1.astype(dt)


def setup_inputs(seed: int = 0) -> dict:
    key = jax.random.key(seed)
    ks = jax.random.split(key, 32)
    f32 = jnp.float32
    nrm = lambda k, shape, s: s * jax.random.normal(k, shape, f32)
    return {
        'x_prompt': nrm(ks[0], (BATCH, SEQ, D_MODEL), 1.0),
        'x_sample': nrm(ks[1], (DEC_BATCH, DEC_SEQ, D_MODEL), 1.0),
        'state_mlstm_C': nrm(ks[2], (DEC_BATCH, M_HEADS, M_HEAD_DIM, M_HEAD_DIM), 0.1),
        'state_mlstm_n': nrm(ks[3], (DEC_BATCH, M_HEADS, M_HEAD_DIM), 0.1),
        'state_mlstm_m': nrm(ks[4], (DEC_BATCH, M_HEADS), 1.0),
        'state_gdn_S': nrm(ks[5], (DEC_BATCH, G_HEADS, G_HEAD_DIM, G_HEAD_DIM), 0.1),
        'state_gdn_conv': nrm(ks[6], (DEC_BATCH, CONV_W - 1, 3 * G_WIDTH), 1.0),
        'c_prompt': nrm(ks[7], (BATCH, D_MODEL), 1.0),
        'c_sample': nrm(ks[8], (DEC_BATCH, D_MODEL), 1.0),
        'w_ada': nrm(ks[9], (D_MODEL, 6 * D_MODEL), 0.5 * D_MODEL ** -0.5),
        'b_ada': nrm(ks[10], (6 * D_MODEL,), 0.02),
        'w_in': nrm(ks[11], (D_MODEL, IN_COLS), D_MODEL ** -0.5),
        'm_i_bias': nrm(ks[12], (M_HEADS,), 0.1),
        'm_f_bias': 3.0 + nrm(ks[13], (M_HEADS,), 0.5),
        'm_norm_g': 1.0 + nrm(ks[14], (M_WIDTH,), 0.02),
        'conv_w': nrm(ks[15], (CONV_W, 3 * G_WIDTH), CONV_W ** -0.5),
        'g_dt_bias': nrm(ks[16], (G_HEADS,), 0.1),
        'g_A_log': jnp.log(jax.random.uniform(ks[17], (G_HEADS,), f32, 1.0, 16.0)),
        'g_norm_g': 1.0 + nrm(ks[18], (G_WIDTH,), 0.02),
        'w_out': nrm(ks[19], (MIX_WIDTH, D_MODEL), DEEPNORM_BETA * MIX_WIDTH ** -0.5),
        'ln1_g': 1.0 + nrm(ks[20], (D_MODEL,), 0.02),
        'ln1_b': nrm(ks[21], (D_MODEL,), 0.02),
        'w_gu': nrm(ks[22], (D_MODEL, 2 * D_FF), D_MODEL ** -0.5),
        'w_down': nrm(ks[23], (D_FF, D_MODEL), DEEPNORM_BETA * D_FF ** -0.5),
        'ln2_g': 1.0 + nrm(ks[24], (D_MODEL,), 0.02),
        'ln2_b': nrm(ks[25], (D_MODEL,), 0.02),
    }


def reference(x_prompt, x_sample, state_mlstm_C, state_mlstm_n, state_mlstm_m, state_gdn_S, state_gdn_conv,
              c_prompt, c_sample, w_ada, b_ada, w_in, m_i_bias, m_f_bias, m_norm_g, conv_w, g_dt_bias,
              g_A_log, g_norm_g, w_out, ln1_g, ln1_b, w_gu, w_down, ln2_g, ln2_b):
    f32 = jnp.float32
    Bp = x_prompt.shape[0]
    y_p = x_prompt
    y_s = x_sample
    for _ in range(DEPTH):
        y_p, p_C, p_n, p_m, p_S, p_conv = hybrid_layer(
            y_p, c_prompt,
            jnp.zeros((Bp, M_HEADS, M_HEAD_DIM, M_HEAD_DIM), f32), jnp.zeros((Bp, M_HEADS, M_HEAD_DIM), f32),
            jnp.zeros((Bp, M_HEADS), f32), jnp.zeros((Bp, G_HEADS, G_HEAD_DIM, G_HEAD_DIM), f32),
            jnp.zeros((Bp, CONV_W - 1, 3 * G_WIDTH), x_prompt.dtype),
            w_ada, b_ada, w_in, m_i_bias, m_f_bias, m_norm_g, conv_w, g_dt_bias, g_A_log, g_norm_g,
            w_out, ln1_g, ln1_b, w_gu, w_down, ln2_g, ln2_b)
        y_s, s_C, s_n, s_m, s_S, s_conv = hybrid_layer(
            y_s, c_sample, state_mlstm_C, state_mlstm_n, state_mlstm_m, state_gdn_S, state_gdn_conv,
            w_ada, b_ada, w_in, m_i_bias, m_f_bias, m_norm_g, conv_w, g_dt_bias, g_A_log, g_norm_g,
            w_out, ln1_g, ln1_b, w_gu, w_down, ln2_g, ln2_b)
    return (y_p, y_s, p_C, p_n, p_m, p_S, p_conv, s_C, s_n, s_m, s_S, s_conv)
```

```python
import functools

import jax
import jax.numpy as jnp
from jax import lax
from jax.experimental import pallas as pl
from jax.experimental.pallas import tpu as pltpu

F32 = jnp.float32
BF16 = jnp.bfloat16

D_MODEL = 2048
M_HEADS = 4
M_HEAD_DIM = 256
M_WIDTH = M_HEADS * M_HEAD_DIM
G_HEADS = 8
G_HEAD_DIM = 128
G_WIDTH = G_HEADS * G_HEAD_DIM
CONV_W = 4
CHUNK = 64
D_FF = 5632
MAIN_COLS = 4 * M_WIDTH + 3 * G_WIDTH + G_WIDTH
GATE_COLS = 2 * M_HEADS + 2 * G_HEADS
GATE_PAD = 128
DEEPNORM_ALPHA = 2.0 ** 0.25
LN_EPS = 1e-5
RMS_EPS = 1e-6
GI0, GF0, GB0, GA0 = 0, M_HEADS, 2 * M_HEADS, 2 * M_HEADS + G_HEADS

VMEM_LIMIT_BYTES = 56 * 1024 * 1024


def _cparams(sem):
    return pltpu.CompilerParams(dimension_semantics=sem, vmem_limit_bytes=VMEM_LIMIT_BYTES)


def _dot(a, b):
    return jnp.dot(a, b, preferred_element_type=F32)


def _dot_nt(a, b):
    return lax.dot_general(a, b, (((1,), (1,)), ((), ())), preferred_element_type=F32)


def _dot_tn(a, b):
    return lax.dot_general(a, b, (((0,), (0,)), ((), ())), preferred_element_type=F32)


def _dot_hi(a, b):
    return jnp.dot(a, b, preferred_element_type=F32, precision=lax.Precision.HIGHEST)


def _ada_kernel(c_ref, w_ref, b_ref, o_ref):
    c = c_ref[...]
    a = (c * jax.nn.sigmoid(c)).astype(BF16)
    o_ref[...] = _dot(a, w_ref[...].astype(BF16)) + b_ref[...]


def _ada(c_all, w_ada, b_ada, tn=1024):
    n_rows, d = c_all.shape
    n_cols = w_ada.shape[1]
    return pl.pallas_call(
        _ada_kernel,
        grid=(n_cols // tn,),
        in_specs=[pl.BlockSpec((n_rows, d), lambda j: (0, 0)),
                  pl.BlockSpec((d, tn), lambda j: (0, j)),
                  pl.BlockSpec((1, tn), lambda j: (0, j))],
        out_specs=pl.BlockSpec((n_rows, tn), lambda j: (0, j)),
        out_shape=jax.ShapeDtypeStruct((n_rows, n_cols), F32),
        compiler_params=_cparams(("arbitrary",)),
        name="ada",
    )(c_all, w_ada, b_ada.reshape(1, n_cols))


def _inproj_kernel(x_ref, sc_ref, sh_ref, w_ref, wg_ref, o_ref, og_ref, h_scr):
    @pl.when(pl.program_id(1) == 0)
    def _():
        h = (x_ref[...] * (1.0 + sc_ref[0]) + sh_ref[0]).astype(BF16)
        h_scr[...] = h
        og_ref[...] = _dot(h, wg_ref[...])

    o_ref[...] = _dot(h_scr[...], w_ref[...])


def _inproj(x2d, sc, sh, w_main, w_gate, tm, tiles_per_mod, tn=1024):
    m, d = x2d.shape
    mod_rows = sc.shape[1]
    mod_spec = pl.BlockSpec((1, mod_rows, d), lambda i, j: (i // tiles_per_mod, 0, 0))
    return pl.pallas_call(
        _inproj_kernel,
        grid=(m // tm, MAIN_COLS // tn),
        in_specs=[pl.BlockSpec((tm, d), lambda i, j: (i, 0)),
                  mod_spec, mod_spec,
                  pl.BlockSpec((d, tn), lambda i, j: (0, j)),
                  pl.BlockSpec((d, GATE_PAD), lambda i, j: (0, 0))],
        out_specs=[pl.BlockSpec((tm, tn), lambda i, j: (i, j)),
                   pl.BlockSpec((tm, GATE_PAD), lambda i, j: (i, 0))],
        out_shape=[jax.ShapeDtypeStruct((m, MAIN_COLS), F32),
                   jax.ShapeDtypeStruct((m, GATE_PAD), F32)],
        scratch_shapes=[pltpu.VMEM((tm, d), BF16)],
        compiler_params=_cparams(("arbitrary", "arbitrary")),
        name="inproj",
    )(x2d, sc, sh, w_main, w_gate)


def _gate_tables(g, bias, alog, gid):
    x = g + bias
    is_f = (gid >= GF0) & (gid < GB0)
    is_a = (gid >= GA0) & (gid < GA0 + G_HEADS)
    log_f = jax.nn.log_sigmoid(x)
    log_a = -jnp.exp(alog) * jax.nn.softplus(x)
    inc = jnp.where(is_f, log_f, jnp.where(is_a, log_a, 0.0))
    return x, inc, jax.nn.sigmoid(x)


def _chunk_masks(L):
    row = lax.broadcasted_iota(jnp.int32, (L, L), 0)
    col = lax.broadcasted_iota(jnp.int32, (L, L), 1)
    return row >= col, row > col, row == col


def _gates_both_forms(gcol_ref, grow_ref, gb_row_ref, gb_col_ref, al_row_ref, al_col_ref, L):
    causal, _, _ = _chunk_masks(L)
    tril = causal.astype(F32)
    triu = (lax.broadcasted_iota(jnp.int32, (L, L), 0) <= lax.broadcasted_iota(jnp.int32, (L, L), 1)).astype(F32)
    gid_c = lax.broadcasted_iota(jnp.int32, (L, GATE_PAD), 1)
    gid_r = lax.broadcasted_iota(jnp.int32, (GATE_PAD, L), 0)
    x_c, inc_c, beta_c = _gate_tables(gcol_ref[0], gb_row_ref[...], al_row_ref[...], gid_c)
    x_r, inc_r, _ = _gate_tables(grow_ref[0], gb_col_ref[...], al_col_ref[...], gid_r)
    cum_c = _dot_hi(tril, inc_c)
    cum_r = _dot_hi(inc_r, triu)
    return x_c, cum_c, beta_c, x_r, cum_r


def _rms_gate(h, gain, gate):
    return h * lax.rsqrt(jnp.mean(h * h, axis=-1, keepdims=True) + RMS_EPS) * gain * gate


def _mlstm_kernel(*refs, L, has_init):
    if has_init:
        (q_ref, k_ref, v_ref, o_ref, gcol_ref, grow_ref, gb_row_ref, gb_col_ref, al_row_ref, al_col_ref,
         ng_ref, c0_ref, n0_ref, m0_ref, h_ref, c_ref, n_ref, m_ref) = refs
    else:
        (q_ref, k_ref, v_ref, o_ref, gcol_ref, grow_ref, gb_row_ref, gb_col_ref, al_row_ref, al_col_ref,
         ng_ref, h_ref, c_ref, n_ref, m_ref) = refs

    @pl.when(pl.program_id(1) == 0)
    def _():
        if has_init:
            c_ref[...] = c0_ref[...]
            n_ref[...] = n0_ref[...]
            m_ref[...] = m0_ref[...]
        else:
            c_ref[...] = jnp.zeros_like(c_ref)
            n_ref[...] = jnp.zeros_like(n_ref)
            m_ref[...] = jnp.zeros_like(m_ref)

    causal, _, _ = _chunk_masks(L)
    x_c, cum_c, _, x_r, cum_r = _gates_both_forms(
        gcol_ref, grow_ref, gb_row_ref, gb_col_ref, al_row_ref, al_col_ref, L)
    lane = lax.broadcasted_iota(jnp.int32, (1, GATE_PAD), 1)
    m_all = m_ref[0]
    m_next = m_all
    scale = M_HEAD_DIM ** -0.5

    for h in range(M_HEADS):
        hs = slice(h * M_HEAD_DIM, (h + 1) * M_HEAD_DIM)
        q = q_ref[0, :, hs]
        k = k_ref[0, :, hs] * scale
        v = v_ref[0, :, hs]
        qb, kb, vb = q.astype(BF16), k.astype(BF16), v.astype(BF16)
        ig_c = x_c[:, GI0 + h:GI0 + h + 1]
        ig_r = x_r[GI0 + h:GI0 + h + 1, :]
        bt_c = cum_c[:, GF0 + h:GF0 + h + 1]
        bt_r = cum_r[GF0 + h:GF0 + h + 1, :]
        m0 = jnp.sum(jnp.where(lane == h, m_all, 0.0), axis=1, keepdims=True)

        log_d = jnp.where(causal, bt_c - bt_r + ig_r, -jnp.inf)
        inter = bt_c + m0
        m_t = jnp.maximum(inter, jnp.max(log_d, axis=1, keepdims=True))
        inter_w = jnp.exp(inter - m_t)
        s = _dot_nt(qb, kb) * jnp.exp(log_d - m_t)
        c_old = c_ref[0, h]
        n_old = n_ref[0, h:h + 1, :]
        num = inter_w * _dot(qb, c_old.astype(BF16)) + _dot(s.astype(BF16), vb)
        den = inter_w * jnp.sum(q * n_old, axis=1, keepdims=True) + jnp.sum(s, axis=1, keepdims=True)
        hh = num / jnp.maximum(jnp.abs(den), jnp.exp(-m_t))

        b_last = bt_c[L - 1:L, :]
        wlog_c = b_last - bt_c + ig_c
        wlog_r = b_last - bt_r + ig_r
        m_new = jnp.maximum(b_last + m0, jnp.max(wlog_r, axis=1, keepdims=True))
        kw = k * jnp.exp(wlog_c - m_new)
        decay = jnp.exp(b_last + m0 - m_new)
        c_ref[0, h] = decay * c_old + _dot_tn(kw.astype(BF16), vb)
        n_ref[0, h:h + 1, :] = decay * n_old + jnp.sum(kw, axis=0, keepdims=True)
        m_next = jnp.where(lane == h, m_new, m_next)

        gate = jax.nn.sigmoid(o_ref[0, :, hs])
        h_ref[0, :, hs] = _rms_gate(hh, ng_ref[:, hs], gate).astype(h_ref.dtype)

    m_ref[0] = m_next


def _col_spec(L, width, col_block, nc):
    return pl.BlockSpec((1, L, width), lambda b, c: (b * nc + c, 0, col_block))


def _gate_specs(L, nc):
    return [pl.BlockSpec((1, L, GATE_PAD), lambda b, c: (b * nc + c, 0, 0)),
            pl.BlockSpec((1, GATE_PAD, L), lambda b, c: (b * nc + c, 0, 0)),
            pl.BlockSpec((1, GATE_PAD), lambda b, c: (0, 0)),
            pl.BlockSpec((GATE_PAD, 1), lambda b, c: (0, 0)),
            pl.BlockSpec((1, GATE_PAD), lambda b, c: (0, 0)),
            pl.BlockSpec((GATE_PAD, 1), lambda b, c: (0, 0))]


def _mlstm(proj3, gates3, gates_t, gvecs, norm_g, batch, nc, L, init):
    has_init = init is not None
    in_specs = [_col_spec(L, M_WIDTH, 0, nc), _col_spec(L, M_WIDTH, 1, nc), _col_spec(L, M_WIDTH, 2, nc),
                _col_spec(L, M_WIDTH, 3, nc)] + _gate_specs(L, nc)
    in_specs.append(pl.BlockSpec((1, M_WIDTH), lambda b, c: (0, 0)))
    c_spec = pl.BlockSpec((1, M_HEADS, M_HEAD_DIM, M_HEAD_DIM), lambda b, c: (b, 0, 0, 0))
    n_spec = pl.BlockSpec((1, M_HEADS, M_HEAD_DIM), lambda b, c: (b, 0, 0))
    m_spec = pl.BlockSpec((1, 1, GATE_PAD), lambda b, c: (b, 0, 0))
    args = [proj3, proj3, proj3, proj3, gates3, gates_t, *gvecs, norm_g.reshape(1, M_WIDTH)]
    if has_init:
        in_specs += [c_spec, n_spec, m_spec]
        args += list(init)
    return pl.pallas_call(
        functools.partial(_mlstm_kernel, L=L, has_init=has_init),
        grid=(batch, nc),
        in_specs=in_specs,
        out_specs=[pl.BlockSpec((1, L, M_WIDTH), lambda b, c: (b * nc + c, 0, 0)), c_spec, n_spec, m_spec],
        out_shape=[jax.ShapeDtypeStruct((batch * nc, L, M_WIDTH), BF16),
                   jax.ShapeDtypeStruct((batch, M_HEADS, M_HEAD_DIM, M_HEAD_DIM), F32),
                   jax.ShapeDtypeStruct((batch, M_HEADS, M_HEAD_DIM), F32),
                   jax.ShapeDtypeStruct((batch, 1, GATE_PAD), F32)],
        compiler_params=_cparams(("arbitrary", "arbitrary")),
        name="mlstm",
    )(*args)


def _inv_unit_lower(a, L):
    _, _, eye = _chunk_masks(L)
    p = -a
    t = eye.astype(F32) + p
    for _ in range(max(1, (L - 1).bit_length()) - 1):
        p = _dot_hi(p, p)
        t = t + _dot_hi(t, p)
    return t


def _gdn_kernel(*refs, L, nc, has_init):
    if has_init:
        (xq_ref, xk_ref, xv_ref, z_ref, gcol_ref, grow_ref, gb_row_ref, gb_col_ref, al_row_ref, al_col_ref,
         cw_ref, ng_ref, s0_ref, conv0_ref, h_ref, s_ref, buf) = refs
    else:
        (xq_ref, xk_ref, xv_ref, z_ref, gcol_ref, grow_ref, gb_row_ref, gb_col_ref, al_row_ref, al_col_ref,
         cw_ref, ng_ref, h_ref, s_ref, buf) = refs
    hist = CONV_W - 1
    base = 8 - hist

    @pl.when(pl.program_id(1) == 0)
    def _():
        buf[0:8, :] = jnp.zeros((8, 3 * G_WIDTH), F32)
        if has_init:
            s_ref[...] = s0_ref[...]
            buf[base:8, :] = conv0_ref[0]
        else:
            s_ref[...] = jnp.zeros_like(s_ref)

    buf[8:8 + L, 0:G_WIDTH] = xq_ref[0]
    buf[8:8 + L, G_WIDTH:2 * G_WIDTH] = xk_ref[0]
    buf[8:8 + L, 2 * G_WIDTH:3 * G_WIDTH] = xv_ref[0]
    y = cw_ref[0:1, :] * buf[base:base + L, :]
    for j in range(1, CONV_W):
        y = y + cw_ref[j:j + 1, :] * buf[base + j:base + j + L, :]
    if nc > 1:
        buf[0:8, :] = buf[L:L + 8, :]
    act = y * jax.nn.sigmoid(y)

    causal, strict, _ = _chunk_masks(L)
    _, cum_c, beta_c, _, cum_r = _gates_both_forms(
        gcol_ref, grow_ref, gb_row_ref, gb_col_ref, al_row_ref, al_col_ref, L)
    scale = G_HEAD_DIM ** -0.5

    for h in range(G_HEADS):
        hs = slice(h * G_HEAD_DIM, (h + 1) * G_HEAD_DIM)
        q = act[:, h * G_HEAD_DIM:(h + 1) * G_HEAD_DIM]
        k = act[:, G_WIDTH + h * G_HEAD_DIM:G_WIDTH + (h + 1) * G_HEAD_DIM]
        v = act[:, 2 * G_WIDTH + h * G_HEAD_DIM:2 * G_WIDTH + (h + 1) * G_HEAD_DIM]
        q = q * lax.rsqrt(jnp.sum(q * q, axis=-1, keepdims=True) + RMS_EPS) * scale
        k = k * lax.rsqrt(jnp.sum(k * k, axis=-1, keepdims=True) + RMS_EPS)
        qb, kb = q.astype(BF16), k.astype(BF16)
        b_c = cum_c[:, GA0 + h:GA0 + h + 1]
        b_r = cum_r[GA0 + h:GA0 + h + 1, :]
        bet = beta_c[:, GB0 + h:GB0 + h + 1]
        decay = jnp.exp(jnp.where(causal, b_c - b_r, -jnp.inf))
        eb = jnp.exp(b_c)
        a = jnp.where(strict, bet * _dot_nt(kb, kb) * decay, 0.0)
        t = _inv_unit_lower(a, L)
        u0 = _dot_hi(t, bet * v)
        w = _dot_hi(t, (bet * eb) * k)
        s_old = s_ref[0, h]
        sb = s_old.astype(BF16)
        u = u0 - _dot(w.astype(BF16), sb)
        qk = _dot_nt(qb, kb) * decay
        o = eb * _dot(qb, sb) + _dot(qk.astype(BF16), u.astype(BF16))
        b_last = b_c[L - 1:L, :]
        wk = jnp.exp(b_last - b_c) * k
        s_ref[0, h] = jnp.exp(b_last) * s_old + _dot_tn(wk.astype(BF16), u.astype(BF16))
        z = z_ref[0, :, hs]
        h_ref[0, :, hs] = _rms_gate(o, ng_ref[:, hs], z * jax.nn.sigmoid(z)).astype(h_ref.dtype)


def _gdn(proj3, gates3, gates_t, gvecs, conv_w, norm_g, batch, nc, L, init):
    has_init = init is not None
    blk0 = 4 * M_WIDTH // G_WIDTH
    in_specs = [_col_spec(L, G_WIDTH, blk0 + i, nc) for i in range(4)] + _gate_specs(L, nc)
    in_specs += [pl.BlockSpec((CONV_W, 3 * G_WIDTH), lambda b, c: (0, 0)),
                 pl.BlockSpec((1, G_WIDTH), lambda b, c: (0, 0))]
    s_spec = pl.BlockSpec((1, G_HEADS, G_HEAD_DIM, G_HEAD_DIM), lambda b, c: (b, 0, 0, 0))
    args = [proj3, proj3, proj3, proj3, gates3, gates_t, *gvecs, conv_w, norm_g.reshape(1, G_WIDTH)]
    if has_init:
        in_specs += [s_spec, pl.BlockSpec((1, CONV_W - 1, 3 * G_WIDTH), lambda b, c: (b, 0, 0))]
        args += list(init)
    buf_rows = 8 + ((L + 7) // 8) * 8
    return pl.pallas_call(
        functools.partial(_gdn_kernel, L=L, nc=nc, has_init=has_init),
        grid=(batch, nc),
        in_specs=in_specs,
        out_specs=[pl.BlockSpec((1, L, G_WIDTH), lambda b, c: (b * nc + c, 0, 0)), s_spec],
        out_shape=[jax.ShapeDtypeStruct((batch * nc, L, G_WIDTH), BF16),
                   jax.ShapeDtypeStruct((batch, G_HEADS, G_HEAD_DIM, G_HEAD_DIM), F32)],
        scratch_shapes=[pltpu.VMEM((buf_rows, 3 * G_WIDTH), F32)],
        compiler_params=_cparams(("arbitrary", "arbitrary")),
        name="gdn",
    )(*args)


def _layer_norm(y, g, b):
    mu = jnp.mean(y, axis=-1, keepdims=True)
    yc = y - mu
    var = jnp.mean(yc * yc, axis=-1, keepdims=True)
    return yc * lax.rsqrt(var + LN_EPS) * g + b


def _outproj_kernel(hm_ref, hg_ref, x_ref, gt_ref, w_ref, g_ref, b_ref, o_ref):
    mix = _dot(hm_ref[...], w_ref[0:M_WIDTH, :]) + _dot(hg_ref[...], w_ref[M_WIDTH:M_WIDTH + G_WIDTH, :])
    y = DEEPNORM_ALPHA * x_ref[...] + (1.0 + gt_ref[0]) * mix
    o_ref[...] = _layer_norm(y, g_ref[...], b_ref[...])


def _outproj(hm, hg, x2d, gt, w_out, ln_g, ln_b, tm, tiles_per_mod):
    m, d = x2d.shape
    mod_rows = gt.shape[1]
    vec = pl.BlockSpec((1, d), lambda i: (0, 0))
    return pl.pallas_call(
        _outproj_kernel,
        grid=(m // tm,),
        in_specs=[pl.BlockSpec((tm, M_WIDTH), lambda i: (i, 0)),
                  pl.BlockSpec((tm, G_WIDTH), lambda i: (i, 0)),
                  pl.BlockSpec((tm, d), lambda i: (i, 0)),
                  pl.BlockSpec((1, mod_rows, d), lambda i: (i // tiles_per_mod, 0, 0)),
                  pl.BlockSpec((d, d), lambda i: (0, 0)),
                  vec, vec],
        out_specs=pl.BlockSpec((tm, d), lambda i: (i, 0)),
        out_shape=jax.ShapeDtypeStruct((m, d), F32),
        compiler_params=_cparams(("arbitrary",)),
        name="outproj",
    )(hm, hg, x2d, gt, w_out, ln_g.reshape(1, d), ln_b.reshape(1, d))


def _ffn_kernel(x_ref, sc_ref, sh_ref, gt_ref, wg_ref, wu_ref, wd_ref, g_ref, b_ref, o_ref, h_scr, acc):
    f = pl.program_id(1)

    @pl.when(f == 0)
    def _():
        h_scr[...] = (x_ref[...] * (1.0 + sc_ref[0]) + sh_ref[0]).astype(BF16)
        acc[...] = jnp.zeros_like(acc)

    h = h_scr[...]
    gate = _dot(h, wg_ref[...])
    up = _dot(h, wu_ref[...])
    act = (gate * jax.nn.sigmoid(gate) * up).astype(BF16)
    acc[...] += _dot(act, wd_ref[...])

    @pl.when(f == pl.num_programs(1) - 1)
    def _():
        y = DEEPNORM_ALPHA * x_ref[...] + (1.0 + gt_ref[0]) * acc[...]
        o_ref[...] = _layer_norm(y, g_ref[...], b_ref[...])


def _ffn(x2d, sc, sh, gt, w_gu, w_down, ln_g, ln_b, tm, tiles_per_mod, tf=512):
    m, d = x2d.shape
    nf = D_FF // tf
    mod_rows = sc.shape[1]
    mod_spec = pl.BlockSpec((1, mod_rows, d), lambda i, f: (i // tiles_per_mod, 0, 0))
    vec = pl.BlockSpec((1, d), lambda i, f: (0, 0))
    return pl.pallas_call(
        _ffn_kernel,
        grid=(m // tm, nf),
        in_specs=[pl.BlockSpec((tm, d), lambda i, f: (i, 0)),
                  mod_spec, mod_spec, mod_spec,
                  pl.BlockSpec((d, tf), lambda i, f: (0, f)),
                  pl.BlockSpec((d, tf), lambda i, f: (0, nf + f)),
                  pl.BlockSpec((tf, d), lambda i, f: (f, 0)),
                  vec, vec],
        out_specs=pl.BlockSpec((tm, d), lambda i, f: (i, 0)),
        out_shape=jax.ShapeDtypeStruct((m, d), F32),
        scratch_shapes=[pltpu.VMEM((tm, d), BF16), pltpu.VMEM((tm, d), F32)],
        compiler_params=_cparams(("arbitrary", "arbitrary")),
        name="ffn",
    )(x2d, sc, sh, gt, w_gu, w_gu, w_down, ln_g.reshape(1, d), ln_b.reshape(1, d))


def _layer(x, mods, weights, init_m, init_g, tm, tiles_per_mod):
    batch, seq, d = x.shape
    L = CHUNK if seq % CHUNK == 0 else seq
    nc = seq // L
    m = batch * seq
    sh1, sc1, gt1, sh2, sc2, gt2 = mods
    x2d = x.reshape(m, d)
    proj, gates = _inproj(x2d, sc1, sh1, weights["w_in_main"], weights["w_in_gate"], tm, tiles_per_mod)
    proj3 = proj.reshape(batch * nc, L, MAIN_COLS)
    gates3 = gates.reshape(batch * nc, L, GATE_PAD)
    gates_t = jnp.swapaxes(gates3, 1, 2)
    gvecs = weights["gvecs"]
    hm, c1, n1, m1 = _mlstm(proj3, gates3, gates_t, gvecs, weights["m_norm_g"], batch, nc, L, init_m)
    hg, s1 = _gdn(proj3, gates3, gates_t, gvecs, weights["conv_w"], weights["g_norm_g"], batch, nc, L, init_g)
    conv1 = proj.reshape(batch, seq, MAIN_COLS)[:, seq - (CONV_W - 1):, 4 * M_WIDTH:4 * M_WIDTH + 3 * G_WIDTH]
    x1 = _outproj(hm.reshape(m, M_WIDTH), hg.reshape(m, G_WIDTH), x2d, gt1, weights["w_out"],
                  weights["ln1_g"], weights["ln1_b"], tm, tiles_per_mod)
    y = _ffn(x1, sc2, sh2, gt2, weights["w_gu"], weights["w_down"], weights["ln2_g"], weights["ln2_b"],
             tm, tiles_per_mod)
    return y.reshape(batch, seq, d), c1, n1, m1[:, 0, :M_HEADS], s1, conv1


def kernel(x_prompt, x_sample, state_mlstm_C, state_mlstm_n, state_mlstm_m, state_gdn_S, state_gdn_conv,
           c_prompt, c_sample, w_ada, b_ada, w_in, m_i_bias, m_f_bias, m_norm_g, conv_w, g_dt_bias,
           g_A_log, g_norm_g, w_out, ln1_g, ln1_b, w_gu, w_down, ln2_g, ln2_b):
    bp, seq_p, d = x_prompt.shape
    bs, seq_s, _ = x_sample.shape

    ada = _ada(jnp.concatenate([c_prompt, c_sample], axis=0), w_ada, b_ada)
    ada6 = [ada[:, i * d:(i + 1) * d] for i in range(6)]

    tm_p, tm_s = 512, 256
    mods_p = [a[:bp].reshape(bp, 1, d) for a in ada6]
    mods_s = [jnp.repeat(a[bp:], seq_s, axis=0).reshape(bs * seq_s // tm_s, tm_s, d) for a in ada6]

    gate_bias = jnp.zeros((GATE_PAD,), F32)
    gate_bias = gate_bias.at[GI0:GI0 + M_HEADS].set(m_i_bias).at[GF0:GF0 + M_HEADS].set(m_f_bias)
    gate_bias = gate_bias.at[GA0:GA0 + G_HEADS].set(g_dt_bias)
    a_log = jnp.zeros((GATE_PAD,), F32).at[GA0:GA0 + G_HEADS].set(g_A_log)
    weights = {
        "w_in_main": w_in[:, :MAIN_COLS].astype(BF16),
        "w_in_gate": jnp.pad(w_in[:, MAIN_COLS:], ((0, 0), (0, GATE_PAD - GATE_COLS))).astype(BF16),
        "gvecs": (gate_bias.reshape(1, GATE_PAD), gate_bias.reshape(GATE_PAD, 1),
                  a_log.reshape(1, GATE_PAD), a_log.reshape(GATE_PAD, 1)),
        "m_norm_g": m_norm_g, "conv_w": conv_w, "g_norm_g": g_norm_g,
        "w_out": w_out.astype(BF16), "ln1_g": ln1_g, "ln1_b": ln1_b,
        "w_gu": w_gu.astype(BF16), "w_down": w_down.astype(BF16), "ln2_g": ln2_g, "ln2_b": ln2_b,
    }

    y_p, p_c, p_n, p_m, p_s, p_conv = _layer(x_prompt, mods_p, weights, None, None, tm_p, seq_p // tm_p)
    m0 = jnp.pad(state_mlstm_m, ((0, 0), (0, GATE_PAD - M_HEADS))).reshape(bs, 1, GATE_PAD)
    y_s, s_c, s_n, s_m, s_s, s_conv = _layer(
        x_sample, mods_s, weights, (state_mlstm_C, state_mlstm_n, m0), (state_gdn_S, state_gdn_conv), tm_s, 1)
    return (y_p, y_s, p_c, p_n, p_m, p_s, p_conv, s_c, s_n, s_m, s_s, s_conv)
```

```python
import functools

import jax
import jax.numpy as jnp
from jax import lax
from jax.experimental import pallas as pl
from jax.experimental.pallas import tpu as pltpu

F32 = jnp.float32
BF16 = jnp.bfloat16

D_MODEL = 2048
M_HEADS = 4
M_HEAD_DIM = 256
M_WIDTH = M_HEADS * M_HEAD_DIM
G_HEADS = 8
G_HEAD_DIM = 128
G_WIDTH = G_HEADS * G_HEAD_DIM
CONV_W = 4
CHUNK = 64
D_FF = 5632
MAIN_COLS = 4 * M_WIDTH + 3 * G_WIDTH + G_WIDTH
GATE_COLS = 2 * M_HEADS + 2 * G_HEADS
GATE_PAD = 128
DEEPNORM_ALPHA = 2.0 ** 0.25
LN_EPS = 1e-5
RMS_EPS = 1e-6
GI0, GF0, GB0, GA0 = 0, M_HEADS, 2 * M_HEADS, 2 * M_HEADS + G_HEADS

SUBLANES = 8
VMEM_LIMIT_BYTES = 56 * 1024 * 1024
TM_PROMPT, TM_SAMPLE = 512, 256
TN_INPROJ, TN_ADA, TF_FFN = 1024, 1024, 512
SEQS_PER_STEP_SAMPLE = 4
SMALL_L = 8


def _cparams(sem):
    return pltpu.CompilerParams(dimension_semantics=sem, vmem_limit_bytes=VMEM_LIMIT_BYTES)


def _dot(a, b):
    return jnp.dot(a, b, preferred_element_type=F32)


def _dot_nt(a, b):
    return lax.dot_general(a, b, (((1,), (1,)), ((), ())), preferred_element_type=F32)


def _dot_tn(a, b):
    return lax.dot_general(a, b, (((0,), (0,)), ((), ())), preferred_element_type=F32)


def _split2(x):
    hi = x.astype(BF16)
    return hi, (x - hi.astype(F32)).astype(BF16)


def _split3(x):
    hi = x.astype(BF16)
    r = x - hi.astype(F32)
    mid = r.astype(BF16)
    return hi, mid, (r - mid.astype(F32)).astype(BF16)


def _dot3(a, b):
    return _dot(a[0], b[0]) + (_dot(a[0], b[1]) + _dot(a[1], b[0]))


def _mm_small(a, b):
    out = a[:, 0:1] * b[0:1, :]
    for i in range(1, a.shape[1]):
        out = out + a[:, i:i + 1] * b[i:i + 1, :]
    return out


def _ada_kernel(c_ref, w_ref, b_ref, o_ref):
    c = c_ref[...]
    a = (c * jax.nn.sigmoid(c)).astype(BF16)
    o_ref[...] = _dot(a, w_ref[...].astype(BF16)) + b_ref[...]


def _ada(c_all, w_ada, b_ada):
    n_rows, d = c_all.shape
    n_cols = w_ada.shape[1]
    tn = TN_ADA
    return pl.pallas_call(
        _ada_kernel,
        grid=(n_cols // tn,),
        in_specs=[pl.BlockSpec((n_rows, d), lambda j: (0, 0)),
                  pl.BlockSpec((d, tn), lambda j: (0, j)),
                  pl.BlockSpec((1, tn), lambda j: (0, j))],
        out_specs=pl.BlockSpec((n_rows, tn), lambda j: (0, j)),
        out_shape=jax.ShapeDtypeStruct((n_rows, n_cols), F32),
        compiler_params=_cparams(("arbitrary",)),
        name="ada",
    )(c_all, w_ada, b_ada.reshape(1, n_cols))


def _inproj_kernel(x_ref, sc_ref, sh_ref, w_ref, wg_ref, o_ref, og_ref, h_scr):
    @pl.when(pl.program_id(1) == 0)
    def _():
        h = (x_ref[...] * (1.0 + sc_ref[0]) + sh_ref[0]).astype(BF16)
        h_scr[...] = h
        og_ref[...] = _dot(h, wg_ref[...])

    o_ref[...] = _dot(h_scr[...], w_ref[...])


def _inproj(x2d, sc, sh, w_main, w_gate, tm, tiles_per_mod):
    m, d = x2d.shape
    tn = TN_INPROJ
    mod_rows = sc.shape[1]
    mod_spec = pl.BlockSpec((1, mod_rows, d), lambda i, j: (i // tiles_per_mod, 0, 0))
    return pl.pallas_call(
        _inproj_kernel,
        grid=(m // tm, MAIN_COLS // tn),
        in_specs=[pl.BlockSpec((tm, d), lambda i, j: (i, 0)),
                  mod_spec, mod_spec,
                  pl.BlockSpec((d, tn), lambda i, j: (0, j)),
                  pl.BlockSpec((d, GATE_PAD), lambda i, j: (0, 0))],
        out_specs=[pl.BlockSpec((tm, tn), lambda i, j: (i, j)),
                   pl.BlockSpec((tm, GATE_PAD), lambda i, j: (i, 0))],
        out_shape=[jax.ShapeDtypeStruct((m, MAIN_COLS), F32),
                   jax.ShapeDtypeStruct((m, GATE_PAD), F32)],
        scratch_shapes=[pltpu.VMEM((tm, d), BF16)],
        compiler_params=_cparams(("arbitrary", "arbitrary")),
        name="inproj",
    )(x2d, sc, sh, w_main, w_gate)


def _gate_tables(g, bias, alog, gid):
    x = g + bias
    is_f = (gid >= GF0) & (gid < GB0)
    is_a = (gid >= GA0) & (gid < GA0 + G_HEADS)
    log_f = jax.nn.log_sigmoid(x)
    log_a = -jnp.exp(alog) * jax.nn.softplus(x)
    inc = jnp.where(is_f, log_f, jnp.where(is_a, log_a, 0.0))
    return x, inc, jax.nn.sigmoid(x)


def _chunk_masks(L):
    row = lax.broadcasted_iota(jnp.int32, (L, L), 0)
    col = lax.broadcasted_iota(jnp.int32, (L, L), 1)
    return row >= col, row > col, row == col


def _gates_both_forms(g_col, g_row, gb_row_ref, gb_col_ref, al_row_ref, al_col_ref, L):
    causal, _, _ = _chunk_masks(L)
    tril = causal.astype(F32)
    triu = (lax.broadcasted_iota(jnp.int32, (L, L), 0) <= lax.broadcasted_iota(jnp.int32, (L, L), 1)).astype(F32)
    gid_c = lax.broadcasted_iota(jnp.int32, (L, GATE_PAD), 1)
    gid_r = lax.broadcasted_iota(jnp.int32, (GATE_PAD, L), 0)
    x_c, inc_c, beta_c = _gate_tables(g_col, gb_row_ref[...], al_row_ref[...], gid_c)
    x_r, inc_r, _ = _gate_tables(g_row, gb_col_ref[...], al_col_ref[...], gid_r)
    if L <= SMALL_L:
        cum_c = _mm_small(tril, inc_c)
        cum_r = _mm_small(inc_r, triu)
    else:
        tril_b, triu_b = tril.astype(BF16), triu.astype(BF16)
        c1, c2, c3 = _split3(inc_c)
        r1, r2, r3 = _split3(inc_r)
        cum_c = _dot(tril_b, c1) + (_dot(tril_b, c2) + _dot(tril_b, c3))
        cum_r = _dot(r1, triu_b) + (_dot(r2, triu_b) + _dot(r3, triu_b))
    return x_c, cum_c, beta_c, x_r, cum_r


def _rms_gate(h, gain, gate):
    return h * lax.rsqrt(jnp.mean(h * h, axis=-1, keepdims=True) + RMS_EPS) * gain * gate


def _mlstm_kernel(*refs, L, nb, has_init):
    if has_init:
        (q_ref, k_ref, v_ref, o_ref, gcol_ref, grow_ref, gb_row_ref, gb_col_ref, al_row_ref, al_col_ref,
         ng_ref, c0_ref, n0_ref, m0_ref, h_ref, c_ref, n_ref, m_ref) = refs
    else:
        (q_ref, k_ref, v_ref, o_ref, gcol_ref, grow_ref, gb_row_ref, gb_col_ref, al_row_ref, al_col_ref,
         ng_ref, h_ref, c_ref, n_ref, m_ref) = refs

    @pl.when(pl.program_id(1) == 0)
    def _():
        if has_init:
            c_ref[...] = c0_ref[...]
            n_ref[...] = n0_ref[...]
            m_ref[...] = m0_ref[...]
        else:
            c_ref[...] = jnp.zeros_like(c_ref)
            n_ref[...] = jnp.zeros_like(n_ref)
            m_ref[...] = jnp.zeros_like(m_ref)

    causal, _, _ = _chunk_masks(L)
    lane = lax.broadcasted_iota(jnp.int32, (1, GATE_PAD), 1)
    scale = M_HEAD_DIM ** -0.5
    gates = [_gates_both_forms(gcol_ref[b * L:(b + 1) * L, :], grow_ref[b], gb_row_ref, gb_col_ref,
                               al_row_ref, al_col_ref, L) for b in range(nb)]
    m_all = [m_ref[b] for b in range(nb)]
    probs = [(b, h) for b in range(nb) for h in range(M_HEADS)]

    def rows(b):
        return slice(b * L, (b + 1) * L)

    def cols(h):
        return slice(h * M_HEAD_DIM, (h + 1) * M_HEAD_DIM)

    q = [q_ref[rows(b), cols(h)] for b, h in probs]
    k = [k_ref[rows(b), cols(h)] * scale for b, h in probs]
    vb = [v_ref[rows(b), cols(h)].astype(BF16) for b, h in probs]
    qb = [x.astype(BF16) for x in q]
    kb = [x.astype(BF16) for x in k]
    ig_c = [gates[b][0][:, GI0 + h:GI0 + h + 1] for b, h in probs]
    ig_r = [gates[b][3][GI0 + h:GI0 + h + 1, :] for b, h in probs]
    bt_c = [gates[b][1][:, GF0 + h:GF0 + h + 1] for b, h in probs]
    bt_r = [gates[b][4][GF0 + h:GF0 + h + 1, :] for b, h in probs]
    m0 = [jnp.sum(jnp.where(lane == h, m_all[b], 0.0), axis=1, keepdims=True) for b, h in probs]
    n_p = len(probs)
    rng = range(n_p)

    log_d = [jnp.where(causal, bt_c[i] - bt_r[i] + ig_r[i], -jnp.inf) for i in rng]
    inter = [bt_c[i] + m0[i] for i in rng]
    m_t = [jnp.maximum(inter[i], jnp.max(log_d[i], axis=1, keepdims=True)) for i in rng]
    inter_w = [jnp.exp(inter[i] - m_t[i]) for i in rng]
    qk = [_dot_nt(qb[i], kb[i]) for i in rng]
    c_old = [c_ref[b, h] for b, h in probs]
    n_old = [n_ref[b, h:h + 1, :] for b, h in probs]
    qc = [_dot(qb[i], c_old[i].astype(BF16)) for i in rng]
    s = [qk[i] * jnp.exp(log_d[i] - m_t[i]) for i in rng]
    sv = [_dot(s[i].astype(BF16), vb[i]) for i in rng]

    b_last = [bt_c[i][L - 1:L, :] for i in rng]
    m_new = [jnp.maximum(b_last[i] + m0[i], jnp.max(b_last[i] - bt_r[i] + ig_r[i], axis=1, keepdims=True))
             for i in rng]
    kw = [k[i] * jnp.exp(b_last[i] - bt_c[i] + ig_c[i] - m_new[i]) for i in rng]
    decay = [jnp.exp(b_last[i] + m0[i] - m_new[i]) for i in rng]
    kv = [_dot_tn(kw[i].astype(BF16), vb[i]) for i in rng]

    m_next = list(m_all)
    for i, (b, h) in enumerate(probs):
        num = inter_w[i] * qc[i] + sv[i]
        den = inter_w[i] * jnp.sum(q[i] * n_old[i], axis=1, keepdims=True) + jnp.sum(s[i], axis=1, keepdims=True)
        hh = num / jnp.maximum(jnp.abs(den), jnp.exp(-m_t[i]))
        c_ref[b, h] = decay[i] * c_old[i] + kv[i]
        n_ref[b, h:h + 1, :] = decay[i] * n_old[i] + jnp.sum(kw[i], axis=0, keepdims=True)
        m_next[b] = jnp.where(lane == h, m_new[i], m_next[b])
        gate = jax.nn.sigmoid(o_ref[rows(b), cols(h)])
        h_ref[rows(b), cols(h)] = _rms_gate(hh, ng_ref[:, cols(h)], gate).astype(h_ref.dtype)
    for b in range(nb):
        m_ref[b] = m_next[b]


def _row_spec(rows, width, col_block, nc):
    return pl.BlockSpec((rows, width), lambda b, c: (b * nc + c, col_block))


def _gate_specs(L, nb, nc):
    vec_r = pl.BlockSpec((1, GATE_PAD), lambda b, c: (0, 0))
    vec_c = pl.BlockSpec((GATE_PAD, 1), lambda b, c: (0, 0))
    return [pl.BlockSpec((nb * L, GATE_PAD), lambda b, c: (b * nc + c, 0)),
            pl.BlockSpec((nb, GATE_PAD, L), lambda b, c: (b * nc + c, 0, 0)),
            vec_r, vec_c, vec_r, vec_c]


def _scan_out_dtype(L):
    return BF16 if L % 16 == 0 else F32


def _mlstm(proj, gates, gates_t, gvecs, norm_g, batch, nc, L, nb, init):
    has_init = init is not None
    m = proj.shape[0]
    in_specs = [_row_spec(nb * L, M_WIDTH, j, nc) for j in range(4)] + _gate_specs(L, nb, nc)
    in_specs.append(pl.BlockSpec((1, M_WIDTH), lambda b, c: (0, 0)))
    c_spec = pl.BlockSpec((nb, M_HEADS, M_HEAD_DIM, M_HEAD_DIM), lambda b, c: (b, 0, 0, 0))
    n_spec = pl.BlockSpec((nb, M_HEADS, M_HEAD_DIM), lambda b, c: (b, 0, 0))
    m_spec = pl.BlockSpec((nb, 1, GATE_PAD), lambda b, c: (b, 0, 0))
    args = [proj, proj, proj, proj, gates, gates_t, *gvecs, norm_g.reshape(1, M_WIDTH)]
    if has_init:
        in_specs += [c_spec, n_spec, m_spec]
        args += list(init)
    return pl.pallas_call(
        functools.partial(_mlstm_kernel, L=L, nb=nb, has_init=has_init),
        grid=(batch // nb, nc),
        in_specs=in_specs,
        out_specs=[_row_spec(nb * L, M_WIDTH, 0, nc), c_spec, n_spec, m_spec],
        out_shape=[jax.ShapeDtypeStruct((m, M_WIDTH), _scan_out_dtype(L)),
                   jax.ShapeDtypeStruct((batch, M_HEADS, M_HEAD_DIM, M_HEAD_DIM), F32),
                   jax.ShapeDtypeStruct((batch, M_HEADS, M_HEAD_DIM), F32),
                   jax.ShapeDtypeStruct((batch, 1, GATE_PAD), F32)],
        compiler_params=_cparams(("arbitrary", "arbitrary")),
        name="mlstm",
    )(*args)


def _inv_unit_lower(a_list, L):
    _, _, eye = _chunk_masks(L)
    eye = eye.astype(F32)
    levels = max(1, (L - 1).bit_length())
    p = [-a for a in a_list]
    t = [eye + x for x in p]
    if L <= SMALL_L:
        for _ in range(levels - 1):
            p = [_mm_small(x, x) for x in p]
            t = [ti + _mm_small(ti, pi) for ti, pi in zip(t, p)]
        return t
    ps = [_split2(x) for x in p]
    for _ in range(levels - 1):
        ps = [_split2(_dot3(x, x)) for x in ps]
        ts = [_split2(x) for x in t]
        t = [ti + _dot3(tsi, psi) for ti, tsi, psi in zip(t, ts, ps)]
    return [_split2(x) for x in t]


def _gdn_kernel(*refs, L, nb, nc, has_init):
    if has_init:
        (xq_ref, xk_ref, xv_ref, z_ref, gcol_ref, grow_ref, gb_row_ref, gb_col_ref, al_row_ref, al_col_ref,
         cw_ref, ng_ref, s0_ref, conv0_ref, h_ref, s_ref, buf) = refs
    else:
        (xq_ref, xk_ref, xv_ref, z_ref, gcol_ref, grow_ref, gb_row_ref, gb_col_ref, al_row_ref, al_col_ref,
         cw_ref, ng_ref, h_ref, s_ref, buf) = refs
    hist = CONV_W - 1
    base = SUBLANES - hist

    def rows(b):
        return slice(b * L, (b + 1) * L)

    def cols(h):
        return slice(h * G_HEAD_DIM, (h + 1) * G_HEAD_DIM)

    @pl.when(pl.program_id(1) == 0)
    def _():
        buf[:, 0:SUBLANES, :] = jnp.zeros((nb, SUBLANES, 3 * G_WIDTH), F32)
        if has_init:
            s_ref[...] = s0_ref[...]
            for b in range(nb):
                buf[b, base:SUBLANES, :] = conv0_ref[b]
        else:
            s_ref[...] = jnp.zeros_like(s_ref)

    act = []
    for b in range(nb):
        buf[b, SUBLANES:SUBLANES + L, 0:G_WIDTH] = xq_ref[rows(b), :]
        buf[b, SUBLANES:SUBLANES + L, G_WIDTH:2 * G_WIDTH] = xk_ref[rows(b), :]
        buf[b, SUBLANES:SUBLANES + L, 2 * G_WIDTH:3 * G_WIDTH] = xv_ref[rows(b), :]
        y = cw_ref[0:1, :] * buf[b, base:base + L, :]
        for j in range(1, CONV_W):
            y = y + cw_ref[j:j + 1, :] * buf[b, base + j:base + j + L, :]
        if nc > 1:
            buf[b, 0:SUBLANES, :] = buf[b, L:L + SUBLANES, :]
        act.append(y * jax.nn.sigmoid(y))

    causal, strict, _ = _chunk_masks(L)
    scale = G_HEAD_DIM ** -0.5
    gates = [_gates_both_forms(gcol_ref[rows(b), :], grow_ref[b], gb_row_ref, gb_col_ref,
                               al_row_ref, al_col_ref, L) for b in range(nb)]
    probs = [(b, h) for b in range(nb) for h in range(G_HEADS)]
    rng = range(len(probs))

    def unit(x):
        return x * lax.rsqrt(jnp.sum(x * x, axis=-1, keepdims=True) + RMS_EPS)

    q = [unit(act[b][:, h * G_HEAD_DIM:(h + 1) * G_HEAD_DIM]) * scale for b, h in probs]
    k = [unit(act[b][:, G_WIDTH + h * G_HEAD_DIM:G_WIDTH + (h + 1) * G_HEAD_DIM]) for b, h in probs]
    v = [act[b][:, 2 * G_WIDTH + h * G_HEAD_DIM:2 * G_WIDTH + (h + 1) * G_HEAD_DIM] for b, h in probs]
    qb = [x.astype(BF16) for x in q]
    kb = [x.astype(BF16) for x in k]
    b_c = [gates[b][1][:, GA0 + h:GA0 + h + 1] for b, h in probs]
    b_r = [gates[b][4][GA0 + h:GA0 + h + 1, :] for b, h in probs]
    bet = [gates[b][2][:, GB0 + h:GB0 + h + 1] for b, h in probs]
    decay = [jnp.exp(jnp.where(causal, b_c[i] - b_r[i], -jnp.inf)) for i in rng]
    eb = [jnp.exp(b_c[i]) for i in rng]
    kk = [_dot_nt(kb[i], kb[i]) for i in rng]
    qk = [_dot_nt(qb[i], kb[i]) * decay[i] for i in rng]
    s_old = [s_ref[b, h] for b, h in probs]
    sb = [x.astype(BF16) for x in s_old]
    qs = [_dot(qb[i], sb[i]) for i in rng]

    t = _inv_unit_lower([jnp.where(strict, bet[i] * kk[i] * decay[i], 0.0) for i in rng], L)
    rhs = [jnp.concatenate([bet[i] * v[i], (bet[i] * eb[i]) * k[i]], axis=-1) for i in rng]
    if L <= SMALL_L:
        sol = [_mm_small(t[i], rhs[i]) for i in rng]
    else:
        sol = [_dot3(t[i], _split2(rhs[i])) for i in rng]
    u = [sol[i][:, 0:G_HEAD_DIM] - _dot(sol[i][:, G_HEAD_DIM:2 * G_HEAD_DIM].astype(BF16), sb[i]) for i in rng]
    ub = [x.astype(BF16) for x in u]
    o = [eb[i] * qs[i] + _dot(qk[i].astype(BF16), ub[i]) for i in rng]
    b_last = [b_c[i][L - 1:L, :] for i in rng]
    wk = [(jnp.exp(b_last[i] - b_c[i]) * k[i]).astype(BF16) for i in rng]
    ds = [_dot_tn(wk[i], ub[i]) for i in rng]
    for i, (b, h) in enumerate(probs):
        s_ref[b, h] = jnp.exp(b_last[i]) * s_old[i] + ds[i]
        z = z_ref[rows(b), cols(h)]
        h_ref[rows(b), cols(h)] = _rms_gate(o[i], ng_ref[:, cols(h)], z * jax.nn.sigmoid(z)).astype(h_ref.dtype)


def _gdn(proj, gates, gates_t, gvecs, conv_w, norm_g, batch, nc, L, nb, init):
    has_init = init is not None
    m = proj.shape[0]
    blk0 = 4 * M_WIDTH // G_WIDTH
    in_specs = [_row_spec(nb * L, G_WIDTH, blk0 + j, nc) for j in range(4)] + _gate_specs(L, nb, nc)
    in_specs += [pl.BlockSpec((CONV_W, 3 * G_WIDTH), lambda b, c: (0, 0)),
                 pl.BlockSpec((1, G_WIDTH), lambda b, c: (0, 0))]
    s_spec = pl.BlockSpec((nb, G_HEADS, G_HEAD_DIM, G_HEAD_DIM), lambda b, c: (b, 0, 0, 0))
    args = [proj, proj, proj, proj, gates, gates_t, *gvecs, conv_w, norm_g.reshape(1, G_WIDTH)]
    if has_init:
        in_specs += [s_spec, pl.BlockSpec((nb, CONV_W - 1, 3 * G_WIDTH), lambda b, c: (b, 0, 0))]
        args += list(init)
    buf_rows = SUBLANES + -(-L // SUBLANES) * SUBLANES
    return pl.pallas_call(
        functools.partial(_gdn_kernel, L=L, nb=nb, nc=nc, has_init=has_init),
        grid=(batch // nb, nc),
        in_specs=in_specs,
        out_specs=[_row_spec(nb * L, G_WIDTH, 0, nc), s_spec],
        out_shape=[jax.ShapeDtypeStruct((m, G_WIDTH), _scan_out_dtype(L)),
                   jax.ShapeDtypeStruct((batch, G_HEADS, G_HEAD_DIM, G_HEAD_DIM), F32)],
        scratch_shapes=[pltpu.VMEM((nb, buf_rows, 3 * G_WIDTH), F32)],
        compiler_params=_cparams(("arbitrary", "arbitrary")),
        name="gdn",
    )(*args)


def _layer_norm(y, g, b):
    mu = jnp.mean(y, axis=-1, keepdims=True)
    yc = y - mu
    var = jnp.mean(yc * yc, axis=-1, keepdims=True)
    return yc * lax.rsqrt(var + LN_EPS) * g + b


def _outproj_kernel(hm_ref, hg_ref, x_ref, gt_ref, w_ref, g_ref, b_ref, o_ref):
    mix = (_dot(hm_ref[...].astype(BF16), w_ref[0:M_WIDTH, :])
           + _dot(hg_ref[...].astype(BF16), w_ref[M_WIDTH:M_WIDTH + G_WIDTH, :]))
    y = DEEPNORM_ALPHA * x_ref[...] + (1.0 + gt_ref[0]) * mix
    o_ref[...] = _layer_norm(y, g_ref[...], b_ref[...])


def _outproj(hm, hg, x2d, gt, w_out, ln_g, ln_b, tm, tiles_per_mod):
    m, d = x2d.shape
    mod_rows = gt.shape[1]
    vec = pl.BlockSpec((1, d), lambda i: (0, 0))
    return pl.pallas_call(
        _outproj_kernel,
        grid=(m // tm,),
        in_specs=[pl.BlockSpec((tm, M_WIDTH), lambda i: (i, 0)),
                  pl.BlockSpec((tm, G_WIDTH), lambda i: (i, 0)),
                  pl.BlockSpec((tm, d), lambda i: (i, 0)),
                  pl.BlockSpec((1, mod_rows, d), lambda i: (i // tiles_per_mod, 0, 0)),
                  pl.BlockSpec((d, d), lambda i: (0, 0)),
                  vec, vec],
        out_specs=pl.BlockSpec((tm, d), lambda i: (i, 0)),
        out_shape=jax.ShapeDtypeStruct((m, d), F32),
        compiler_params=_cparams(("arbitrary",)),
        name="outproj",
    )(hm, hg, x2d, gt, w_out, ln_g.reshape(1, d), ln_b.reshape(1, d))


def _ffn_kernel(x_ref, sc_ref, sh_ref, gt_ref, wg_ref, wu_ref, wd_ref, g_ref, b_ref, o_ref, h_scr, acc):
    f = pl.program_id(1)

    @pl.when(f == 0)
    def _():
        h_scr[...] = (x_ref[...] * (1.0 + sc_ref[0]) + sh_ref[0]).astype(BF16)
        acc[...] = jnp.zeros_like(acc)

    h = h_scr[...]
    gate = _dot(h, wg_ref[...])
    up = _dot(h, wu_ref[...])
    act = (gate * jax.nn.sigmoid(gate) * up).astype(BF16)
    acc[...] += _dot(act, wd_ref[...])

    @pl.when(f == pl.num_programs(1) - 1)
    def _():
        y = DEEPNORM_ALPHA * x_ref[...] + (1.0 + gt_ref[0]) * acc[...]
        o_ref[...] = _layer_norm(y, g_ref[...], b_ref[...])


def _ffn(x2d, sc, sh, gt, w_gu, w_down, ln_g, ln_b, tm, tiles_per_mod):
    m, d = x2d.shape
    tf = TF_FFN
    nf = D_FF // tf
    mod_rows = sc.shape[1]
    mod_spec = pl.BlockSpec((1, mod_rows, d), lambda i, f: (i // tiles_per_mod, 0, 0))
    vec = pl.BlockSpec((1, d), lambda i, f: (0, 0))
    return pl.pallas_call(
        _ffn_kernel,
        grid=(m // tm, nf),
        in_specs=[pl.BlockSpec((tm, d), lambda i, f: (i, 0)),
                  mod_spec, mod_spec, mod_spec,
                  pl.BlockSpec((d, tf), lambda i, f: (0, f)),
                  pl.BlockSpec((d, tf), lambda i, f: (0, nf + f)),
                  pl.BlockSpec((tf, d), lambda i, f: (f, 0)),
                  vec, vec],
        out_specs=pl.BlockSpec((tm, d), lambda i, f: (i, 0)),
        out_shape=jax.ShapeDtypeStruct((m, d), F32),
        scratch_shapes=[pltpu.VMEM((tm, d), BF16), pltpu.VMEM((tm, d), F32)],
        compiler_params=_cparams(("arbitrary", "arbitrary")),
        name="ffn",
    )(x2d, sc, sh, gt, w_gu, w_gu, w_down, ln_g.reshape(1, d), ln_b.reshape(1, d))


def _layer(x, mods, weights, init_m, init_g, tm, tiles_per_mod, nb):
    batch, seq, d = x.shape
    L = CHUNK if seq % CHUNK == 0 else seq
    nc = seq // L
    m = batch * seq
    sh1, sc1, gt1, sh2, sc2, gt2 = mods
    x2d = x.reshape(m, d)
    proj, gates = _inproj(x2d, sc1, sh1, weights["w_in_main"], weights["w_in_gate"], tm, tiles_per_mod)
    gates_t = jnp.swapaxes(gates.reshape(m // L, L, GATE_PAD), 1, 2)
    gvecs = weights["gvecs"]
    hm, c1, n1, m1 = _mlstm(proj, gates, gates_t, gvecs, weights["m_norm_g"], batch, nc, L, nb, init_m)
    hg, s1 = _gdn(proj, gates, gates_t, gvecs, weights["conv_w"], weights["g_norm_g"], batch, nc, L, nb, init_g)
    conv1 = proj.reshape(batch, seq, MAIN_COLS)[:, seq - (CONV_W - 1):, 4 * M_WIDTH:4 * M_WIDTH + 3 * G_WIDTH]
    x1 = _outproj(hm, hg, x2d, gt1, weights["w_out"], weights["ln1_g"], weights["ln1_b"], tm, tiles_per_mod)
    y = _ffn(x1, sc2, sh2, gt2, weights["w_gu"], weights["w_down"], weights["ln2_g"], weights["ln2_b"],
             tm, tiles_per_mod)
    return y.reshape(batch, seq, d), c1, n1, m1[:, 0, :M_HEADS], s1, conv1


def kernel(x_prompt, x_sample, state_mlstm_C, state_mlstm_n, state_mlstm_m, state_gdn_S, state_gdn_conv,
           c_prompt, c_sample, w_ada, b_ada, w_in, m_i_bias, m_f_bias, m_norm_g, conv_w, g_dt_bias,
           g_A_log, g_norm_g, w_out, ln1_g, ln1_b, w_gu, w_down, ln2_g, ln2_b):
    bp, seq_p, d = x_prompt.shape
    bs, seq_s, _ = x_sample.shape

    ada = _ada(jnp.concatenate([c_prompt, c_sample], axis=0), w_ada, b_ada)
    ada6 = [ada[:, i * d:(i + 1) * d] for i in range(6)]

    tm_p, tm_s = TM_PROMPT, TM_SAMPLE
    mods_p = [a[:bp].reshape(bp, 1, d) for a in ada6]
    mods_s = [jnp.repeat(a[bp:], seq_s, axis=0).reshape(bs * seq_s // tm_s, tm_s, d) for a in ada6]

    gate_bias = jnp.zeros((GATE_PAD,), F32)
    gate_bias = gate_bias.at[GI0:GI0 + M_HEADS].set(m_i_bias).at[GF0:GF0 + M_HEADS].set(m_f_bias)
    gate_bias = gate_bias.at[GA0:GA0 + G_HEADS].set(g_dt_bias)
    a_log = jnp.zeros((GATE_PAD,), F32).at[GA0:GA0 + G_HEADS].set(g_A_log)
    weights = {
        "w_in_main": w_in[:, :MAIN_COLS].astype(BF16),
        "w_in_gate": jnp.pad(w_in[:, MAIN_COLS:], ((0, 0), (0, GATE_PAD - GATE_COLS))).astype(BF16),
        "gvecs": (gate_bias.reshape(1, GATE_PAD), gate_bias.reshape(GATE_PAD, 1),
                  a_log.reshape(1, GATE_PAD), a_log.reshape(GATE_PAD, 1)),
        "m_norm_g": m_norm_g, "conv_w": conv_w, "g_norm_g": g_norm_g,
        "w_out": w_out.astype(BF16), "ln1_g": ln1_g, "ln1_b": ln1_b,
        "w_gu": w_gu.astype(BF16), "w_down": w_down.astype(BF16), "ln2_g": ln2_g, "ln2_b": ln2_b,
    }

    y_p, p_c, p_n, p_m, p_s, p_conv = _layer(x_prompt, mods_p, weights, None, None, tm_p, seq_p // tm_p, 1)
    m0 = jnp.pad(state_mlstm_m, ((0, 0), (0, GATE_PAD - M_HEADS))).reshape(bs, 1, GATE_PAD)
    y_s, s_c, s_n, s_m, s_s, s_conv = _layer(
        x_sample, mods_s, weights, (state_mlstm_C, state_mlstm_n, m0), (state_gdn_S, state_gdn_conv), tm_s, 1,
        SEQS_PER_STEP_SAMPLE)
    return (y_p, y_s, p_c, p_n, p_m, p_s, p_conv, s_c, s_n, s_m, s_s, s_conv)
```

```python
import functools

import jax
import jax.numpy as jnp
from jax import lax
from jax.experimental import pallas as pl
from jax.experimental.pallas import tpu as pltpu

F32 = jnp.float32
BF16 = jnp.bfloat16

D_MODEL = 2048
M_HEADS = 4
M_HEAD_DIM = 256
M_WIDTH = M_HEADS * M_HEAD_DIM
G_HEADS = 8
G_HEAD_DIM = 128
G_WIDTH = G_HEADS * G_HEAD_DIM
CONV_W = 4
CHUNK = 64
D_FF = 5632
MAIN_COLS = 4 * M_WIDTH + 3 * G_WIDTH + G_WIDTH
GATE_COLS = 2 * M_HEADS + 2 * G_HEADS
GATE_PAD = 128
DEEPNORM_ALPHA = 2.0 ** 0.25
LN_EPS = 1e-5
RMS_EPS = 1e-6
GI0, GF0, GB0, GA0 = 0, M_HEADS, 2 * M_HEADS, 2 * M_HEADS + G_HEADS

SUBLANES = 8
VMEM_LIMIT_BYTES = 56 * 1024 * 1024
TM_INPROJ, TM_OUTPROJ, TM_FFN = 1024, 512, 512
TN_INPROJ, TN_ADA, TF_FFN = 512, 1024, 512
CAST_ROWS, CAST_COLS = 2048, 1024
SEQS_PER_STEP_PROMPT = {"mlstm": 1, "gdn": 2}
SEQS_PER_STEP_SAMPLE = {"mlstm": 4, "gdn": 4}
SMALL_L = 8


def _cparams(sem):
    return pltpu.CompilerParams(dimension_semantics=sem, vmem_limit_bytes=VMEM_LIMIT_BYTES)


def _dot(a, b):
    return jnp.dot(a, b, preferred_element_type=F32)


def _dot_nt(a, b):
    return lax.dot_general(a, b, (((1,), (1,)), ((), ())), preferred_element_type=F32)


def _dot_tn(a, b):
    return lax.dot_general(a, b, (((0,), (0,)), ((), ())), preferred_element_type=F32)


def _split2(x):
    hi = x.astype(BF16)
    return hi, (x - hi.astype(F32)).astype(BF16)


def _split3(x):
    hi = x.astype(BF16)
    r = x - hi.astype(F32)
    mid = r.astype(BF16)
    return hi, mid, (r - mid.astype(F32)).astype(BF16)


def _dot3(a, b):
    return _dot(a[0], b[0]) + (_dot(a[0], b[1]) + _dot(a[1], b[0]))


def _mm_small(a, b):
    out = a[:, 0:1] * b[0:1, :]
    for i in range(1, a.shape[1]):
        out = out + a[:, i:i + 1] * b[i:i + 1, :]
    return out


def _tok_get(ref, b, L, cols):
    if len(ref.shape) == 3:
        return ref[b, :, cols]
    return ref[b * L:(b + 1) * L, cols]


def _tok_set(ref, b, L, cols, val):
    if len(ref.shape) == 3:
        ref[b, :, cols] = val
    else:
        ref[b * L:(b + 1) * L, cols] = val


ALL = slice(None)


def _cast_kernel(x_ref, o_ref):
    o_ref[...] = x_ref[...].astype(BF16)


def _cast_bf16(w, rows, cols):
    bc = min(cols, CAST_COLS)
    br = max(r for r in range(SUBLANES, min(rows, CAST_ROWS) + 1, SUBLANES) if rows % r == 0)
    return pl.pallas_call(
        _cast_kernel,
        grid=(rows // br, cols // bc),
        in_specs=[pl.BlockSpec((br, bc), lambda i, j: (i, j))],
        out_specs=pl.BlockSpec((br, bc), lambda i, j: (i, j)),
        out_shape=jax.ShapeDtypeStruct((rows, cols), BF16),
        compiler_params=_cparams(("arbitrary", "arbitrary")),
        name="cast",
    )(w)


def _ada_kernel(c_ref, w_ref, b_ref, o_ref):
    c = c_ref[...]
    a = (c * jax.nn.sigmoid(c)).astype(BF16)
    o_ref[...] = _dot(a, w_ref[...].astype(BF16)) + b_ref[...]


def _ada(c_all, w_ada, b_ada):
    n_rows, d = c_all.shape
    n_cols = w_ada.shape[1]
    tn = TN_ADA
    return pl.pallas_call(
        _ada_kernel,
        grid=(n_cols // tn,),
        in_specs=[pl.BlockSpec((n_rows, d), lambda j: (0, 0)),
                  pl.BlockSpec((d, tn), lambda j: (0, j)),
                  pl.BlockSpec((1, tn), lambda j: (0, j))],
        out_specs=pl.BlockSpec((n_rows, tn), lambda j: (0, j)),
        out_shape=jax.ShapeDtypeStruct((n_rows, n_cols), F32),
        compiler_params=_cparams(("arbitrary",)),
        name="ada",
    )(c_all, w_ada, b_ada.reshape(1, n_cols))


def _inproj_kernel(x_ref, sc_ref, sh_ref, w_ref, wg_ref, o_ref, og_ref, h_scr):
    @pl.when(pl.program_id(1) == 0)
    def _():
        h = (x_ref[...] * (1.0 + sc_ref[0]) + sh_ref[0]).astype(BF16)
        h_scr[...] = h
        og_ref[...] = _dot(h, wg_ref[...])

    o_ref[...] = _dot(h_scr[...], w_ref[...])


def _mod_spec(mod, tiles_per_mod, grid_rank):
    _, mod_rows, d = mod.shape
    if grid_rank == 1:
        index_map = lambda i: (i // tiles_per_mod, 0, 0)
    else:
        index_map = lambda i, j: (i // tiles_per_mod, 0, 0)
    if mod_rows == 1:
        return pl.BlockSpec((1, 1, d), index_map)
    return pl.BlockSpec((1, mod_rows, d), index_map, pipeline_mode=pl.Buffered(1))


def _inproj(x2d, sc, sh, w_main, w_gate, tm, tiles_per_mod):
    m, d = x2d.shape
    tn = TN_INPROJ
    mod_spec = _mod_spec(sc, tiles_per_mod, 2)
    return pl.pallas_call(
        _inproj_kernel,
        grid=(m // tm, MAIN_COLS // tn),
        in_specs=[pl.BlockSpec((tm, d), lambda i, j: (i, 0)),
                  mod_spec, mod_spec,
                  pl.BlockSpec((d, tn), lambda i, j: (0, j)),
                  pl.BlockSpec((d, GATE_PAD), lambda i, j: (0, 0))],
        out_specs=[pl.BlockSpec((tm, tn), lambda i, j: (i, j)),
                   pl.BlockSpec((tm, GATE_PAD), lambda i, j: (i, 0))],
        out_shape=[jax.ShapeDtypeStruct((m, MAIN_COLS), F32),
                   jax.ShapeDtypeStruct((m, GATE_PAD), F32)],
        scratch_shapes=[pltpu.VMEM((tm, d), BF16)],
        compiler_params=_cparams(("arbitrary", "arbitrary")),
        name="inproj",
    )(x2d, sc, sh, w_main, w_gate)


def _gate_tables(g, bias, alog, gid):
    x = g + bias
    is_f = (gid >= GF0) & (gid < GB0)
    is_a = (gid >= GA0) & (gid < GA0 + G_HEADS)
    log_f = jax.nn.log_sigmoid(x)
    log_a = -jnp.exp(alog) * jax.nn.softplus(x)
    inc = jnp.where(is_f, log_f, jnp.where(is_a, log_a, 0.0))
    return x, inc, jax.nn.sigmoid(x)


def _chunk_masks(L):
    row = lax.broadcasted_iota(jnp.int32, (L, L), 0)
    col = lax.broadcasted_iota(jnp.int32, (L, L), 1)
    return row >= col, row > col, row == col


def _gates_both_forms(g_col, g_row, gb_row_ref, gb_col_ref, al_row_ref, al_col_ref, L):
    causal, _, _ = _chunk_masks(L)
    tril = causal.astype(F32)
    triu = (lax.broadcasted_iota(jnp.int32, (L, L), 0) <= lax.broadcasted_iota(jnp.int32, (L, L), 1)).astype(F32)
    gid_c = lax.broadcasted_iota(jnp.int32, (L, GATE_PAD), 1)
    gid_r = lax.broadcasted_iota(jnp.int32, (GATE_PAD, L), 0)
    x_c, inc_c, beta_c = _gate_tables(g_col, gb_row_ref[...], al_row_ref[...], gid_c)
    x_r, inc_r, _ = _gate_tables(g_row, gb_col_ref[...], al_col_ref[...], gid_r)
    if L <= SMALL_L:
        cum_c = _mm_small(tril, inc_c)
        cum_r = _mm_small(inc_r, triu)
    else:
        tril_b, triu_b = tril.astype(BF16), triu.astype(BF16)
        c1, c2, c3 = _split3(inc_c)
        r1, r2, r3 = _split3(inc_r)
        cum_c = _dot(tril_b, c1) + (_dot(tril_b, c2) + _dot(tril_b, c3))
        cum_r = _dot(r1, triu_b) + (_dot(r2, triu_b) + _dot(r3, triu_b))
    return x_c, cum_c, beta_c, x_r, cum_r


def _rms_gate(h, gain, gate):
    return h * lax.rsqrt(jnp.mean(h * h, axis=-1, keepdims=True) + RMS_EPS) * gain * gate


def _mlstm_kernel(*refs, L, nb, has_init):
    if has_init:
        (q_ref, k_ref, v_ref, o_ref, gcol_ref, grow_ref, gb_row_ref, gb_col_ref, al_row_ref, al_col_ref,
         ng_ref, c0_ref, n0_ref, m0_ref, h_ref, c_ref, n_ref, m_ref) = refs
    else:
        (q_ref, k_ref, v_ref, o_ref, gcol_ref, grow_ref, gb_row_ref, gb_col_ref, al_row_ref, al_col_ref,
         ng_ref, h_ref, c_ref, n_ref, m_ref) = refs

    @pl.when(pl.program_id(1) == 0)
    def _():
        if has_init:
            c_ref[...] = c0_ref[...]
            n_ref[...] = n0_ref[...]
            m_ref[...] = m0_ref[...]
        else:
            c_ref[...] = jnp.zeros_like(c_ref)
            n_ref[...] = jnp.zeros_like(n_ref)
            m_ref[...] = jnp.zeros_like(m_ref)

    causal, _, _ = _chunk_masks(L)
    lane = lax.broadcasted_iota(jnp.int32, (1, GATE_PAD), 1)
    scale = M_HEAD_DIM ** -0.5
    gates = [_gates_both_forms(_tok_get(gcol_ref, b, L, ALL), grow_ref[b, 0], gb_row_ref, gb_col_ref,
                               al_row_ref, al_col_ref, L) for b in range(nb)]
    m_all = [m_ref[b] for b in range(nb)]
    probs = [(b, h) for b in range(nb) for h in range(M_HEADS)]

    def cols(h):
        return slice(h * M_HEAD_DIM, (h + 1) * M_HEAD_DIM)

    q = [_tok_get(q_ref, b, L, cols(h)) for b, h in probs]
    k = [_tok_get(k_ref, b, L, cols(h)) * scale for b, h in probs]
    vb = [_tok_get(v_ref, b, L, cols(h)).astype(BF16) for b, h in probs]
    qb = [x.astype(BF16) for x in q]
    kb = [x.astype(BF16) for x in k]
    ig_c = [gates[b][0][:, GI0 + h:GI0 + h + 1] for b, h in probs]
    ig_r = [gates[b][3][GI0 + h:GI0 + h + 1, :] for b, h in probs]
    bt_c = [gates[b][1][:, GF0 + h:GF0 + h + 1] for b, h in probs]
    bt_r = [gates[b][4][GF0 + h:GF0 + h + 1, :] for b, h in probs]
    m0 = [jnp.sum(jnp.where(lane == h, m_all[b], 0.0), axis=1, keepdims=True) for b, h in probs]
    n_p = len(probs)
    rng = range(n_p)

    log_d = [jnp.where(causal, bt_c[i] - bt_r[i] + ig_r[i], -jnp.inf) for i in rng]
    inter = [bt_c[i] + m0[i] for i in rng]
    m_t = [jnp.maximum(inter[i], jnp.max(log_d[i], axis=1, keepdims=True)) for i in rng]
    inter_w = [jnp.exp(inter[i] - m_t[i]) for i in rng]
    qk = [_dot_nt(qb[i], kb[i]) for i in rng]
    c_old = [c_ref[b, h] for b, h in probs]
    n_old = [n_ref[b, h:h + 1, :] for b, h in probs]
    qc = [_dot(qb[i], c_old[i].astype(BF16)) for i in rng]
    s = [qk[i] * jnp.exp(log_d[i] - m_t[i]) for i in rng]
    sv = [_dot(s[i].astype(BF16), vb[i]) for i in rng]

    b_last = [bt_c[i][L - 1:L, :] for i in rng]
    m_new = [jnp.maximum(b_last[i] + m0[i], jnp.max(b_last[i] - bt_r[i] + ig_r[i], axis=1, keepdims=True))
             for i in rng]
    kw = [k[i] * jnp.exp(b_last[i] - bt_c[i] + ig_c[i] - m_new[i]) for i in rng]
    decay = [jnp.exp(b_last[i] + m0[i] - m_new[i]) for i in rng]
    kv = [_dot_tn(kw[i].astype(BF16), vb[i]) for i in rng]

    m_next = list(m_all)
    for i, (b, h) in enumerate(probs):
        num = inter_w[i] * qc[i] + sv[i]
        den = inter_w[i] * jnp.sum(q[i] * n_old[i], axis=1, keepdims=True) + jnp.sum(s[i], axis=1, keepdims=True)
        hh = num / jnp.maximum(jnp.abs(den), jnp.exp(-m_t[i]))
        c_ref[b, h] = decay[i] * c_old[i] + kv[i]
        n_ref[b, h:h + 1, :] = decay[i] * n_old[i] + jnp.sum(kw[i], axis=0, keepdims=True)
        m_next[b] = jnp.where(lane == h, m_new[i], m_next[b])
        gate = jax.nn.sigmoid(_tok_get(o_ref, b, L, cols(h)))
        _tok_set(h_ref, b, L, cols(h), _rms_gate(hh, ng_ref[:, cols(h)], gate).astype(h_ref.dtype))
    for b in range(nb):
        m_ref[b] = m_next[b]


def _tok_spec(L, nb, nc, width, col_block):
    if nc == 1:
        return pl.BlockSpec((nb * L, width), lambda b, c: (b, col_block))
    return pl.BlockSpec((nb, L, width), lambda b, c: (b, c, col_block))


def _tok_view(x2d, batch, nc):
    return x2d if nc == 1 else x2d.reshape(batch, x2d.shape[0] // batch, x2d.shape[1])


def _gate_specs(L, nb, nc):
    vec_r = pl.BlockSpec((1, GATE_PAD), lambda b, c: (0, 0))
    vec_c = pl.BlockSpec((GATE_PAD, 1), lambda b, c: (0, 0))
    return [_tok_spec(L, nb, nc, GATE_PAD, 0),
            pl.BlockSpec((nb, 1, GATE_PAD, L), lambda b, c: (b, c, 0, 0)),
            vec_r, vec_c, vec_r, vec_c]


def _scan_out_dtype(L):
    return BF16 if L % 16 == 0 else F32


def _mlstm(proj, gates, gates_t, gvecs, norm_g, batch, nc, L, nb, init):
    has_init = init is not None
    in_specs = [_tok_spec(L, nb, nc, M_WIDTH, j) for j in range(4)] + _gate_specs(L, nb, nc)
    in_specs.append(pl.BlockSpec((1, M_WIDTH), lambda b, c: (0, 0)))
    c_spec = pl.BlockSpec((nb, M_HEADS, M_HEAD_DIM, M_HEAD_DIM), lambda b, c: (b, 0, 0, 0))
    n_spec = pl.BlockSpec((nb, M_HEADS, M_HEAD_DIM), lambda b, c: (b, 0, 0))
    m_spec = pl.BlockSpec((nb, 1, GATE_PAD), lambda b, c: (b, 0, 0))
    args = [proj, proj, proj, proj, gates, gates_t, *gvecs, norm_g.reshape(1, M_WIDTH)]
    if has_init:
        in_specs += [c_spec, n_spec, m_spec]
        args += list(init)
    return pl.pallas_call(
        functools.partial(_mlstm_kernel, L=L, nb=nb, has_init=has_init),
        grid=(batch // nb, nc),
        in_specs=in_specs,
        out_specs=[_tok_spec(L, nb, nc, M_WIDTH, 0), c_spec, n_spec, m_spec],
        out_shape=[jax.ShapeDtypeStruct(proj.shape[:-1] + (M_WIDTH,), _scan_out_dtype(L)),
                   jax.ShapeDtypeStruct((batch, M_HEADS, M_HEAD_DIM, M_HEAD_DIM), F32),
                   jax.ShapeDtypeStruct((batch, M_HEADS, M_HEAD_DIM), F32),
                   jax.ShapeDtypeStruct((batch, 1, GATE_PAD), F32)],
        compiler_params=_cparams(("arbitrary", "arbitrary")),
        name="mlstm",
    )(*args)


def _inv_unit_lower(a_list, L):
    _, _, eye = _chunk_masks(L)
    eye = eye.astype(F32)
    levels = max(1, (L - 1).bit_length())
    p = [-a for a in a_list]
    t = [eye + x for x in p]
    if L <= SMALL_L:
        for _ in range(levels - 1):
            p = [_mm_small(x, x) for x in p]
            t = [ti + _mm_small(ti, pi) for ti, pi in zip(t, p)]
        return t
    ps = [_split2(x) for x in p]
    for _ in range(levels - 1):
        ps = [_split2(_dot3(x, x)) for x in ps]
        ts = [_split2(x) for x in t]
        t = [ti + _dot3(tsi, psi) for ti, tsi, psi in zip(t, ts, ps)]
    return [_split2(x) for x in t]


def _gdn_kernel(*refs, L, nb, nc, has_init):
    if has_init:
        (xq_ref, xk_ref, xv_ref, z_ref, gcol_ref, grow_ref, gb_row_ref, gb_col_ref, al_row_ref, al_col_ref,
         cw_ref, ng_ref, s0_ref, conv0_ref, h_ref, s_ref, conv_ref, buf) = refs
    else:
        (xq_ref, xk_ref, xv_ref, z_ref, gcol_ref, grow_ref, gb_row_ref, gb_col_ref, al_row_ref, al_col_ref,
         cw_ref, ng_ref, h_ref, s_ref, conv_ref, buf) = refs
    hist = CONV_W - 1
    base = SUBLANES - hist

    def cols(h):
        return slice(h * G_HEAD_DIM, (h + 1) * G_HEAD_DIM)

    @pl.when(pl.program_id(1) == 0)
    def _():
        buf[:, 0:SUBLANES, :] = jnp.zeros((nb, SUBLANES, 3 * G_WIDTH), F32)
        if has_init:
            s_ref[...] = s0_ref[...]
            for b in range(nb):
                buf[b, base:SUBLANES, :] = conv0_ref[b]
        else:
            s_ref[...] = jnp.zeros_like(s_ref)

    act = []
    for b in range(nb):
        buf[b, SUBLANES:SUBLANES + L, 0:G_WIDTH] = _tok_get(xq_ref, b, L, ALL)
        buf[b, SUBLANES:SUBLANES + L, G_WIDTH:2 * G_WIDTH] = _tok_get(xk_ref, b, L, ALL)
        buf[b, SUBLANES:SUBLANES + L, 2 * G_WIDTH:3 * G_WIDTH] = _tok_get(xv_ref, b, L, ALL)
        y = cw_ref[0:1, :] * buf[b, base:base + L, :]
        for j in range(1, CONV_W):
            y = y + cw_ref[j:j + 1, :] * buf[b, base + j:base + j + L, :]
        conv_ref[b] = buf[b, SUBLANES + L - hist:SUBLANES + L, :]
        if nc > 1:
            buf[b, 0:SUBLANES, :] = buf[b, L:L + SUBLANES, :]
        act.append(y * jax.nn.sigmoid(y))

    causal, strict, _ = _chunk_masks(L)
    scale = G_HEAD_DIM ** -0.5
    gates = [_gates_both_forms(_tok_get(gcol_ref, b, L, ALL), grow_ref[b, 0], gb_row_ref, gb_col_ref,
                               al_row_ref, al_col_ref, L) for b in range(nb)]
    probs = [(b, h) for b in range(nb) for h in range(G_HEADS)]
    rng = range(len(probs))

    def unit(x):
        return x * lax.rsqrt(jnp.sum(x * x, axis=-1, keepdims=True) + RMS_EPS)

    q = [unit(act[b][:, h * G_HEAD_DIM:(h + 1) * G_HEAD_DIM]) * scale for b, h in probs]
    k = [unit(act[b][:, G_WIDTH + h * G_HEAD_DIM:G_WIDTH + (h + 1) * G_HEAD_DIM]) for b, h in probs]
    v = [act[b][:, 2 * G_WIDTH + h * G_HEAD_DIM:2 * G_WIDTH + (h + 1) * G_HEAD_DIM] for b, h in probs]
    qb = [x.astype(BF16) for x in q]
    kb = [x.astype(BF16) for x in k]
    b_c = [gates[b][1][:, GA0 + h:GA0 + h + 1] for b, h in probs]
    b_r = [gates[b][4][GA0 + h:GA0 + h + 1, :] for b, h in probs]
    bet = [gates[b][2][:, GB0 + h:GB0 + h + 1] for b, h in probs]
    decay = [jnp.exp(jnp.where(causal, b_c[i] - b_r[i], -jnp.inf)) for i in rng]
    eb = [jnp.exp(b_c[i]) for i in rng]
    kk = [_dot_nt(kb[i], kb[i]) for i in rng]
    qk = [_dot_nt(qb[i], kb[i]) * decay[i] for i in rng]
    s_old = [s_ref[b, h] for b, h in probs]
    sb = [x.astype(BF16) for x in s_old]
    qs = [_dot(qb[i], sb[i]) for i in rng]

    t = _inv_unit_lower([jnp.where(strict, bet[i] * kk[i] * decay[i], 0.0) for i in rng], L)
    rhs = [jnp.concatenate([bet[i] * v[i], (bet[i] * eb[i]) * k[i]], axis=-1) for i in rng]
    if L <= SMALL_L:
        sol = [_mm_small(t[i], rhs[i]) for i in rng]
    else:
        sol = [_dot3(t[i], _split2(rhs[i])) for i in rng]
    u = [sol[i][:, 0:G_HEAD_DIM] - _dot(sol[i][:, G_HEAD_DIM:2 * G_HEAD_DIM].astype(BF16), sb[i]) for i in rng]
    ub = [x.astype(BF16) for x in u]
    o = [eb[i] * qs[i] + _dot(qk[i].astype(BF16), ub[i]) for i in rng]
    b_last = [b_c[i][L - 1:L, :] for i in rng]
    wk = [(jnp.exp(b_last[i] - b_c[i]) * k[i]).astype(BF16) for i in rng]
    ds = [_dot_tn(wk[i], ub[i]) for i in rng]
    for i, (b, h) in enumerate(probs):
        s_ref[b, h] = jnp.exp(b_last[i]) * s_old[i] + ds[i]
        z = _tok_get(z_ref, b, L, cols(h))
        _tok_set(h_ref, b, L, cols(h),
                 _rms_gate(o[i], ng_ref[:, cols(h)], z * jax.nn.sigmoid(z)).astype(h_ref.dtype))


def _gdn(proj, gates, gates_t, gvecs, conv_w, norm_g, batch, nc, L, nb, init):
    has_init = init is not None
    blk0 = 4 * M_WIDTH // G_WIDTH
    in_specs = [_tok_spec(L, nb, nc, G_WIDTH, blk0 + j) for j in range(4)] + _gate_specs(L, nb, nc)
    in_specs += [pl.BlockSpec((CONV_W, 3 * G_WIDTH), lambda b, c: (0, 0)),
                 pl.BlockSpec((1, G_WIDTH), lambda b, c: (0, 0))]
    s_spec = pl.BlockSpec((nb, G_HEADS, G_HEAD_DIM, G_HEAD_DIM), lambda b, c: (b, 0, 0, 0))
    conv_spec = pl.BlockSpec((nb, CONV_W - 1, 3 * G_WIDTH), lambda b, c: (b, 0, 0))
    args = [proj, proj, proj, proj, gates, gates_t, *gvecs, conv_w, norm_g.reshape(1, G_WIDTH)]
    if has_init:
        in_specs += [s_spec, conv_spec]
        args += list(init)
    buf_rows = SUBLANES + -(-L // SUBLANES) * SUBLANES
    return pl.pallas_call(
        functools.partial(_gdn_kernel, L=L, nb=nb, nc=nc, has_init=has_init),
        grid=(batch // nb, nc),
        in_specs=in_specs,
        out_specs=[_tok_spec(L, nb, nc, G_WIDTH, 0), s_spec, conv_spec],
        out_shape=[jax.ShapeDtypeStruct(proj.shape[:-1] + (G_WIDTH,), _scan_out_dtype(L)),
                   jax.ShapeDtypeStruct((batch, G_HEADS, G_HEAD_DIM, G_HEAD_DIM), F32),
                   jax.ShapeDtypeStruct((batch, CONV_W - 1, 3 * G_WIDTH), F32)],
        scratch_shapes=[pltpu.VMEM((nb, buf_rows, 3 * G_WIDTH), F32)],
        compiler_params=_cparams(("arbitrary", "arbitrary")),
        name="gdn",
    )(*args)


def _layer_norm(y, g, b):
    mu = jnp.mean(y, axis=-1, keepdims=True)
    yc = y - mu
    var = jnp.mean(yc * yc, axis=-1, keepdims=True)
    return yc * lax.rsqrt(var + LN_EPS) * g + b


def _outproj_kernel(hm_ref, hg_ref, x_ref, gt_ref, w_ref, g_ref, b_ref, o_ref):
    mix = (_dot(hm_ref[...].astype(BF16), w_ref[0:M_WIDTH, :])
           + _dot(hg_ref[...].astype(BF16), w_ref[M_WIDTH:M_WIDTH + G_WIDTH, :]))
    y = DEEPNORM_ALPHA * x_ref[...] + (1.0 + gt_ref[0]) * mix
    o_ref[...] = _layer_norm(y, g_ref[...], b_ref[...])


def _outproj(hm, hg, x2d, gt, w_out, ln_g, ln_b, tm, tiles_per_mod):
    m, d = x2d.shape
    vec = pl.BlockSpec((1, d), lambda i: (0, 0))
    return pl.pallas_call(
        _outproj_kernel,
        grid=(m // tm,),
        in_specs=[pl.BlockSpec((tm, M_WIDTH), lambda i: (i, 0)),
                  pl.BlockSpec((tm, G_WIDTH), lambda i: (i, 0)),
                  pl.BlockSpec((tm, d), lambda i: (i, 0)),
                  _mod_spec(gt, tiles_per_mod, 1),
                  pl.BlockSpec((d, d), lambda i: (0, 0)),
                  vec, vec],
        out_specs=pl.BlockSpec((tm, d), lambda i: (i, 0)),
        out_shape=jax.ShapeDtypeStruct((m, d), F32),
        compiler_params=_cparams(("arbitrary",)),
        name="outproj",
    )(hm, hg, x2d, gt, w_out, ln_g.reshape(1, d), ln_b.reshape(1, d))


def _ffn_kernel(x_ref, sc_ref, sh_ref, gt_ref, wg_ref, wu_ref, wd_ref, g_ref, b_ref, o_ref, h_scr, acc):
    f = pl.program_id(1)

    @pl.when(f == 0)
    def _():
        h_scr[...] = (x_ref[...] * (1.0 + sc_ref[0]) + sh_ref[0]).astype(BF16)
        acc[...] = jnp.zeros_like(acc)

    h = h_scr[...]
    gate = _dot(h, wg_ref[...])
    up = _dot(h, wu_ref[...])
    act = (gate * jax.nn.sigmoid(gate) * up).astype(BF16)
    acc[...] += _dot(act, wd_ref[...])

    @pl.when(f == pl.num_programs(1) - 1)
    def _():
        y = DEEPNORM_ALPHA * x_ref[...] + (1.0 + gt_ref[0]) * acc[...]
        o_ref[...] = _layer_norm(y, g_ref[...], b_ref[...])


def _ffn(x2d, sc, sh, gt, w_gu, w_down, ln_g, ln_b, tm, tiles_per_mod):
    m, d = x2d.shape
    tf = TF_FFN
    nf = D_FF // tf
    mod_spec = _mod_spec(sc, tiles_per_mod, 2)
    vec = pl.BlockSpec((1, d), lambda i, f: (0, 0))
    return pl.pallas_call(
        _ffn_kernel,
        grid=(m // tm, nf),
        in_specs=[pl.BlockSpec((tm, d), lambda i, f: (i, 0)),
                  mod_spec, mod_spec, mod_spec,
                  pl.BlockSpec((d, tf), lambda i, f: (0, f)),
                  pl.BlockSpec((d, tf), lambda i, f: (0, nf + f)),
                  pl.BlockSpec((tf, d), lambda i, f: (f, 0)),
                  vec, vec],
        out_specs=pl.BlockSpec((tm, d), lambda i, f: (i, 0), pipeline_mode=pl.Buffered(1)),
        out_shape=jax.ShapeDtypeStruct((m, d), F32),
        scratch_shapes=[pltpu.VMEM((tm, d), BF16), pltpu.VMEM((tm, d), F32)],
        compiler_params=_cparams(("arbitrary", "arbitrary")),
        name="ffn",
    )(x2d, sc, sh, gt, w_gu, w_gu, w_down, ln_g.reshape(1, d), ln_b.reshape(1, d))


def _layer(x, ada6, weights, init_m, init_g, nb):
    batch, seq, d = x.shape
    L = CHUNK if seq % CHUNK == 0 else seq
    nc = seq // L
    m = batch * seq
    x2d = x.reshape(m, d)

    def tiling(tm_cap):
        tm = min(tm_cap, m)
        if seq % tm == 0:
            return tm, seq // tm, lambda a: a.reshape(batch, 1, d)
        return tm, 1, lambda a: jnp.repeat(a, seq, axis=0).reshape(m // tm, tm, d)

    sh1, sc1, gt1, sh2, sc2, gt2 = ada6
    tm, tpm, mod = tiling(TM_INPROJ)
    proj, gates = _inproj(x2d, mod(sc1), mod(sh1), weights["w_in_main"], weights["w_in_gate"], tm, tpm)
    gates_t = jnp.swapaxes(gates.reshape(batch, nc, L, GATE_PAD), 2, 3)
    proj_v, gates_v = _tok_view(proj, batch, nc), _tok_view(gates, batch, nc)
    gvecs = weights["gvecs"]
    hm, c1, n1, m1 = _mlstm(proj_v, gates_v, gates_t, gvecs, weights["m_norm_g"], batch, nc, L, nb["mlstm"],
                            init_m)
    hg, s1, conv1 = _gdn(proj_v, gates_v, gates_t, gvecs, weights["conv_w"], weights["g_norm_g"], batch, nc, L,
                         nb["gdn"], init_g)
    tm, tpm, mod = tiling(TM_OUTPROJ)
    x1 = _outproj(hm.reshape(m, M_WIDTH), hg.reshape(m, G_WIDTH), x2d, mod(gt1), weights["w_out"],
                  weights["ln1_g"], weights["ln1_b"], tm, tpm)
    tm, tpm, mod = tiling(TM_FFN)
    y = _ffn(x1, mod(sc2), mod(sh2), mod(gt2), weights["w_gu"], weights["w_down"], weights["ln2_g"],
             weights["ln2_b"], tm, tpm)
    return y.reshape(batch, seq, d), c1, n1, m1[:, 0, :M_HEADS], s1, conv1


def kernel(x_prompt, x_sample, state_mlstm_C, state_mlstm_n, state_mlstm_m, state_gdn_S, state_gdn_conv,
           c_prompt, c_sample, w_ada, b_ada, w_in, m_i_bias, m_f_bias, m_norm_g, conv_w, g_dt_bias,
           g_A_log, g_norm_g, w_out, ln1_g, ln1_b, w_gu, w_down, ln2_g, ln2_b):
    bp, seq_p, d = x_prompt.shape
    bs, seq_s, _ = x_sample.shape

    ada = _ada(jnp.concatenate([c_prompt, c_sample], axis=0), w_ada, b_ada)
    ada6 = [ada[:, i * d:(i + 1) * d] for i in range(6)]

    gate_bias = jnp.zeros((GATE_PAD,), F32)
    gate_bias = gate_bias.at[GI0:GI0 + M_HEADS].set(m_i_bias).at[GF0:GF0 + M_HEADS].set(m_f_bias)
    gate_bias = gate_bias.at[GA0:GA0 + G_HEADS].set(g_dt_bias)
    a_log = jnp.zeros((GATE_PAD,), F32).at[GA0:GA0 + G_HEADS].set(g_A_log)
    weights = {
        "w_in_main": _cast_bf16(w_in, d, MAIN_COLS),
        "w_in_gate": jnp.pad(w_in[:, MAIN_COLS:], ((0, 0), (0, GATE_PAD - GATE_COLS))).astype(BF16),
        "gvecs": (gate_bias.reshape(1, GATE_PAD), gate_bias.reshape(GATE_PAD, 1),
                  a_log.reshape(1, GATE_PAD), a_log.reshape(GATE_PAD, 1)),
        "m_norm_g": m_norm_g, "conv_w": conv_w, "g_norm_g": g_norm_g,
        "w_out": _cast_bf16(w_out, *w_out.shape), "ln1_g": ln1_g, "ln1_b": ln1_b,
        "w_gu": _cast_bf16(w_gu, *w_gu.shape), "w_down": _cast_bf16(w_down, *w_down.shape),
        "ln2_g": ln2_g, "ln2_b": ln2_b,
    }

    y_p, p_c, p_n, p_m, p_s, p_conv = _layer(x_prompt, [a[:bp] for a in ada6], weights, None, None,
                                             SEQS_PER_STEP_PROMPT)
    m0 = jnp.pad(state_mlstm_m, ((0, 0), (0, GATE_PAD - M_HEADS))).reshape(bs, 1, GATE_PAD)
    y_s, s_c, s_n, s_m, s_s, s_conv = _layer(
        x_sample, [a[bp:] for a in ada6], weights, (state_mlstm_C, state_mlstm_n, m0),
        (state_gdn_S, state_gdn_conv), SEQS_PER_STEP_SAMPLE)
    return (y_p, y_s, p_c, p_n, p_m, p_s, p_conv, s_c, s_n, s_m, s_s, s_conv)
```

```python
import functools

import jax
import jax.numpy as jnp
from jax import lax
from jax.experimental import pallas as pl
from jax.experimental.pallas import tpu as pltpu

F32 = jnp.float32
BF16 = jnp.bfloat16

D_MODEL = 2048
M_HEADS = 4
M_HEAD_DIM = 256
M_WIDTH = M_HEADS * M_HEAD_DIM
G_HEADS = 8
G_HEAD_DIM = 128
G_WIDTH = G_HEADS * G_HEAD_DIM
CONV_W = 4
CHUNK = 64
D_FF = 5632
MAIN_COLS = 4 * M_WIDTH + 3 * G_WIDTH + G_WIDTH
GATE_COLS = 2 * M_HEADS + 2 * G_HEADS
GATE_PAD = 128
DEEPNORM_ALPHA = 2.0 ** 0.25
LN_EPS = 1e-5
RMS_EPS = 1e-6
GI0, GF0, GB0, GA0 = 0, M_HEADS, 2 * M_HEADS, 2 * M_HEADS + G_HEADS

SUBLANES = 8
VMEM_LIMIT_BYTES = 56 * 1024 * 1024
TM_INPROJ, TM_OUTPROJ, TM_FFN = 1024, 512, 512
TN_INPROJ, TN_ADA, TF_FFN = 1024, 1024, 512
CAST_ROWS, CAST_COLS = 2048, 1024
SEQS_PER_STEP_PROMPT = {"mlstm": 1, "gdn": 2}
SEQS_PER_STEP_SAMPLE = {"mlstm": 4, "gdn": 4}
SMALL_L = 8


def _cparams(sem):
    return pltpu.CompilerParams(dimension_semantics=sem, vmem_limit_bytes=VMEM_LIMIT_BYTES)


def _dot(a, b):
    return jnp.dot(a, b, preferred_element_type=F32)


def _dot_nt(a, b):
    return lax.dot_general(a, b, (((1,), (1,)), ((), ())), preferred_element_type=F32)


def _dot_tn(a, b):
    return lax.dot_general(a, b, (((0,), (0,)), ((), ())), preferred_element_type=F32)


def _split2(x):
    hi = x.astype(BF16)
    return hi, (x - hi.astype(F32)).astype(BF16)


def _split3(x):
    hi = x.astype(BF16)
    r = x - hi.astype(F32)
    mid = r.astype(BF16)
    return hi, mid, (r - mid.astype(F32)).astype(BF16)


def _dot3(a, b):
    return _dot(a[0], b[0]) + (_dot(a[0], b[1]) + _dot(a[1], b[0]))


def _mm_small(a, b):
    out = a[:, 0:1] * b[0:1, :]
    for i in range(1, a.shape[1]):
        out = out + a[:, i:i + 1] * b[i:i + 1, :]
    return out


def _tok_get(ref, b, L, cols):
    if len(ref.shape) == 3:
        return ref[b, :, cols]
    return ref[b * L:(b + 1) * L, cols]


def _tok_set(ref, b, L, cols, val):
    if len(ref.shape) == 3:
        ref[b, :, cols] = val
    else:
        ref[b * L:(b + 1) * L, cols] = val


ALL = slice(None)


def _cast_kernel(x_ref, o_ref):
    o_ref[...] = x_ref[...].astype(BF16)


def _cast_bf16(w, rows, cols):
    bc = min(cols, CAST_COLS)
    br = max(r for r in range(SUBLANES, min(rows, CAST_ROWS) + 1, SUBLANES) if rows % r == 0)
    return pl.pallas_call(
        _cast_kernel,
        grid=(rows // br, cols // bc),
        in_specs=[pl.BlockSpec((br, bc), lambda i, j: (i, j))],
        out_specs=pl.BlockSpec((br, bc), lambda i, j: (i, j)),
        out_shape=jax.ShapeDtypeStruct((rows, cols), BF16),
        compiler_params=_cparams(("arbitrary", "arbitrary")),
        name="cast",
    )(w)


def _ada_kernel(c_ref, w_ref, b_ref, o_ref):
    c = c_ref[...]
    a = (c * jax.nn.sigmoid(c)).astype(BF16)
    o_ref[...] = _dot(a, w_ref[...].astype(BF16)) + b_ref[...]


def _ada(c_all, w_ada, b_ada):
    n_rows, d = c_all.shape
    n_cols = w_ada.shape[1]
    tn = TN_ADA
    return pl.pallas_call(
        _ada_kernel,
        grid=(n_cols // tn,),
        in_specs=[pl.BlockSpec((n_rows, d), lambda j: (0, 0)),
                  pl.BlockSpec((d, tn), lambda j: (0, j)),
                  pl.BlockSpec((1, tn), lambda j: (0, j))],
        out_specs=pl.BlockSpec((n_rows, tn), lambda j: (0, j)),
        out_shape=jax.ShapeDtypeStruct((n_rows, n_cols), F32),
        compiler_params=_cparams(("arbitrary",)),
        name="ada",
    )(c_all, w_ada, b_ada.reshape(1, n_cols))


def _inproj_kernel(x_ref, sc_ref, sh_ref, wt_ref, wgt_ref, o_ref, og_ref, h_scr):
    @pl.when(pl.program_id(1) == 0)
    def _():
        h = (x_ref[...] * (1.0 + sc_ref[0]) + sh_ref[0]).astype(BF16)
        h_scr[...] = h
        og_ref[...] = _dot_nt(h, wgt_ref[...])

    o_ref[...] = _dot_nt(h_scr[...], wt_ref[...])


def _mod_spec(mod, tiles_per_mod, grid_rank):
    _, mod_rows, d = mod.shape
    if grid_rank == 1:
        index_map = lambda i: (i // tiles_per_mod, 0, 0)
    else:
        index_map = lambda i, j: (i // tiles_per_mod, 0, 0)
    if mod_rows == 1:
        return pl.BlockSpec((1, 1, d), index_map)
    return pl.BlockSpec((1, mod_rows, d), index_map, pipeline_mode=pl.Buffered(1))


def _inproj(x2d, sc, sh, w_main, w_gate, tm, tiles_per_mod):
    m, d = x2d.shape
    tn = TN_INPROJ
    mod_spec = _mod_spec(sc, tiles_per_mod, 2)
    return pl.pallas_call(
        _inproj_kernel,
        grid=(m // tm, MAIN_COLS // tn),
        in_specs=[pl.BlockSpec((tm, d), lambda i, j: (i, 0)),
                  mod_spec, mod_spec,
                  pl.BlockSpec((tn, d), lambda i, j: (j, 0)),
                  pl.BlockSpec((GATE_PAD, d), lambda i, j: (0, 0))],
        out_specs=[pl.BlockSpec((tm, tn), lambda i, j: (i, j)),
                   pl.BlockSpec((tm, GATE_PAD), lambda i, j: (i, 0))],
        out_shape=[jax.ShapeDtypeStruct((m, MAIN_COLS), F32),
                   jax.ShapeDtypeStruct((m, GATE_PAD), F32)],
        scratch_shapes=[pltpu.VMEM((tm, d), BF16)],
        compiler_params=_cparams(("arbitrary", "arbitrary")),
        name="inproj",
    )(x2d, sc, sh, w_main, w_gate)


def _gate_tables(g, bias, alog, gid):
    x = g + bias
    is_f = (gid >= GF0) & (gid < GB0)
    is_a = (gid >= GA0) & (gid < GA0 + G_HEADS)
    log_f = jax.nn.log_sigmoid(x)
    log_a = -jnp.exp(alog) * jax.nn.softplus(x)
    inc = jnp.where(is_f, log_f, jnp.where(is_a, log_a, 0.0))
    return x, inc, jax.nn.sigmoid(x)


def _chunk_masks(L):
    row = lax.broadcasted_iota(jnp.int32, (L, L), 0)
    col = lax.broadcasted_iota(jnp.int32, (L, L), 1)
    return row >= col, row > col, row == col


def _gates_both_forms(g_col, g_row, gb_row_ref, gb_col_ref, al_row_ref, al_col_ref, L):
    causal, _, _ = _chunk_masks(L)
    tril = causal.astype(F32)
    triu = (lax.broadcasted_iota(jnp.int32, (L, L), 0) <= lax.broadcasted_iota(jnp.int32, (L, L), 1)).astype(F32)
    gid_c = lax.broadcasted_iota(jnp.int32, (L, GATE_PAD), 1)
    gid_r = lax.broadcasted_iota(jnp.int32, (GATE_PAD, L), 0)
    x_c, inc_c, beta_c = _gate_tables(g_col, gb_row_ref[...], al_row_ref[...], gid_c)
    x_r, inc_r, _ = _gate_tables(g_row, gb_col_ref[...], al_col_ref[...], gid_r)
    if L <= SMALL_L:
        cum_c = _mm_small(tril, inc_c)
        cum_r = _mm_small(inc_r, triu)
    else:
        tril_b, triu_b = tril.astype(BF16), triu.astype(BF16)
        c1, c2, c3 = _split3(inc_c)
        r1, r2, r3 = _split3(inc_r)
        cum_c = _dot(tril_b, c1) + (_dot(tril_b, c2) + _dot(tril_b, c3))
        cum_r = _dot(r1, triu_b) + (_dot(r2, triu_b) + _dot(r3, triu_b))
    return x_c, cum_c, beta_c, x_r, cum_r


def _rms_gate(h, gain, gate):
    return h * lax.rsqrt(jnp.mean(h * h, axis=-1, keepdims=True) + RMS_EPS) * gain * gate


def _mlstm_kernel(*refs, L, nb, nc, has_init):
    if has_init:
        (q_ref, k_ref, v_ref, o_ref, gcol_ref, grow_ref, gb_row_ref, gb_col_ref, al_row_ref, al_col_ref,
         ng_ref, c0_ref, n0_ref, m0_ref, h_ref, c_ref, n_ref, m_ref) = refs
    else:
        (q_ref, k_ref, v_ref, o_ref, gcol_ref, grow_ref, gb_row_ref, gb_col_ref, al_row_ref, al_col_ref,
         ng_ref, h_ref, c_ref, n_ref, m_ref) = refs

    if has_init and nc == 1:
        c_src, n_src, m_src = c0_ref, n0_ref, m0_ref
    else:
        c_src, n_src, m_src = c_ref, n_ref, m_ref

        @pl.when(pl.program_id(1) == 0)
        def _():
            if has_init:
                c_ref[...] = c0_ref[...]
                n_ref[...] = n0_ref[...]
                m_ref[...] = m0_ref[...]
            else:
                c_ref[...] = jnp.zeros_like(c_ref)
                n_ref[...] = jnp.zeros_like(n_ref)
                m_ref[...] = jnp.zeros_like(m_ref)

    causal, _, _ = _chunk_masks(L)
    lane = lax.broadcasted_iota(jnp.int32, (1, GATE_PAD), 1)
    scale = M_HEAD_DIM ** -0.5
    gates = [_gates_both_forms(_tok_get(gcol_ref, b, L, ALL), grow_ref[b, 0], gb_row_ref, gb_col_ref,
                               al_row_ref, al_col_ref, L) for b in range(nb)]
    m_all = [m_src[b] for b in range(nb)]
    probs = [(b, h) for b in range(nb) for h in range(M_HEADS)]

    def cols(h):
        return slice(h * M_HEAD_DIM, (h + 1) * M_HEAD_DIM)

    q = [_tok_get(q_ref, b, L, cols(h)) for b, h in probs]
    k = [_tok_get(k_ref, b, L, cols(h)) * scale for b, h in probs]
    vb = [_tok_get(v_ref, b, L, cols(h)).astype(BF16) for b, h in probs]
    qb = [x.astype(BF16) for x in q]
    kb = [x.astype(BF16) for x in k]
    ig_c = [gates[b][0][:, GI0 + h:GI0 + h + 1] for b, h in probs]
    ig_r = [gates[b][3][GI0 + h:GI0 + h + 1, :] for b, h in probs]
    bt_c = [gates[b][1][:, GF0 + h:GF0 + h + 1] for b, h in probs]
    bt_r = [gates[b][4][GF0 + h:GF0 + h + 1, :] for b, h in probs]
    m0 = [jnp.sum(jnp.where(lane == h, m_all[b], 0.0), axis=1, keepdims=True) for b, h in probs]
    n_p = len(probs)
    rng = range(n_p)

    log_d = [jnp.where(causal, bt_c[i] - bt_r[i] + ig_r[i], -jnp.inf) for i in rng]
    inter = [bt_c[i] + m0[i] for i in rng]
    m_t = [jnp.maximum(inter[i], jnp.max(log_d[i], axis=1, keepdims=True)) for i in rng]
    inter_w = [jnp.exp(inter[i] - m_t[i]) for i in rng]
    qk = [_dot_nt(qb[i], kb[i]) for i in rng]
    c_old = [c_src[b, h] for b, h in probs]
    n_old = [n_src[b, h:h + 1, :] for b, h in probs]
    qc = [_dot(qb[i], c_old[i].astype(BF16)) for i in rng]
    s = [qk[i] * jnp.exp(log_d[i] - m_t[i]) for i in rng]
    sv = [_dot(s[i].astype(BF16), vb[i]) for i in rng]

    b_last = [bt_c[i][L - 1:L, :] for i in rng]
    m_new = [jnp.maximum(b_last[i] + m0[i], jnp.max(b_last[i] - bt_r[i] + ig_r[i], axis=1, keepdims=True))
             for i in rng]
    kw = [k[i] * jnp.exp(b_last[i] - bt_c[i] + ig_c[i] - m_new[i]) for i in rng]
    decay = [jnp.exp(b_last[i] + m0[i] - m_new[i]) for i in rng]
    kv = [_dot_tn(kw[i].astype(BF16), vb[i]) for i in rng]

    m_next = list(m_all)
    for i, (b, h) in enumerate(probs):
        num = inter_w[i] * qc[i] + sv[i]
        den = inter_w[i] * jnp.sum(q[i] * n_old[i], axis=1, keepdims=True) + jnp.sum(s[i], axis=1, keepdims=True)
        hh = num / jnp.maximum(jnp.abs(den), jnp.exp(-m_t[i]))
        c_ref[b, h] = decay[i] * c_old[i] + kv[i]
        n_ref[b, h:h + 1, :] = decay[i] * n_old[i] + jnp.sum(kw[i], axis=0, keepdims=True)
        m_next[b] = jnp.where(lane == h, m_new[i], m_next[b])
        gate = jax.nn.sigmoid(_tok_get(o_ref, b, L, cols(h)))
        _tok_set(h_ref, b, L, cols(h), _rms_gate(hh, ng_ref[:, cols(h)], gate).astype(h_ref.dtype))
    for b in range(nb):
        m_ref[b] = m_next[b]


def _tok_spec(L, nb, nc, width, col_block):
    if nc == 1:
        return pl.BlockSpec((nb * L, width), lambda b, c: (b, col_block))
    return pl.BlockSpec((nb, L, width), lambda b, c: (b, c, col_block))


def _tok_view(x2d, batch, nc):
    return x2d if nc == 1 else x2d.reshape(batch, x2d.shape[0] // batch, x2d.shape[1])


def _gate_specs(L, nb, nc):
    vec_r = pl.BlockSpec((1, GATE_PAD), lambda b, c: (0, 0))
    vec_c = pl.BlockSpec((GATE_PAD, 1), lambda b, c: (0, 0))
    return [_tok_spec(L, nb, nc, GATE_PAD, 0),
            pl.BlockSpec((nb, 1, GATE_PAD, L), lambda b, c: (b, c, 0, 0)),
            vec_r, vec_c, vec_r, vec_c]


def _scan_out_dtype(L):
    return BF16 if L % 16 == 0 else F32


def _mlstm(proj, gates, gates_t, gvecs, norm_g, batch, nc, L, nb, init):
    has_init = init is not None
    in_specs = [_tok_spec(L, nb, nc, M_WIDTH, j) for j in range(4)] + _gate_specs(L, nb, nc)
    in_specs.append(pl.BlockSpec((1, M_WIDTH), lambda b, c: (0, 0)))
    c_spec = pl.BlockSpec((nb, M_HEADS, M_HEAD_DIM, M_HEAD_DIM), lambda b, c: (b, 0, 0, 0))
    n_spec = pl.BlockSpec((nb, M_HEADS, M_HEAD_DIM), lambda b, c: (b, 0, 0))
    m_spec = pl.BlockSpec((nb, 1, GATE_PAD), lambda b, c: (b, 0, 0))
    args = [proj, proj, proj, proj, gates, gates_t, *gvecs, norm_g.reshape(1, M_WIDTH)]
    if has_init:
        in_specs += [c_spec, n_spec, m_spec]
        args += list(init)
    return pl.pallas_call(
        functools.partial(_mlstm_kernel, L=L, nb=nb, nc=nc, has_init=has_init),
        grid=(batch // nb, nc),
        in_specs=in_specs,
        out_specs=[_tok_spec(L, nb, nc, M_WIDTH, 0), c_spec, n_spec, m_spec],
        out_shape=[jax.ShapeDtypeStruct(proj.shape[:-1] + (M_WIDTH,), _scan_out_dtype(L)),
                   jax.ShapeDtypeStruct((batch, M_HEADS, M_HEAD_DIM, M_HEAD_DIM), F32),
                   jax.ShapeDtypeStruct((batch, M_HEADS, M_HEAD_DIM), F32),
                   jax.ShapeDtypeStruct((batch, 1, GATE_PAD), F32)],
        compiler_params=_cparams(("arbitrary", "arbitrary")),
        name="mlstm",
    )(*args)


def _inv_unit_lower(a_list, L):
    _, _, eye = _chunk_masks(L)
    eye = eye.astype(F32)
    levels = max(1, (L - 1).bit_length())
    p = [-a for a in a_list]
    t = [eye + x for x in p]
    if L <= SMALL_L:
        for _ in range(levels - 1):
            p = [_mm_small(x, x) for x in p]
            t = [ti + _mm_small(ti, pi) for ti, pi in zip(t, p)]
        return t
    ps = [_split2(x) for x in p]
    for _ in range(levels - 1):
        ps = [_split2(_dot3(x, x)) for x in ps]
        ts = [_split2(x) for x in t]
        t = [ti + _dot3(tsi, psi) for ti, tsi, psi in zip(t, ts, ps)]
    return [_split2(x) for x in t]


def _gdn_kernel(*refs, L, nb, nc, has_init):
    if has_init:
        (xq_ref, xk_ref, xv_ref, z_ref, gcol_ref, grow_ref, gb_row_ref, gb_col_ref, al_row_ref, al_col_ref,
         cw_ref, ng_ref, s0_ref, conv0_ref, h_ref, s_ref, conv_ref, buf) = refs
    else:
        (xq_ref, xk_ref, xv_ref, z_ref, gcol_ref, grow_ref, gb_row_ref, gb_col_ref, al_row_ref, al_col_ref,
         cw_ref, ng_ref, h_ref, s_ref, conv_ref, buf) = refs
    hist = CONV_W - 1
    base = SUBLANES - hist

    def cols(h):
        return slice(h * G_HEAD_DIM, (h + 1) * G_HEAD_DIM)

    s_src = s0_ref if (has_init and nc == 1) else s_ref

    @pl.when(pl.program_id(1) == 0)
    def _():
        buf[:, 0:SUBLANES, :] = jnp.zeros((nb, SUBLANES, 3 * G_WIDTH), F32)
        if has_init:
            for b in range(nb):
                buf[b, base:SUBLANES, :] = conv0_ref[b]
            if nc > 1:
                s_ref[...] = s0_ref[...]
        else:
            s_ref[...] = jnp.zeros_like(s_ref)

    act = []
    for b in range(nb):
        buf[b, SUBLANES:SUBLANES + L, 0:G_WIDTH] = _tok_get(xq_ref, b, L, ALL)
        buf[b, SUBLANES:SUBLANES + L, G_WIDTH:2 * G_WIDTH] = _tok_get(xk_ref, b, L, ALL)
        buf[b, SUBLANES:SUBLANES + L, 2 * G_WIDTH:3 * G_WIDTH] = _tok_get(xv_ref, b, L, ALL)
        y = cw_ref[0:1, :] * buf[b, base:base + L, :]
        for j in range(1, CONV_W):
            y = y + cw_ref[j:j + 1, :] * buf[b, base + j:base + j + L, :]
        conv_ref[b] = buf[b, SUBLANES + L - hist:SUBLANES + L, :]
        if nc > 1:
            buf[b, 0:SUBLANES, :] = buf[b, L:L + SUBLANES, :]
        act.append(y * jax.nn.sigmoid(y))

    causal, strict, _ = _chunk_masks(L)
    scale = G_HEAD_DIM ** -0.5
    gates = [_gates_both_forms(_tok_get(gcol_ref, b, L, ALL), grow_ref[b, 0], gb_row_ref, gb_col_ref,
                               al_row_ref, al_col_ref, L) for b in range(nb)]
    probs = [(b, h) for b in range(nb) for h in range(G_HEADS)]
    rng = range(len(probs))

    def unit(x):
        return x * lax.rsqrt(jnp.sum(x * x, axis=-1, keepdims=True) + RMS_EPS)

    q = [unit(act[b][:, h * G_HEAD_DIM:(h + 1) * G_HEAD_DIM]) * scale for b, h in probs]
    k = [unit(act[b][:, G_WIDTH + h * G_HEAD_DIM:G_WIDTH + (h + 1) * G_HEAD_DIM]) for b, h in probs]
    v = [act[b][:, 2 * G_WIDTH + h * G_HEAD_DIM:2 * G_WIDTH + (h + 1) * G_HEAD_DIM] for b, h in probs]
    qb = [x.astype(BF16) for x in q]
    kb = [x.astype(BF16) for x in k]
    b_c = [gates[b][1][:, GA0 + h:GA0 + h + 1] for b, h in probs]
    b_r = [gates[b][4][GA0 + h:GA0 + h + 1, :] for b, h in probs]
    bet = [gates[b][2][:, GB0 + h:GB0 + h + 1] for b, h in probs]
    decay = [jnp.exp(jnp.where(causal, b_c[i] - b_r[i], -jnp.inf)) for i in rng]
    eb = [jnp.exp(b_c[i]) for i in rng]
    kk = [_dot_nt(kb[i], kb[i]) for i in rng]
    qk = [_dot_nt(qb[i], kb[i]) * decay[i] for i in rng]
    s_old = [s_src[b, h] for b, h in probs]
    sb = [x.astype(BF16) for x in s_old]
    qs = [_dot(qb[i], sb[i]) for i in rng]

    t = _inv_unit_lower([jnp.where(strict, bet[i] * kk[i] * decay[i], 0.0) for i in rng], L)
    rhs = [jnp.concatenate([bet[i] * v[i], (bet[i] * eb[i]) * k[i]], axis=-1) for i in rng]
    if L <= SMALL_L:
        sol = [_mm_small(t[i], rhs[i]) for i in rng]
    else:
        sol = [_dot3(t[i], _split2(rhs[i])) for i in rng]
    u = [sol[i][:, 0:G_HEAD_DIM] - _dot(sol[i][:, G_HEAD_DIM:2 * G_HEAD_DIM].astype(BF16), sb[i]) for i in rng]
    ub = [x.astype(BF16) for x in u]
    o = [eb[i] * qs[i] + _dot(qk[i].astype(BF16), ub[i]) for i in rng]
    b_last = [b_c[i][L - 1:L, :] for i in rng]
    wk = [(jnp.exp(b_last[i] - b_c[i]) * k[i]).astype(BF16) for i in rng]
    ds = [_dot_tn(wk[i], ub[i]) for i in rng]
    for i, (b, h) in enumerate(probs):
        s_ref[b, h] = jnp.exp(b_last[i]) * s_old[i] + ds[i]
        z = _tok_get(z_ref, b, L, cols(h))
        _tok_set(h_ref, b, L, cols(h),
                 _rms_gate(o[i], ng_ref[:, cols(h)], z * jax.nn.sigmoid(z)).astype(h_ref.dtype))


def _gdn(proj, gates, gates_t, gvecs, conv_w, norm_g, batch, nc, L, nb, init):
    has_init = init is not None
    blk0 = 4 * M_WIDTH // G_WIDTH
    in_specs = [_tok_spec(L, nb, nc, G_WIDTH, blk0 + j) for j in range(4)] + _gate_specs(L, nb, nc)
    in_specs += [pl.BlockSpec((CONV_W, 3 * G_WIDTH), lambda b, c: (0, 0)),
                 pl.BlockSpec((1, G_WIDTH), lambda b, c: (0, 0))]
    s_spec = pl.BlockSpec((nb, G_HEADS, G_HEAD_DIM, G_HEAD_DIM), lambda b, c: (b, 0, 0, 0))
    conv_spec = pl.BlockSpec((nb, CONV_W - 1, 3 * G_WIDTH), lambda b, c: (b, 0, 0))
    args = [proj, proj, proj, proj, gates, gates_t, *gvecs, conv_w, norm_g.reshape(1, G_WIDTH)]
    if has_init:
        in_specs += [s_spec, conv_spec]
        args += list(init)
    buf_rows = SUBLANES + -(-L // SUBLANES) * SUBLANES
    return pl.pallas_call(
        functools.partial(_gdn_kernel, L=L, nb=nb, nc=nc, has_init=has_init),
        grid=(batch // nb, nc),
        in_specs=in_specs,
        out_specs=[_tok_spec(L, nb, nc, G_WIDTH, 0), s_spec, conv_spec],
        out_shape=[jax.ShapeDtypeStruct(proj.shape[:-1] + (G_WIDTH,), _scan_out_dtype(L)),
                   jax.ShapeDtypeStruct((batch, G_HEADS, G_HEAD_DIM, G_HEAD_DIM), F32),
                   jax.ShapeDtypeStruct((batch, CONV_W - 1, 3 * G_WIDTH), F32)],
        scratch_shapes=[pltpu.VMEM((nb, buf_rows, 3 * G_WIDTH), F32)],
        compiler_params=_cparams(("arbitrary", "arbitrary")),
        name="gdn",
    )(*args)


def _layer_norm(y, g, b):
    mu = jnp.mean(y, axis=-1, keepdims=True)
    yc = y - mu
    var = jnp.mean(yc * yc, axis=-1, keepdims=True)
    return yc * lax.rsqrt(var + LN_EPS) * g + b


def _outproj_kernel(hm_ref, hg_ref, x_ref, gt_ref, w_ref, g_ref, b_ref, o_ref):
    mix = (_dot(hm_ref[...].astype(BF16), w_ref[0:M_WIDTH, :])
           + _dot(hg_ref[...].astype(BF16), w_ref[M_WIDTH:M_WIDTH + G_WIDTH, :]))
    y = DEEPNORM_ALPHA * x_ref[...] + (1.0 + gt_ref[0]) * mix
    o_ref[...] = _layer_norm(y, g_ref[...], b_ref[...])


def _outproj(hm, hg, x2d, gt, w_out, ln_g, ln_b, tm, tiles_per_mod):
    m, d = x2d.shape
    vec = pl.BlockSpec((1, d), lambda i: (0, 0))
    return pl.pallas_call(
        _outproj_kernel,
        grid=(m // tm,),
        in_specs=[pl.BlockSpec((tm, M_WIDTH), lambda i: (i, 0)),
                  pl.BlockSpec((tm, G_WIDTH), lambda i: (i, 0)),
                  pl.BlockSpec((tm, d), lambda i: (i, 0)),
                  _mod_spec(gt, tiles_per_mod, 1),
                  pl.BlockSpec((d, d), lambda i: (0, 0)),
                  vec, vec],
        out_specs=pl.BlockSpec((tm, d), lambda i: (i, 0)),
        out_shape=jax.ShapeDtypeStruct((m, d), F32),
        compiler_params=_cparams(("arbitrary",)),
        name="outproj",
    )(hm, hg, x2d, gt, w_out, ln_g.reshape(1, d), ln_b.reshape(1, d))


def _ffn_kernel(x_ref, sc_ref, sh_ref, gt_ref, wg_ref, wu_ref, wd_ref, g_ref, b_ref, o_ref, h_scr, acc):
    f = pl.program_id(1)

    @pl.when(f == 0)
    def _():
        h_scr[...] = (x_ref[...] * (1.0 + sc_ref[0]) + sh_ref[0]).astype(BF16)
        acc[...] = jnp.zeros_like(acc)

    h = h_scr[...]
    gate = _dot(h, wg_ref[...])
    up = _dot(h, wu_ref[...])
    act = (gate * jax.nn.sigmoid(gate) * up).astype(BF16)
    acc[...] += _dot(act, wd_ref[...])

    @pl.when(f == pl.num_programs(1) - 1)
    def _():
        y = DEEPNORM_ALPHA * x_ref[...] + (1.0 + gt_ref[0]) * acc[...]
        o_ref[...] = _layer_norm(y, g_ref[...], b_ref[...])


def _ffn(x2d, sc, sh, gt, w_gu, w_down, ln_g, ln_b, tm, tiles_per_mod):
    m, d = x2d.shape
    tf = TF_FFN
    nf = D_FF // tf
    mod_spec = _mod_spec(sc, tiles_per_mod, 2)
    vec = pl.BlockSpec((1, d), lambda i, f: (0, 0))
    return pl.pallas_call(
        _ffn_kernel,
        grid=(m // tm, nf),
        in_specs=[pl.BlockSpec((tm, d), lambda i, f: (i, 0)),
                  mod_spec, mod_spec, mod_spec,
                  pl.BlockSpec((d, tf), lambda i, f: (0, f)),
                  pl.BlockSpec((d, tf), lambda i, f: (0, nf + f)),
                  pl.BlockSpec((tf, d), lambda i, f: (f, 0)),
                  vec, vec],
        out_specs=pl.BlockSpec((tm, d), lambda i, f: (i, 0)),
        out_shape=jax.ShapeDtypeStruct((m, d), F32),
        scratch_shapes=[pltpu.VMEM((tm, d), BF16), pltpu.VMEM((tm, d), F32)],
        compiler_params=_cparams(("arbitrary", "arbitrary")),
        name="ffn",
    )(x2d, sc, sh, gt, w_gu, w_gu, w_down, ln_g.reshape(1, d), ln_b.reshape(1, d))


def _layer(x, ada6, weights, init_m, init_g, nb):
    batch, seq, d = x.shape
    L = CHUNK if seq % CHUNK == 0 else seq
    nc = seq // L
    m = batch * seq
    x2d = x.reshape(m, d)

    def tiling(tm_cap):
        tm = min(tm_cap, m)
        if seq % tm == 0:
            return tm, seq // tm, lambda a: a.reshape(batch, 1, d)
        return tm, 1, lambda a: jnp.repeat(a, seq, axis=0).reshape(m // tm, tm, d)

    sh1, sc1, gt1, sh2, sc2, gt2 = ada6
    tm, tpm, mod = tiling(TM_INPROJ)
    proj, gates = _inproj(x2d, mod(sc1), mod(sh1), weights["w_in_main"], weights["w_in_gate"], tm, tpm)
    gates_t = jnp.swapaxes(gates.reshape(batch, nc, L, GATE_PAD), 2, 3)
    proj_v, gates_v = _tok_view(proj, batch, nc), _tok_view(gates, batch, nc)
    gvecs = weights["gvecs"]
    hm, c1, n1, m1 = _mlstm(proj_v, gates_v, gates_t, gvecs, weights["m_norm_g"], batch, nc, L, nb["mlstm"],
                            init_m)
    hg, s1, conv1 = _gdn(proj_v, gates_v, gates_t, gvecs, weights["conv_w"], weights["g_norm_g"], batch, nc, L,
                         nb["gdn"], init_g)
    tm, tpm, mod = tiling(TM_OUTPROJ)
    x1 = _outproj(hm.reshape(m, M_WIDTH), hg.reshape(m, G_WIDTH), x2d, mod(gt1), weights["w_out"],
                  weights["ln1_g"], weights["ln1_b"], tm, tpm)
    tm, tpm, mod = tiling(TM_FFN)
    y = _ffn(x1, mod(sc2), mod(sh2), mod(gt2), weights["w_gu"], weights["w_down"], weights["ln2_g"],
             weights["ln2_b"], tm, tpm)
    return y.reshape(batch, seq, d), c1, n1, m1[:, 0, :M_HEADS], s1, conv1


def kernel(x_prompt, x_sample, state_mlstm_C, state_mlstm_n, state_mlstm_m, state_gdn_S, state_gdn_conv,
           c_prompt, c_sample, w_ada, b_ada, w_in, m_i_bias, m_f_bias, m_norm_g, conv_w, g_dt_bias,
           g_A_log, g_norm_g, w_out, ln1_g, ln1_b, w_gu, w_down, ln2_g, ln2_b):
    bp, seq_p, d = x_prompt.shape
    bs, seq_s, _ = x_sample.shape

    ada = _ada(jnp.concatenate([c_prompt, c_sample], axis=0), w_ada, b_ada)
    ada6 = [ada[:, i * d:(i + 1) * d] for i in range(6)]

    gate_bias = jnp.zeros((GATE_PAD,), F32)
    gate_bias = gate_bias.at[GI0:GI0 + M_HEADS].set(m_i_bias).at[GF0:GF0 + M_HEADS].set(m_f_bias)
    gate_bias = gate_bias.at[GA0:GA0 + G_HEADS].set(g_dt_bias)
    a_log = jnp.zeros((GATE_PAD,), F32).at[GA0:GA0 + G_HEADS].set(g_A_log)
    weights = {
        "w_in_main": _cast_bf16(w_in.T, MAIN_COLS, d),
        "w_in_gate": jnp.pad(w_in.T[MAIN_COLS:], ((0, GATE_PAD - GATE_COLS), (0, 0))).astype(BF16),
        "gvecs": (gate_bias.reshape(1, GATE_PAD), gate_bias.reshape(GATE_PAD, 1),
                  a_log.reshape(1, GATE_PAD), a_log.reshape(GATE_PAD, 1)),
        "m_norm_g": m_norm_g, "conv_w": conv_w, "g_norm_g": g_norm_g,
        "w_out": _cast_bf16(w_out, *w_out.shape), "ln1_g": ln1_g, "ln1_b": ln1_b,
        "w_gu": _cast_bf16(w_gu, *w_gu.shape), "w_down": _cast_bf16(w_down, *w_down.shape),
        "ln2_g": ln2_g, "ln2_b": ln2_b,
    }

    y_p, p_c, p_n, p_m, p_s, p_conv = _layer(x_prompt, [a[:bp] for a in ada6], weights, None, None,
                                             SEQS_PER_STEP_PROMPT)
    m0 = jnp.pad(state_mlstm_m, ((0, 0), (0, GATE_PAD - M_HEADS))).reshape(bs, 1, GATE_PAD)
    y_s, s_c, s_n, s_m, s_s, s_conv = _layer(
        x_sample, [a[bp:] for a in ada6], weights, (state_mlstm_C, state_mlstm_n, m0),
        (state_gdn_S, state_gdn_conv), SEQS_PER_STEP_SAMPLE)
    return (y_p, y_s, p_c, p_n, p_m, p_s, p_conv, s_c, s_n, s_m, s_s, s_conv)
```

```python
import functools

import jax
import jax.numpy as jnp
from jax import lax
from jax.experimental import pallas as pl
from jax.experimental.pallas import tpu as pltpu

F32 = jnp.float32
BF16 = jnp.bfloat16

D_MODEL = 2048
M_HEADS = 4
M_HEAD_DIM = 256
M_WIDTH = M_HEADS * M_HEAD_DIM
G_HEADS = 8
G_HEAD_DIM = 128
G_WIDTH = G_HEADS * G_HEAD_DIM
CONV_W = 4
CHUNK = 64
D_FF = 5632
MAIN_COLS = 4 * M_WIDTH + 3 * G_WIDTH + G_WIDTH
GATE_COLS = 2 * M_HEADS + 2 * G_HEADS
GATE_PAD = 128
DEEPNORM_ALPHA = 2.0 ** 0.25
LN_EPS = 1e-5
RMS_EPS = 1e-6
GI0, GF0, GB0, GA0 = 0, M_HEADS, 2 * M_HEADS, 2 * M_HEADS + G_HEADS

SUBLANES = 8
VMEM_LIMIT_BYTES = 56 * 1024 * 1024
TM_INPROJ, TM_OUTPROJ, TM_FFN = 1024, 512, 512
TN_INPROJ, TN_ADA, TF_FFN = 1024, 1024, 512
CAST_ROWS, CAST_COLS = 2048, 1024
SEQS_PER_STEP_PROMPT = {"mlstm": 1, "gdn": 2}
SEQS_PER_STEP_SAMPLE = {"mlstm": 4, "gdn": 4}
SMALL_L = 8
PAIR_ROWS = 16


def _cparams(sem, flags=None):
    return pltpu.CompilerParams(dimension_semantics=sem, vmem_limit_bytes=VMEM_LIMIT_BYTES, flags=flags)


def _dot(a, b):
    return jnp.dot(a, b, preferred_element_type=F32)


def _dot_nt(a, b):
    return lax.dot_general(a, b, (((1,), (1,)), ((), ())), preferred_element_type=F32)


def _dot_tn(a, b):
    return lax.dot_general(a, b, (((0,), (0,)), ((), ())), preferred_element_type=F32)


def _split2(x):
    hi = x.astype(BF16)
    return hi, (x - hi.astype(F32)).astype(BF16)


def _split3(x):
    hi = x.astype(BF16)
    r = x - hi.astype(F32)
    mid = r.astype(BF16)
    return hi, mid, (r - mid.astype(F32)).astype(BF16)


def _dot3(a, b):
    return _dot(a[0], b[0]) + (_dot(a[0], b[1]) + _dot(a[1], b[0]))


def _mm_small(a, b):
    out = a[:, 0:1] * b[0:1, :]
    for i in range(1, a.shape[1]):
        out = out + a[:, i:i + 1] * b[i:i + 1, :]
    return out


def _tok_get(ref, b, L, cols):
    if len(ref.shape) == 3:
        return ref[b, :, cols]
    return ref[b * L:(b + 1) * L, cols]


def _tok_set(ref, b, L, cols, val):
    if len(ref.shape) == 3:
        ref[b, :, cols] = val
    else:
        ref[b * L:(b + 1) * L, cols] = val


ALL = slice(None)


def _cast_kernel(x_ref, o_ref):
    o_ref[...] = x_ref[...].astype(BF16)


def _cast_bf16(w, rows, cols):
    bc = min(cols, CAST_COLS)
    br = max(r for r in range(SUBLANES, min(rows, CAST_ROWS) + 1, SUBLANES) if rows % r == 0)
    return pl.pallas_call(
        _cast_kernel,
        grid=(rows // br, cols // bc),
        in_specs=[pl.BlockSpec((br, bc), lambda i, j: (i, j))],
        out_specs=pl.BlockSpec((br, bc), lambda i, j: (i, j)),
        out_shape=jax.ShapeDtypeStruct((rows, cols), BF16),
        compiler_params=_cparams(("arbitrary", "arbitrary")),
        name="cast",
    )(w)


def _ada_kernel(c_ref, w_ref, b_ref, o_ref):
    c = c_ref[...]
    a = (c * jax.nn.sigmoid(c)).astype(BF16)
    o_ref[...] = _dot(a, w_ref[...].astype(BF16)) + b_ref[...]


def _ada(c_all, w_ada, b_ada):
    n_rows, d = c_all.shape
    n_cols = w_ada.shape[1]
    tn = TN_ADA
    return pl.pallas_call(
        _ada_kernel,
        grid=(n_cols // tn,),
        in_specs=[pl.BlockSpec((n_rows, d), lambda j: (0, 0)),
                  pl.BlockSpec((d, tn), lambda j: (0, j)),
                  pl.BlockSpec((1, tn), lambda j: (0, j))],
        out_specs=pl.BlockSpec((n_rows, tn), lambda j: (0, j)),
        out_shape=jax.ShapeDtypeStruct((n_rows, n_cols), F32),
        compiler_params=_cparams(("arbitrary",)),
        name="ada",
    )(c_all, w_ada, b_ada.reshape(1, n_cols))


def _inproj_kernel(x_ref, sc_ref, sh_ref, wt_ref, wgt_ref, o_ref, og_ref, h_scr):
    @pl.when(pl.program_id(1) == 0)
    def _():
        h = (x_ref[...] * (1.0 + sc_ref[0]) + sh_ref[0]).astype(BF16)
        h_scr[...] = h
        og_ref[...] = _dot_nt(h, wgt_ref[...])

    o_ref[...] = _dot_nt(h_scr[...], wt_ref[...])


def _mod_spec(mod, tiles_per_mod, grid_rank):
    _, mod_rows, d = mod.shape
    if grid_rank == 1:
        index_map = lambda i: (i // tiles_per_mod, 0, 0)
    else:
        index_map = lambda i, j: (i // tiles_per_mod, 0, 0)
    if mod_rows == 1:
        return pl.BlockSpec((1, 1, d), index_map)
    return pl.BlockSpec((1, mod_rows, d), index_map, pipeline_mode=pl.Buffered(1))


def _inproj(x2d, sc, sh, w_main, w_gate, tm, tiles_per_mod):
    m, d = x2d.shape
    tn = TN_INPROJ
    mod_spec = _mod_spec(sc, tiles_per_mod, 2)
    return pl.pallas_call(
        _inproj_kernel,
        grid=(m // tm, MAIN_COLS // tn),
        in_specs=[pl.BlockSpec((tm, d), lambda i, j: (i, 0)),
                  mod_spec, mod_spec,
                  pl.BlockSpec((tn, d), lambda i, j: (j, 0)),
                  pl.BlockSpec((GATE_PAD, d), lambda i, j: (0, 0))],
        out_specs=[pl.BlockSpec((tm, tn), lambda i, j: (i, j)),
                   pl.BlockSpec((tm, GATE_PAD), lambda i, j: (i, 0))],
        out_shape=[jax.ShapeDtypeStruct((m, MAIN_COLS), F32),
                   jax.ShapeDtypeStruct((m, GATE_PAD), F32)],
        scratch_shapes=[pltpu.VMEM((tm, d), BF16)],
        compiler_params=_cparams(("arbitrary", "arbitrary")),
        name="inproj",
    )(x2d, sc, sh, w_main, w_gate)


def _gate_tables(g, bias, alog, gid):
    x = g + bias
    is_f = (gid >= GF0) & (gid < GB0)
    is_a = (gid >= GA0) & (gid < GA0 + G_HEADS)
    log_f = jax.nn.log_sigmoid(x)
    log_a = -jnp.exp(alog) * jax.nn.softplus(x)
    inc = jnp.where(is_f, log_f, jnp.where(is_a, log_a, 0.0))
    return x, inc, jax.nn.sigmoid(x)


def _chunk_masks(L):
    row = lax.broadcasted_iota(jnp.int32, (L, L), 0)
    col = lax.broadcasted_iota(jnp.int32, (L, L), 1)
    return row >= col, row > col, row == col


def _gates_col_form(g_col, gb_row_ref, al_row_ref, L):
    causal, _, _ = _chunk_masks(L)
    tril = causal.astype(F32)
    gid_c = lax.broadcasted_iota(jnp.int32, (L, GATE_PAD), 1)
    x_c, inc_c, beta_c = _gate_tables(g_col, gb_row_ref[...], al_row_ref[...], gid_c)
    if L <= SMALL_L:
        cum_c = _mm_small(tril, inc_c)
    else:
        tril_b = tril.astype(BF16)
        c1, c2, c3 = _split3(inc_c)
        cum_c = _dot(tril_b, c1) + (_dot(tril_b, c2) + _dot(tril_b, c3))
    return x_c, cum_c, beta_c


def _gates_both_forms(g_col, g_row, gb_row_ref, gb_col_ref, al_row_ref, al_col_ref, L):
    x_c, cum_c, beta_c = _gates_col_form(g_col, gb_row_ref, al_row_ref, L)
    triu = (lax.broadcasted_iota(jnp.int32, (L, L), 0) <= lax.broadcasted_iota(jnp.int32, (L, L), 1)).astype(F32)
    gid_r = lax.broadcasted_iota(jnp.int32, (GATE_PAD, L), 0)
    x_r, inc_r, _ = _gate_tables(g_row, gb_col_ref[...], al_col_ref[...], gid_r)
    if L <= SMALL_L:
        cum_r = _mm_small(inc_r, triu)
    else:
        triu_b = triu.astype(BF16)
        r1, r2, r3 = _split3(inc_r)
        cum_r = _dot(r1, triu_b) + (_dot(r2, triu_b) + _dot(r3, triu_b))
    return x_c, cum_c, beta_c, x_r, cum_r


def _rms_gate(h, gain, gate):
    return h * lax.rsqrt(jnp.mean(h * h, axis=-1, keepdims=True) + RMS_EPS) * gain * gate


def _mlstm_kernel(*refs, L, nb, nc, has_init):
    if has_init:
        (q_ref, k_ref, v_ref, o_ref, gcol_ref, grow_ref, gb_row_ref, gb_col_ref, al_row_ref, al_col_ref,
         ng_ref, c0_ref, n0_ref, m0_ref, h_ref, c_ref, n_ref, m_ref) = refs
    else:
        (q_ref, k_ref, v_ref, o_ref, gcol_ref, grow_ref, gb_row_ref, gb_col_ref, al_row_ref, al_col_ref,
         ng_ref, h_ref, c_ref, n_ref, m_ref) = refs

    if has_init and nc == 1:
        c_src, n_src, m_src = c0_ref, n0_ref, m0_ref
    else:
        c_src, n_src, m_src = c_ref, n_ref, m_ref

        @pl.when(pl.program_id(1) == 0)
        def _():
            if has_init:
                c_ref[...] = c0_ref[...]
                n_ref[...] = n0_ref[...]
                m_ref[...] = m0_ref[...]
            else:
                c_ref[...] = jnp.zeros_like(c_ref)
                n_ref[...] = jnp.zeros_like(n_ref)
                m_ref[...] = jnp.zeros_like(m_ref)

    causal, _, _ = _chunk_masks(L)
    lane = lax.broadcasted_iota(jnp.int32, (1, GATE_PAD), 1)
    scale = M_HEAD_DIM ** -0.5
    gates = [_gates_both_forms(_tok_get(gcol_ref, b, L, ALL), grow_ref[b, 0], gb_row_ref, gb_col_ref,
                               al_row_ref, al_col_ref, L) for b in range(nb)]
    m_all = [m_src[b] for b in range(nb)]
    probs = [(b, h) for b in range(nb) for h in range(M_HEADS)]

    def cols(h):
        return slice(h * M_HEAD_DIM, (h + 1) * M_HEAD_DIM)

    q = [_tok_get(q_ref, b, L, cols(h)) for b, h in probs]
    k = [_tok_get(k_ref, b, L, cols(h)) * scale for b, h in probs]
    vb = [_tok_get(v_ref, b, L, cols(h)).astype(BF16) for b, h in probs]
    qb = [x.astype(BF16) for x in q]
    kb = [x.astype(BF16) for x in k]
    ig_c = [gates[b][0][:, GI0 + h:GI0 + h + 1] for b, h in probs]
    ig_r = [gates[b][3][GI0 + h:GI0 + h + 1, :] for b, h in probs]
    bt_c = [gates[b][1][:, GF0 + h:GF0 + h + 1] for b, h in probs]
    bt_r = [gates[b][4][GF0 + h:GF0 + h + 1, :] for b, h in probs]
    m0 = [jnp.sum(jnp.where(lane == h, m_all[b], 0.0), axis=1, keepdims=True) for b, h in probs]
    n_p = len(probs)
    rng = range(n_p)

    log_d = [jnp.where(causal, bt_c[i] - bt_r[i] + ig_r[i], -jnp.inf) for i in rng]
    inter = [bt_c[i] + m0[i] for i in rng]
    m_t = [jnp.maximum(inter[i], jnp.max(log_d[i], axis=1, keepdims=True)) for i in rng]
    inter_w = [jnp.exp(inter[i] - m_t[i]) for i in rng]
    qk = [_dot_nt(qb[i], kb[i]) for i in rng]
    c_old = [c_src[b, h] for b, h in probs]
    n_old = [n_src[b, h:h + 1, :] for b, h in probs]
    qc = [_dot(qb[i], c_old[i].astype(BF16)) for i in rng]
    s = [qk[i] * jnp.exp(log_d[i] - m_t[i]) for i in rng]
    sv = [_dot(s[i].astype(BF16), vb[i]) for i in rng]

    b_last = [bt_c[i][L - 1:L, :] for i in rng]
    m_new = [jnp.maximum(b_last[i] + m0[i], jnp.max(b_last[i] - bt_r[i] + ig_r[i], axis=1, keepdims=True))
             for i in rng]
    kw = [k[i] * jnp.exp(b_last[i] - bt_c[i] + ig_c[i] - m_new[i]) for i in rng]
    decay = [jnp.exp(b_last[i] + m0[i] - m_new[i]) for i in rng]
    kv = [_dot_tn(kw[i].astype(BF16), vb[i]) for i in rng]

    m_next = list(m_all)
    for i, (b, h) in enumerate(probs):
        num = inter_w[i] * qc[i] + sv[i]
        den = inter_w[i] * jnp.sum(q[i] * n_old[i], axis=1, keepdims=True) + jnp.sum(s[i], axis=1, keepdims=True)
        hh = num / jnp.maximum(jnp.abs(den), jnp.exp(-m_t[i]))
        c_ref[b, h] = decay[i] * c_old[i] + kv[i]
        n_ref[b, h:h + 1, :] = decay[i] * n_old[i] + jnp.sum(kw[i], axis=0, keepdims=True)
        m_next[b] = jnp.where(lane == h, m_new[i], m_next[b])
        gate = jax.nn.sigmoid(_tok_get(o_ref, b, L, cols(h)))
        _tok_set(h_ref, b, L, cols(h), _rms_gate(hh, ng_ref[:, cols(h)], gate).astype(h_ref.dtype))
    for b in range(nb):
        m_ref[b] = m_next[b]


def _tok_spec(L, nb, nc, width, col_block):
    if nc == 1:
        return pl.BlockSpec((nb * L, width), lambda b, c: (b, col_block))
    return pl.BlockSpec((nb, L, width), lambda b, c: (b, c, col_block))


def _tok_view(x2d, batch, nc):
    return x2d if nc == 1 else x2d.reshape(batch, x2d.shape[0] // batch, x2d.shape[1])


def _gate_specs(L, nb, nc):
    vec_r = pl.BlockSpec((1, GATE_PAD), lambda b, c: (0, 0))
    vec_c = pl.BlockSpec((GATE_PAD, 1), lambda b, c: (0, 0))
    return [_tok_spec(L, nb, nc, GATE_PAD, 0),
            pl.BlockSpec((nb, 1, GATE_PAD, L), lambda b, c: (b, c, 0, 0)),
            vec_r, vec_c, vec_r, vec_c]


def _scan_out_dtype(L):
    return BF16 if L % 16 == 0 else F32


def _mlstm(proj, gates, gates_t, gvecs, norm_g, batch, nc, L, nb, init):
    has_init = init is not None
    in_specs = [_tok_spec(L, nb, nc, M_WIDTH, j) for j in range(4)] + _gate_specs(L, nb, nc)
    in_specs.append(pl.BlockSpec((1, M_WIDTH), lambda b, c: (0, 0)))
    c_spec = pl.BlockSpec((nb, M_HEADS, M_HEAD_DIM, M_HEAD_DIM), lambda b, c: (b, 0, 0, 0))
    n_spec = pl.BlockSpec((nb, M_HEADS, M_HEAD_DIM), lambda b, c: (b, 0, 0))
    m_spec = pl.BlockSpec((nb, 1, GATE_PAD), lambda b, c: (b, 0, 0))
    args = [proj, proj, proj, proj, gates, gates_t, *gvecs, norm_g.reshape(1, M_WIDTH)]
    if has_init:
        in_specs += [c_spec, n_spec, m_spec]
        args += list(init)
    return pl.pallas_call(
        functools.partial(_mlstm_kernel, L=L, nb=nb, nc=nc, has_init=has_init),
        grid=(batch // nb, nc),
        in_specs=in_specs,
        out_specs=[_tok_spec(L, nb, nc, M_WIDTH, 0), c_spec, n_spec, m_spec],
        out_shape=[jax.ShapeDtypeStruct(proj.shape[:-1] + (M_WIDTH,), _scan_out_dtype(L)),
                   jax.ShapeDtypeStruct((batch, M_HEADS, M_HEAD_DIM, M_HEAD_DIM), F32),
                   jax.ShapeDtypeStruct((batch, M_HEADS, M_HEAD_DIM), F32),
                   jax.ShapeDtypeStruct((batch, 1, GATE_PAD), F32)],
        compiler_params=_cparams(("arbitrary", "arbitrary")),
        name="mlstm",
    )(*args)


def _inv_unit_lower_small(a_list, L):
    _, _, eye = _chunk_masks(L)
    p = [-a for a in a_list]
    t = [eye.astype(F32) + x for x in p]
    for _ in range(max(1, (L - 1).bit_length()) - 1):
        p = [_mm_small(x, x) for x in p]
        t = [ti + _mm_small(ti, pi) for ti, pi in zip(t, p)]
    return t


def _gate_rows_paired(gp, bias_p, alog_p, L):
    rid = lax.broadcasted_iota(jnp.int32, gp.shape, 0)
    x = gp + bias_p
    is_f = (rid >= GF0 // 2) & (rid < GB0 // 2)
    is_a = (rid >= GA0 // 2) & (rid < (GA0 + G_HEADS) // 2)
    inc = jnp.where(is_f, jax.nn.log_sigmoid(x), jnp.where(is_a, -jnp.exp(alog_p) * jax.nn.softplus(x), 0.0))
    r = lax.broadcasted_iota(jnp.int32, (2 * L, 2 * L), 0)
    c = lax.broadcasted_iota(jnp.int32, (2 * L, 2 * L), 1)
    triu2 = (((r < L) == (c < L)) & (r <= c)).astype(BF16)
    p1, p2, p3 = _split3(inc)
    return _dot(p1, triu2) + (_dot(p2, triu2) + _dot(p3, triu2))


def _gdn_heads_paired(qn, kn, vv, gates_c, cum_p, s_ref, z_ref, h_ref, ng_ref, L, nb):
    dh = G_HEAD_DIM
    row =lax.broadcasted_iota(jnp.int32, (L, 2 * L), 0)
    lane = lax.broadcasted_iota(jnp.int32, (L, 2 * L), 1)
    left = lane < L
    colp = jnp.where(left, lane, lane - L)
    causal_p, strict_p, eye_p = row >= colp, row > colp, (row == colp).astype(F32)
    left2 = lax.broadcasted_iota(jnp.int32, (L, 2 * dh), 1) < dh
    lane4 = lax.broadcasted_iota(jnp.int32, (L, 4 * dh), 1)
    first4 = (lane4 // dh) % 2 == 0

    def bdiag(y, first):
        z = jnp.zeros_like(y)
        return jnp.concatenate([jnp.where(first, y, z), jnp.where(first, z, y)], axis=0)

    pairs = [(b, p) for b in range(nb) for p in range(G_HEADS // 2)]
    rng = range(len(pairs))
    q2 = [qn[b][:, 2 * p * dh:(2 * p + 2) * dh] for b, p in pairs]
    k2 = [kn[b][:, 2 * p * dh:(2 * p + 2) * dh] for b, p in pairs]
    v2 = [vv[b][:, 2 * p * dh:(2 * p + 2) * dh] for b, p in pairs]
    q2b = [x.astype(BF16) for x in q2]
    k2b = [x.astype(BF16) for x in k2]
    kbd = [bdiag(x, left2) for x in k2b]
    b_ca = [gates_c[b][1][:, GA0 + 2 * p:GA0 + 2 * p + 1] for b, p in pairs]
    b_cb = [gates_c[b][1][:, GA0 + 2 * p + 1:GA0 + 2 * p + 2] for b, p in pairs]
    bet_a = [gates_c[b][2][:, GB0 + 2 * p:GB0 + 2 * p + 1] for b, p in pairs]
    bet_b = [gates_c[b][2][:, GB0 + 2 * p + 1:GB0 + 2 * p + 2] for b, p in pairs]
    b_r = [cum_p[b][GA0 // 2 + p:GA0 // 2 + p + 1, :] for b, p in pairs]
    decay = [jnp.exp(jnp.where(causal_p, jnp.where(left, b_ca[i], b_cb[i]) - b_r[i], -jnp.inf)) for i in rng]
    kk = [_dot_nt(k2b[i], kbd[i]) for i in rng]
    qk = [(_dot_nt(q2b[i], kbd[i]) * decay[i]).astype(BF16) for i in rng]
    s_old = [(s_ref[b, 2 * p], s_ref[b, 2 * p + 1]) for b, p in pairs]
    zero = jnp.zeros((dh, dh), BF16)
    sbd = [jnp.concatenate([jnp.concatenate([sa.astype(BF16), zero], axis=1),
                            jnp.concatenate([zero, sb.astype(BF16)], axis=1)], axis=0) for sa, sb in s_old]
    qs = [_dot(q2b[i], sbd[i]) for i in rng]

    a = [jnp.where(strict_p, jnp.where(left, bet_a[i], bet_b[i]) * kk[i] * decay[i], 0.0) for i in rng]
    p = [-x for x in a]
    t = [eye_p + x for x in p]
    ps = [_split2(x) for x in p]
    p = [_dot3(ps[i], (bdiag(ps[i][0], left), bdiag(ps[i][1], left))) for i in rng]
    for lvl in range(1, (L - 1).bit_length()):
        last = lvl == (L - 1).bit_length() - 1
        ps = [_split2(x) for x in p]
        pbd = [(bdiag(hi, left), bdiag(lo, left)) for hi, lo in ps]
        if last:
            t = [t[i] + _dot3(_split2(t[i]), pbd[i]) for i in rng]
        else:
            ts = [_split2(x) for x in t]
            both = [_dot3((jnp.concatenate([ts[i][0], ps[i][0]], axis=0),
                           jnp.concatenate([ts[i][1], ps[i][1]], axis=0)), pbd[i]) for i in rng]
            t = [t[i] + both[i][0:L] for i in rng]
            p = [both[i][L:2 * L] for i in rng]

    bet2 = [jnp.where(left2, bet_a[i], bet_b[i]) for i in rng]
    eb2 = [jnp.where(left2, jnp.exp(b_ca[i]), jnp.exp(b_cb[i])) for i in rng]
    rhs = [_split2(jnp.concatenate([bet2[i] * v2[i], (bet2[i] * eb2[i]) * k2[i]], axis=-1)) for i in rng]
    sol = [_dot3(_split2(t[i]), (bdiag(rhs[i][0], first4), bdiag(rhs[i][1], first4))) for i in rng]
    u = [sol[i][:, 0:2 * dh] - _dot(sol[i][:, 2 * dh:4 * dh].astype(BF16), sbd[i]) for i in rng]
    ub = [x.astype(BF16) for x in u]
    o = [eb2[i] * qs[i] + _dot(qk[i], bdiag(ub[i], left2)) for i in rng]
    bl_a = [b_ca[i][L - 1:L, :] for i in rng]
    bl_b = [b_cb[i][L - 1:L, :] for i in rng]
    wk = [(jnp.where(left2, jnp.exp(bl_a[i] - b_ca[i]), jnp.exp(bl_b[i] - b_cb[i])) * k2[i]).astype(BF16)
          for i in rng]
    ds = [_dot_tn(wk[i], ub[i]) for i in rng]
    for i, (b, p) in enumerate(pairs):
        for j, bl in enumerate((bl_a[i], bl_b[i])):
            h = 2 * p + j
            blk = slice(j * dh, (j + 1) * dh)
            cols = slice(h * dh, (h + 1) * dh)
            s_ref[b, h] = jnp.exp(bl) * s_old[i][j] + ds[i][blk, blk]
            z = _tok_get(z_ref, b, L, cols)
            _tok_set(h_ref, b, L, cols,
                     _rms_gate(o[i][:, blk], ng_ref[:, cols], z * jax.nn.sigmoid(z)).astype(h_ref.dtype))


CONV_HIST = CONV_W - 1
CONV_BASE = SUBLANES - CONV_HIST


def _unit(x):
    return x * lax.rsqrt(jnp.sum(x * x, axis=-1, keepdims=True) + RMS_EPS)


def _gdn_conv_act(b, xq_ref, xk_ref, xv_ref, cw_ref, conv_ref, buf, L, carry):
    buf[b, SUBLANES:SUBLANES + L, 0:G_WIDTH] = _tok_get(xq_ref, b, L, ALL)
    buf[b, SUBLANES:SUBLANES + L, G_WIDTH:2 * G_WIDTH] = _tok_get(xk_ref, b, L, ALL)
    buf[b, SUBLANES:SUBLANES + L, 2 * G_WIDTH:3 * G_WIDTH] = _tok_get(xv_ref, b, L, ALL)
    y = cw_ref[0:1, :] * buf[b, CONV_BASE:CONV_BASE + L, :]
    for j in range(1, CONV_W):
        y = y + cw_ref[j:j + 1, :] * buf[b, CONV_BASE + j:CONV_BASE + j + L, :]
    conv_ref[b] = buf[b, SUBLANES + L - CONV_HIST:SUBLANES + L, :]
    if carry:
        buf[b, 0:SUBLANES, :] = buf[b, L:L + SUBLANES, :]
    return y * jax.nn.sigmoid(y)


def _gdn_paired_kernel(xq_ref, xk_ref, xv_ref, z_ref, gcol_ref, gb_row_ref, al_row_ref, gp_ref, gbp_ref,
                       alp_ref, cw_ref, ng_ref, h_ref, s_ref, conv_ref, buf, *, L, nb):
    @pl.when(pl.program_id(1) == 0)
    def _():
        buf[:, 0:SUBLANES, :] = jnp.zeros((nb, SUBLANES, 3 * G_WIDTH), F32)
        s_ref[...] = jnp.zeros_like(s_ref)

    scale = G_HEAD_DIM ** -0.5
    qn, kn, vv = [], [], []
    for b in range(nb):
        act = _gdn_conv_act(b, xq_ref, xk_ref, xv_ref, cw_ref, conv_ref, buf, L, True)
        qn.append(jnp.concatenate(
            [_unit(act[:, h * G_HEAD_DIM:(h + 1) * G_HEAD_DIM]) * scale for h in range(G_HEADS)], axis=-1))
        kn.append(jnp.concatenate(
            [_unit(act[:, G_WIDTH + h * G_HEAD_DIM:G_WIDTH + (h + 1) * G_HEAD_DIM]) for h in range(G_HEADS)],
            axis=-1))
        vv.append(act[:, 2 * G_WIDTH:3 * G_WIDTH])
    gates_c = [_gates_col_form(gcol_ref[b], gb_row_ref, al_row_ref, L) for b in range(nb)]
    cum_p = [_gate_rows_paired(gp_ref[b, 0], gbp_ref[...], alp_ref[...], L) for b in range(nb)]
    _gdn_heads_paired(qn, kn, vv, gates_c, cum_p, s_ref, z_ref, h_ref, ng_ref, L, nb)


def _gdn_kernel(*refs, L, nb, nc, has_init):
    xq_ref, xk_ref, xv_ref, z_ref, gcol_ref, grow_ref, gb_row_ref, gb_col_ref, al_row_ref, al_col_ref = refs[:10]
    if has_init:
        cw_ref, ng_ref, s0_ref, conv0_ref, h_ref, s_ref, conv_ref, buf = refs[10:]
    else:
        cw_ref, ng_ref, h_ref, s_ref, conv_ref, buf = refs[10:]

    def cols(h):
        return slice(h * G_HEAD_DIM, (h + 1) * G_HEAD_DIM)

    s_src = s0_ref if (has_init and nc == 1) else s_ref

    @pl.when(pl.program_id(1) == 0)
    def _():
        buf[:, 0:SUBLANES, :] = jnp.zeros((nb, SUBLANES, 3 * G_WIDTH), F32)
        if has_init:
            for b in range(nb):
                buf[b, CONV_BASE:SUBLANES, :] = conv0_ref[b]
            if nc > 1:
                s_ref[...] = s0_ref[...]
        else:
            s_ref[...] = jnp.zeros_like(s_ref)

    act = [_gdn_conv_act(b, xq_ref, xk_ref, xv_ref, cw_ref, conv_ref, buf, L, nc > 1) for b in range(nb)]
    causal, strict, _ = _chunk_masks(L)
    scale = G_HEAD_DIM ** -0.5
    gates = [_gates_both_forms(_tok_get(gcol_ref, b, L, ALL), grow_ref[b, 0], gb_row_ref, gb_col_ref,
                               al_row_ref, al_col_ref, L) for b in range(nb)]
    probs = [(b, h) for b in range(nb) for h in range(G_HEADS)]
    rng = range(len(probs))

    q = [_unit(act[b][:, h * G_HEAD_DIM:(h + 1) * G_HEAD_DIM]) * scale for b, h in probs]
    k = [_unit(act[b][:, G_WIDTH + h * G_HEAD_DIM:G_WIDTH + (h + 1) * G_HEAD_DIM]) for b, h in probs]
    v = [act[b][:, 2 * G_WIDTH + h * G_HEAD_DIM:2 * G_WIDTH + (h + 1) * G_HEAD_DIM] for b, h in probs]
    qb = [x.astype(BF16) for x in q]
    kb = [x.astype(BF16) for x in k]
    b_c = [gates[b][1][:, GA0 + h:GA0 + h + 1] for b, h in probs]
    b_r = [gates[b][4][GA0 + h:GA0 + h + 1, :] for b, h in probs]
    bet = [gates[b][2][:, GB0 + h:GB0 + h + 1] for b, h in probs]
    decay = [jnp.exp(jnp.where(causal, b_c[i] - b_r[i], -jnp.inf)) for i in rng]
    eb = [jnp.exp(b_c[i]) for i in rng]
    kk = [_dot_nt(kb[i], kb[i]) for i in rng]
    qk = [_dot_nt(qb[i], kb[i]) * decay[i] for i in rng]
    s_old = [s_src[b, h] for b, h in probs]
    sb = [x.astype(BF16) for x in s_old]
    qs = [_dot(qb[i], sb[i]) for i in rng]

    t = _inv_unit_lower_small([jnp.where(strict, bet[i] * kk[i] * decay[i], 0.0) for i in rng], L)
    rhs = [jnp.concatenate([bet[i] * v[i], (bet[i] * eb[i]) * k[i]], axis=-1) for i in rng]
    sol = [_mm_small(t[i], rhs[i]) for i in rng]
    u = [sol[i][:, 0:G_HEAD_DIM] - _dot(sol[i][:, G_HEAD_DIM:2 * G_HEAD_DIM].astype(BF16), sb[i]) for i in rng]
    ub = [x.astype(BF16) for x in u]
    o = [eb[i] * qs[i] + _dot(qk[i].astype(BF16), ub[i]) for i in rng]
    b_last = [b_c[i][L - 1:L, :] for i in rng]
    wk = [(jnp.exp(b_last[i] - b_c[i]) * k[i]).astype(BF16) for i in rng]
    ds = [_dot_tn(wk[i], ub[i]) for i in rng]
    for i, (b, h) in enumerate(probs):
        s_ref[b, h] = jnp.exp(b_last[i]) * s_old[i] + ds[i]
        z = _tok_get(z_ref, b, L, cols(h))
        _tok_set(h_ref, b, L, cols(h),
                 _rms_gate(o[i], ng_ref[:, cols(h)], z * jax.nn.sigmoid(z)).astype(h_ref.dtype))


def _gdn_paired(proj, gates, gates_t, gvecs, conv_w, norm_g, batch, nc, L, nb):
    assert 2 * L == GATE_PAD and nc > 1
    blk0 = 4 * M_WIDTH // G_WIDTH

    def tok(width, col_block):
        return pl.BlockSpec((nb, L, width), lambda b, c: (b, c, col_block))

    vec_r = pl.BlockSpec((1, GATE_PAD), lambda b, c: (0, 0))
    tab = pl.BlockSpec((PAIR_ROWS, GATE_PAD), lambda b, c: (0, 0))
    gb_row, _, al_row, _ = gvecs
    pair = lambda vec: jnp.repeat(vec.reshape(GATE_PAD)[:2 * PAIR_ROWS], L).reshape(PAIR_ROWS, GATE_PAD)
    in_specs = [tok(G_WIDTH, blk0 + j) for j in range(4)]
    in_specs += [tok(GATE_PAD, 0), vec_r, vec_r,
                 pl.BlockSpec((nb, 1, PAIR_ROWS, GATE_PAD), lambda b, c: (b, c, 0, 0)), tab, tab,
                 pl.BlockSpec((CONV_W, 3 * G_WIDTH), lambda b, c: (0, 0)),
                 pl.BlockSpec((1, G_WIDTH), lambda b, c: (0, 0))]
    args = [proj, proj, proj, proj, gates, gb_row, al_row,
            gates_t.reshape(batch, nc, GATE_PAD // 2, GATE_PAD), pair(gb_row), pair(al_row),
            conv_w, norm_g.reshape(1, G_WIDTH)]
    s_spec = pl.BlockSpec((nb, G_HEADS, G_HEAD_DIM, G_HEAD_DIM), lambda b, c: (b, 0, 0, 0))
    conv_spec = pl.BlockSpec((nb, CONV_HIST, 3 * G_WIDTH), lambda b, c: (b, 0, 0))
    return pl.pallas_call(
        functools.partial(_gdn_paired_kernel, L=L, nb=nb),
        grid=(batch // nb, nc),
        in_specs=in_specs,
        out_specs=[tok(G_WIDTH, 0), s_spec, conv_spec],
        out_shape=[jax.ShapeDtypeStruct(proj.shape[:-1] + (G_WIDTH,), _scan_out_dtype(L)),
                   jax.ShapeDtypeStruct((batch, G_HEADS, G_HEAD_DIM, G_HEAD_DIM), F32),
                   jax.ShapeDtypeStruct((batch, CONV_HIST, 3 * G_WIDTH), F32)],
        scratch_shapes=[pltpu.VMEM((nb, SUBLANES + L, 3 * G_WIDTH), F32)],
        compiler_params=_cparams(("arbitrary", "arbitrary")),
        name="gdn",
    )(*args)


def _gdn(proj, gates, gates_t, gvecs, conv_w, norm_g, batch, nc, L, nb, init):
    if init is None and L > SMALL_L:
        return _gdn_paired(proj, gates, gates_t, gvecs, conv_w, norm_g, batch, nc, L, nb)
    assert L <= SMALL_L
    has_init = init is not None
    blk0 = 4 * M_WIDTH // G_WIDTH
    in_specs = [_tok_spec(L, nb, nc, G_WIDTH, blk0 + j) for j in range(4)] + _gate_specs(L, nb, nc)
    in_specs += [pl.BlockSpec((CONV_W, 3 * G_WIDTH), lambda b, c: (0, 0)),
                 pl.BlockSpec((1, G_WIDTH), lambda b, c: (0, 0))]
    s_spec = pl.BlockSpec((nb, G_HEADS, G_HEAD_DIM, G_HEAD_DIM), lambda b, c: (b, 0, 0, 0))
    conv_spec = pl.BlockSpec((nb, CONV_HIST, 3 * G_WIDTH), lambda b, c: (b, 0, 0))
    args = [proj, proj, proj, proj, gates, gates_t, *gvecs, conv_w, norm_g.reshape(1, G_WIDTH)]
    if has_init:
        in_specs += [s_spec, conv_spec]
        args += list(init)
    return pl.pallas_call(
        functools.partial(_gdn_kernel, L=L, nb=nb, nc=nc, has_init=has_init),
        grid=(batch // nb, nc),
        in_specs=in_specs,
        out_specs=[_tok_spec(L, nb, nc, G_WIDTH, 0), s_spec, conv_spec],
        out_shape=[jax.ShapeDtypeStruct(proj.shape[:-1] + (G_WIDTH,), _scan_out_dtype(L)),
                   jax.ShapeDtypeStruct((batch, G_HEADS, G_HEAD_DIM, G_HEAD_DIM), F32),
                   jax.ShapeDtypeStruct((batch, CONV_HIST, 3 * G_WIDTH), F32)],
        scratch_shapes=[pltpu.VMEM((nb, 2 * SUBLANES, 3 * G_WIDTH), F32)],
        compiler_params=_cparams(("arbitrary", "arbitrary")),
        name="gdn",
    )(*args)


def _layer_norm(y, g, b):
    mu = jnp.mean(y, axis=-1, keepdims=True)
    yc = y - mu
    var = jnp.mean(yc * yc, axis=-1, keepdims=True)
    return yc * lax.rsqrt(var + LN_EPS) * g + b


def _outproj_kernel(hm_ref, hg_ref, x_ref, gt_ref, w_ref, g_ref, b_ref, o_ref):
    mix = (_dot(hm_ref[...].astype(BF16), w_ref[0:M_WIDTH, :])
           + _dot(hg_ref[...].astype(BF16), w_ref[M_WIDTH:M_WIDTH + G_WIDTH, :]))
    y = DEEPNORM_ALPHA * x_ref[...] + (1.0 + gt_ref[0]) * mix
    o_ref[...] = _layer_norm(y, g_ref[...], b_ref[...])


def _outproj(hm, hg, x2d, gt, w_out, ln_g, ln_b, tm, tiles_per_mod):
    m, d = x2d.shape
    vec = pl.BlockSpec((1, d), lambda i: (0, 0))
    return pl.pallas_call(
        _outproj_kernel,
        grid=(m // tm,),
        in_specs=[pl.BlockSpec((tm, M_WIDTH), lambda i: (i, 0)),
                  pl.BlockSpec((tm, G_WIDTH), lambda i: (i, 0)),
                  pl.BlockSpec((tm, d), lambda i: (i, 0)),
                  _mod_spec(gt, tiles_per_mod, 1),
                  pl.BlockSpec((d, d), lambda i: (0, 0)),
                  vec, vec],
        out_specs=pl.BlockSpec((tm, d), lambda i: (i, 0)),
        out_shape=jax.ShapeDtypeStruct((m, d), F32),
        compiler_params=_cparams(("arbitrary",)),
        name="outproj",
    )(hm, hg, x2d, gt, w_out, ln_g.reshape(1, d), ln_b.reshape(1, d))


def _ffn_kernel(x_ref, sc_ref, sh_ref, gt_ref, wg_ref, wu_ref, wd_ref, g_ref, b_ref, o_ref, h_scr, acc):
    f = pl.program_id(1)

    @pl.when(f == 0)
    def _():
        h_scr[...] = (x_ref[...] * (1.0 + sc_ref[0]) + sh_ref[0]).astype(BF16)
        acc[...] = jnp.zeros_like(acc)

    h = h_scr[...]
    gate = _dot(h, wg_ref[...])
    up = _dot(h, wu_ref[...])
    act = (gate * jax.nn.sigmoid(gate) * up).astype(BF16)
    acc[...] += _dot(act, wd_ref[...])

    @pl.when(f == pl.num_programs(1) - 1)
    def _():
        y = DEEPNORM_ALPHA * x_ref[...] + (1.0 + gt_ref[0]) * acc[...]
        o_ref[...] = _layer_norm(y, g_ref[...], b_ref[...])


def _ffn(x2d, sc, sh, gt, w_gu, w_down, ln_g, ln_b, tm, tiles_per_mod):
    m, d = x2d.shape
    tf = TF_FFN
    nf = D_FF // tf
    mod_spec = _mod_spec(sc, tiles_per_mod, 2)
    vec = pl.BlockSpec((1, d), lambda i, f: (0, 0))
    return pl.pallas_call(
        _ffn_kernel,
        grid=(m // tm, nf),
        in_specs=[pl.BlockSpec((tm, d), lambda i, f: (i, 0)),
                  mod_spec, mod_spec, mod_spec,
                  pl.BlockSpec((d, tf), lambda i, f: (0, f)),
                  pl.BlockSpec((d, tf), lambda i, f: (0, nf + f)),
                  pl.BlockSpec((tf, d), lambda i, f: (f, 0)),
                  vec, vec],
        out_specs=pl.BlockSpec((tm, d), lambda i, f: (i, 0)),
        out_shape=jax.ShapeDtypeStruct((m, d), F32),
        scratch_shapes=[pltpu.VMEM((tm, d), BF16), pltpu.VMEM((tm, d), F32)],
        compiler_params=_cparams(("arbitrary", "arbitrary")),
        name="ffn",
    )(x2d, sc, sh, gt, w_gu, w_gu, w_down, ln_g.reshape(1, d), ln_b.reshape(1, d))


def _layer(x, ada6, weights, init_m, init_g, nb):
    batch, seq, d = x.shape
    L = CHUNK if seq % CHUNK == 0 else seq
    nc = seq // L
    m = batch * seq
    x2d = x.reshape(m, d)

    def tiling(tm_cap):
        tm = min(tm_cap, m)
        if seq % tm == 0:
            return tm, seq // tm, lambda a: a.reshape(batch, 1, d)
        return tm, 1, lambda a: jnp.repeat(a, seq, axis=0).reshape(m // tm, tm, d)

    sh1, sc1, gt1, sh2, sc2, gt2 = ada6
    tm, tpm, mod = tiling(TM_INPROJ)
    proj, gates = _inproj(x2d, mod(sc1), mod(sh1), weights["w_in_main"], weights["w_in_gate"], tm, tpm)
    gates_t = jnp.swapaxes(gates.reshape(batch, nc, L, GATE_PAD), 2, 3)
    proj_v, gates_v = _tok_view(proj, batch, nc), _tok_view(gates, batch, nc)
    gvecs = weights["gvecs"]
    hm, c1, n1, m1 = _mlstm(proj_v, gates_v, gates_t, gvecs, weights["m_norm_g"], batch, nc, L, nb["mlstm"],
                            init_m)
    hg, s1, conv1 = _gdn(proj_v, gates_v, gates_t, gvecs, weights["conv_w"], weights["g_norm_g"], batch, nc, L,
                         nb["gdn"], init_g)
    tm, tpm, mod = tiling(TM_OUTPROJ)
    x1 = _outproj(hm.reshape(m, M_WIDTH), hg.reshape(m, G_WIDTH), x2d, mod(gt1), weights["w_out"],
                  weights["ln1_g"], weights["ln1_b"], tm, tpm)
    tm, tpm, mod = tiling(TM_FFN)
    y = _ffn(x1, mod(sc2), mod(sh2), mod(gt2), weights["w_gu"], weights["w_down"], weights["ln2_g"],
             weights["ln2_b"], tm, tpm)
    return y.reshape(batch, seq, d), c1, n1, m1[:, 0, :M_HEADS], s1, conv1


def kernel(x_prompt, x_sample, state_mlstm_C, state_mlstm_n, state_mlstm_m, state_gdn_S, state_gdn_conv,
           c_prompt, c_sample, w_ada, b_ada, w_in, m_i_bias, m_f_bias, m_norm_g, conv_w, g_dt_bias,
           g_A_log, g_norm_g, w_out, ln1_g, ln1_b, w_gu, w_down, ln2_g, ln2_b):
    bp, seq_p, d = x_prompt.shape
    bs, seq_s, _ = x_sample.shape

    ada = _ada(jnp.concatenate([c_prompt, c_sample], axis=0), w_ada, b_ada)
    ada6 = [ada[:, i * d:(i + 1) * d] for i in range(6)]

    gate_bias = jnp.zeros((GATE_PAD,), F32)
    gate_bias = gate_bias.at[GI0:GI0 + M_HEADS].set(m_i_bias).at[GF0:GF0 + M_HEADS].set(m_f_bias)
    gate_bias = gate_bias.at[GA0:GA0 + G_HEADS].set(g_dt_bias)
    a_log = jnp.zeros((GATE_PAD,), F32).at[GA0:GA0 + G_HEADS].set(g_A_log)
    weights = {
        "w_in_main": _cast_bf16(w_in.T, MAIN_COLS, d),
        "w_in_gate": jnp.pad(w_in.T[MAIN_COLS:], ((0, GATE_PAD - GATE_COLS), (0, 0))).astype(BF16),
        "gvecs": (gate_bias.reshape(1, GATE_PAD), gate_bias.reshape(GATE_PAD, 1),
                  a_log.reshape(1, GATE_PAD), a_log.reshape(GATE_PAD, 1)),
        "m_norm_g": m_norm_g, "conv_w": conv_w, "g_norm_g": g_norm_g,
        "w_out": _cast_bf16(w_out, *w_out.shape), "ln1_g": ln1_g, "ln1_b": ln1_b,
        "w_gu": _cast_bf16(w_gu, *w_gu.shape), "w_down": _cast_bf16(w_down, *w_down.shape),
        "ln2_g": ln2_g, "ln2_b": ln2_b,
    }

    y_p, p_c, p_n, p_m, p_s, p_conv = _layer(x_prompt, [a[:bp] for a in ada6], weights, None, None,
                                             SEQS_PER_STEP_PROMPT)
    m0 = jnp.pad(state_mlstm_m, ((0, 0), (0, GATE_PAD - M_HEADS))).reshape(bs, 1, GATE_PAD)
    y_s, s_c, s_n, s_m, s_s, s_conv = _layer(
        x_sample, [a[bp:] for a in ada6], weights, (state_mlstm_C, state_mlstm_n, m0),
        (state_gdn_S, state_gdn_conv), SEQS_PER_STEP_SAMPLE)
    return (y_p, y_s, p_c, p_n, p_m, p_s, p_conv, s_c, s_n, s_m, s_s, s_conv)
```

```python
import functools

import jax
import jax.numpy as jnp
from jax import lax
from jax.experimental import pallas as pl
from jax.experimental.pallas import tpu as pltpu

F32 = jnp.float32
BF16 = jnp.bfloat16

D_MODEL = 2048
M_HEADS = 4
M_HEAD_DIM = 256
M_WIDTH = M_HEADS * M_HEAD_DIM
G_HEADS = 8
G_HEAD_DIM = 128
G_WIDTH = G_HEADS * G_HEAD_DIM
CONV_W = 4
CHUNK = 64
D_FF = 5632
MAIN_COLS = 4 * M_WIDTH + 3 * G_WIDTH + G_WIDTH
GATE_COLS = 2 * M_HEADS + 2 * G_HEADS
GATE_PAD = 128
DEEPNORM_ALPHA = 2.0 ** 0.25
LN_EPS = 1e-5
RMS_EPS = 1e-6
GI0, GF0, GB0, GA0 = 0, M_HEADS, 2 * M_HEADS, 2 * M_HEADS + G_HEADS

SUBLANES = 8
VMEM_LIMIT_BYTES = 56 * 1024 * 1024
TM_INPROJ, TM_OUTPROJ, TM_FFN = 1024, 512, 512
TN_INPROJ, TN_ADA, TF_FFN = 1024, 1024, 512
OUTPROJ_SLABS = 2
CAST_ROWS, CAST_COLS = 2048, 1024
SEQS_PER_STEP_PROMPT = {"mlstm": 1, "gdn": 2}
SEQS_PER_STEP_SAMPLE = {"mlstm": 2, "gdn": 4}
SMALL_L = 8
PAIR_ROWS = 16


def _cparams(sem, flags=None):
    return pltpu.CompilerParams(dimension_semantics=sem, vmem_limit_bytes=VMEM_LIMIT_BYTES, flags=flags)


def _dot(a, b):
    return jnp.dot(a, b, preferred_element_type=F32)


def _dot_nt(a, b):
    return lax.dot_general(a, b, (((1,), (1,)), ((), ())), preferred_element_type=F32)


def _dot_tn(a, b):
    return lax.dot_general(a, b, (((0,), (0,)), ((), ())), preferred_element_type=F32)


def _split2(x):
    hi = x.astype(BF16)
    return hi, (x - hi.astype(F32)).astype(BF16)


def _split3(x):
    hi = x.astype(BF16)
    r = x - hi.astype(F32)
    mid = r.astype(BF16)
    return hi, mid, (r - mid.astype(F32)).astype(BF16)


def _dot3(a, b):
    return _dot(a[0], b[0]) + (_dot(a[0], b[1]) + _dot(a[1], b[0]))


def _mm_small(a, b):
    out = a[:, 0:1] * b[0:1, :]
    for i in range(1, a.shape[1]):
        out = out + a[:, i:i + 1] * b[i:i + 1, :]
    return out


def _tok_get(ref, b, L, cols):
    if len(ref.shape) == 3:
        return ref[b, :, cols]
    return ref[b * L:(b + 1) * L, cols]


def _tok_set(ref, b, L, cols, val):
    if len(ref.shape) == 3:
        ref[b, :, cols] = val
    else:
        ref[b * L:(b + 1) * L, cols] = val


ALL = slice(None)


def _cast_kernel(x_ref, o_ref):
    o_ref[...] = x_ref[...].astype(BF16)


def _cast_bf16(w, rows, cols):
    bc = min(cols, CAST_COLS)
    br = max(r for r in range(SUBLANES, min(rows, CAST_ROWS) + 1, SUBLANES) if rows % r == 0)
    return pl.pallas_call(
        _cast_kernel,
        grid=(rows // br, cols // bc),
        in_specs=[pl.BlockSpec((br, bc), lambda i, j: (i, j))],
        out_specs=pl.BlockSpec((br, bc), lambda i, j: (i, j)),
        out_shape=jax.ShapeDtypeStruct((rows, cols), BF16),
        compiler_params=_cparams(("arbitrary", "arbitrary")),
        name="cast",
    )(w)


def _ada_kernel(c_ref, w_ref, b_ref, o_ref):
    c = c_ref[...]
    a = (c * jax.nn.sigmoid(c)).astype(BF16)
    o_ref[...] = _dot(a, w_ref[...].astype(BF16)) + b_ref[...]


def _ada(c_all, w_ada, b_ada):
    n_rows, d = c_all.shape
    n_cols = w_ada.shape[1]
    tn = TN_ADA
    return pl.pallas_call(
        _ada_kernel,
        grid=(n_cols // tn,),
        in_specs=[pl.BlockSpec((n_rows, d), lambda j: (0, 0)),
                  pl.BlockSpec((d, tn), lambda j: (0, j)),
                  pl.BlockSpec((1, tn), lambda j: (0, j))],
        out_specs=pl.BlockSpec((n_rows, tn), lambda j: (0, j)),
        out_shape=jax.ShapeDtypeStruct((n_rows, n_cols), F32),
        compiler_params=_cparams(("arbitrary",)),
        name="ada",
    )(c_all, w_ada, b_ada.reshape(1, n_cols))


def _inproj_kernel(x_ref, sc_ref, sh_ref, wt_ref, wgt_ref, o_ref, og_ref, h_scr):
    @pl.when(pl.program_id(1) == 0)
    def _():
        h = (x_ref[...] * (1.0 + sc_ref[0]) + sh_ref[0]).astype(BF16)
        h_scr[...] = h
        og_ref[...] = _dot_nt(h, wgt_ref[...])

    o_ref[...] = _dot_nt(h_scr[...], wt_ref[...])


def _mod_spec(mod, tiles_per_mod, grid_rank):
    _, mod_rows, d = mod.shape
    if grid_rank == 1:
        index_map = lambda i: (i // tiles_per_mod, 0, 0)
    else:
        index_map = lambda i, j: (i // tiles_per_mod, 0, 0)
    if mod_rows == 1:
        return pl.BlockSpec((1, 1, d), index_map)
    return pl.BlockSpec((1, mod_rows, d), index_map, pipeline_mode=pl.Buffered(1))


def _inproj(x2d, sc, sh, w_main, w_gate, tm, tiles_per_mod):
    m, d = x2d.shape
    tn = TN_INPROJ
    mod_spec = _mod_spec(sc, tiles_per_mod, 2)
    return pl.pallas_call(
        _inproj_kernel,
        grid=(m // tm, MAIN_COLS // tn),
        in_specs=[pl.BlockSpec((tm, d), lambda i, j: (i, 0)),
                  mod_spec, mod_spec,
                  pl.BlockSpec((tn, d), lambda i, j: (j, 0)),
                  pl.BlockSpec((GATE_PAD, d), lambda i, j: (0, 0))],
        out_specs=[pl.BlockSpec((tm, tn), lambda i, j: (i, j)),
                   pl.BlockSpec((tm, GATE_PAD), lambda i, j: (i, 0))],
        out_shape=[jax.ShapeDtypeStruct((m, MAIN_COLS), F32),
                   jax.ShapeDtypeStruct((m, GATE_PAD), F32)],
        scratch_shapes=[pltpu.VMEM((tm, d), BF16)],
        compiler_params=_cparams(("arbitrary", "arbitrary")),
        name="inproj",
    )(x2d, sc, sh, w_main, w_gate)


def _gate_tables(g, bias, alog, gid):
    x = g + bias
    is_f = (gid >= GF0) & (gid < GB0)
    is_a = (gid >= GA0) & (gid < GA0 + G_HEADS)
    log_f = jax.nn.log_sigmoid(x)
    log_a = -jnp.exp(alog) * jax.nn.softplus(x)
    inc = jnp.where(is_f, log_f, jnp.where(is_a, log_a, 0.0))
    return x, inc, jax.nn.sigmoid(x)


def _chunk_masks(L):
    row = lax.broadcasted_iota(jnp.int32, (L, L), 0)
    col = lax.broadcasted_iota(jnp.int32, (L, L), 1)
    return row >= col, row > col, row == col


def _gates_col_form(g_col, gb_row_ref, al_row_ref, L):
    causal, _, _ = _chunk_masks(L)
    tril = causal.astype(F32)
    gid_c = lax.broadcasted_iota(jnp.int32, (L, GATE_PAD), 1)
    x_c, inc_c, beta_c = _gate_tables(g_col, gb_row_ref[...], al_row_ref[...], gid_c)
    if L <= SMALL_L:
        cum_c = _mm_small(tril, inc_c)
    else:
        tril_b = tril.astype(BF16)
        c1, c2, c3 = _split3(inc_c)
        cum_c = _dot(tril_b, c1) + (_dot(tril_b, c2) + _dot(tril_b, c3))
    return x_c, cum_c, beta_c


def _gates_both_forms(g_col, g_row, gb_row_ref, gb_col_ref, al_row_ref, al_col_ref, L):
    x_c, cum_c, beta_c = _gates_col_form(g_col, gb_row_ref, al_row_ref, L)
    triu = (lax.broadcasted_iota(jnp.int32, (L, L), 0) <= lax.broadcasted_iota(jnp.int32, (L, L), 1)).astype(F32)
    gid_r = lax.broadcasted_iota(jnp.int32, (GATE_PAD, L), 0)
    x_r, inc_r, _ = _gate_tables(g_row, gb_col_ref[...], al_col_ref[...], gid_r)
    if L <= SMALL_L:
        cum_r = _mm_small(inc_r, triu)
    else:
        triu_b = triu.astype(BF16)
        r1, r2, r3 = _split3(inc_r)
        cum_r = _dot(r1, triu_b) + (_dot(r2, triu_b) + _dot(r3, triu_b))
    return x_c, cum_c, beta_c, x_r, cum_r


def _rms_gate(h, gain, gate):
    return h * lax.rsqrt(jnp.mean(h * h, axis=-1, keepdims=True) + RMS_EPS) * gain * gate


def _mlstm_kernel(*refs, L, nb, nc, has_init, cast_periods):
    (q_ref, k_ref, v_ref, o_ref, gcol_ref, grow_ref, gb_row_ref, gb_col_ref, al_row_ref, al_col_ref,
     ng_ref) = refs[:11]
    n_cast = len(cast_periods)
    n_in = 11 + (3 if has_init else 0)
    if has_init:
        c0_ref, n0_ref, m0_ref = refs[11:14]
    cast_src = refs[n_in:n_in + n_cast]
    h_ref, c_ref, n_ref, m_ref = refs[n_in + n_cast:n_in + n_cast + 4]
    cast_dst = refs[n_in + n_cast + 4:]

    step = pl.program_id(0) * nc + pl.program_id(1)
    for src, dst, period in zip(cast_src, cast_dst, cast_periods):
        if period == 1:
            dst[...] = src[...].astype(BF16)
        else:
            @pl.when(step % period == 0)
            def _(src=src, dst=dst):
                dst[...] = src[...].astype(BF16)

    if has_init and nc == 1:
        c_src, n_src, m_src = c0_ref, n0_ref, m0_ref
    else:
        c_src, n_src, m_src = c_ref, n_ref, m_ref

        @pl.when(pl.program_id(1) == 0)
        def _():
            if has_init:
                c_ref[...] = c0_ref[...]
                n_ref[...] = n0_ref[...]
                m_ref[...] = m0_ref[...]
            else:
                c_ref[...] = jnp.zeros_like(c_ref)
                n_ref[...] = jnp.zeros_like(n_ref)
                m_ref[...] = jnp.zeros_like(m_ref)

    causal, _, _ = _chunk_masks(L)
    lane = lax.broadcasted_iota(jnp.int32, (1, GATE_PAD), 1)
    scale = M_HEAD_DIM ** -0.5
    gates = [_gates_both_forms(_tok_get(gcol_ref, b, L, ALL), grow_ref[b, 0], gb_row_ref, gb_col_ref,
                               al_row_ref, al_col_ref, L) for b in range(nb)]
    m_all = [m_src[b] for b in range(nb)]
    probs = [(b, h) for b in range(nb) for h in range(M_HEADS)]

    def cols(h):
        return slice(h * M_HEAD_DIM, (h + 1) * M_HEAD_DIM)

    q = [_tok_get(q_ref, b, L, cols(h)) for b, h in probs]
    k = [_tok_get(k_ref, b, L, cols(h)) * scale for b, h in probs]
    vb = [_tok_get(v_ref, b, L, cols(h)).astype(BF16) for b, h in probs]
    qb = [x.astype(BF16) for x in q]
    kb = [x.astype(BF16) for x in k]
    ig_c = [gates[b][0][:, GI0 + h:GI0 + h + 1] for b, h in probs]
    ig_r = [gates[b][3][GI0 + h:GI0 + h + 1, :] for b, h in probs]
    bt_c = [gates[b][1][:, GF0 + h:GF0 + h + 1] for b, h in probs]
    bt_r = [gates[b][4][GF0 + h:GF0 + h + 1, :] for b, h in probs]
    m0 = [jnp.sum(jnp.where(lane == h, m_all[b], 0.0), axis=1, keepdims=True) for b, h in probs]
    n_p = len(probs)
    rng = range(n_p)

    log_d = [jnp.where(causal, bt_c[i] - bt_r[i] + ig_r[i], -jnp.inf) for i in rng]
    inter = [bt_c[i] + m0[i] for i in rng]
    m_t = [jnp.maximum(inter[i], jnp.max(log_d[i], axis=1, keepdims=True)) for i in rng]
    inter_w = [jnp.exp(inter[i] - m_t[i]) for i in rng]
    qk = [_dot_nt(qb[i], kb[i]) for i in rng]
    c_old = [c_src[b, h] for b, h in probs]
    n_old = [n_src[b, h:h + 1, :] for b, h in probs]
    qc = [_dot(qb[i], c_old[i].astype(BF16)) for i in rng]
    s = [qk[i] * jnp.exp(log_d[i] - m_t[i]) for i in rng]
    sv = [_dot(s[i].astype(BF16), vb[i]) for i in rng]

    b_last = [bt_c[i][L - 1:L, :] for i in rng]
    m_new = [jnp.maximum(b_last[i] + m0[i], jnp.max(b_last[i] - bt_r[i] + ig_r[i], axis=1, keepdims=True))
             for i in rng]
    kw = [k[i] * jnp.exp(b_last[i] - bt_c[i] + ig_c[i] - m_new[i]) for i in rng]
    decay = [jnp.exp(b_last[i] + m0[i] - m_new[i]) for i in rng]
    kv = [_dot_tn(kw[i].astype(BF16), vb[i]) for i in rng]

    m_next = list(m_all)
    for i, (b, h) in enumerate(probs):
        num = inter_w[i] * qc[i] + sv[i]
        den = inter_w[i] * jnp.sum(q[i] * n_old[i], axis=1, keepdims=True) + jnp.sum(s[i], axis=1, keepdims=True)
        hh = num / jnp.maximum(jnp.abs(den), jnp.exp(-m_t[i]))
        c_ref[b, h] = decay[i] * c_old[i] + kv[i]
        n_ref[b, h:h + 1, :] = decay[i] * n_old[i] + jnp.sum(kw[i], axis=0, keepdims=True)
        m_next[b] = jnp.where(lane == h, m_new[i], m_next[b])
        gate = jax.nn.sigmoid(_tok_get(o_ref, b, L, cols(h)))
        _tok_set(h_ref, b, L, cols(h), _rms_gate(hh, ng_ref[:, cols(h)], gate).astype(h_ref.dtype))
    for b in range(nb):
        m_ref[b] = m_next[b]


def _tok_spec(L, nb, nc, width, col_block):
    if nc == 1:
        return pl.BlockSpec((nb * L, width), lambda b, c: (b, col_block))
    return pl.BlockSpec((nb, L, width), lambda b, c: (b, c, col_block))


def _tok_view(x2d, batch, nc):
    return x2d if nc == 1 else x2d.reshape(batch, x2d.shape[0] // batch, x2d.shape[1])


def _gate_specs(L, nb, nc):
    vec_r = pl.BlockSpec((1, GATE_PAD), lambda b, c: (0, 0))
    vec_c = pl.BlockSpec((GATE_PAD, 1), lambda b, c: (0, 0))
    return [_tok_spec(L, nb, nc, GATE_PAD, 0),
            pl.BlockSpec((nb, 1, GATE_PAD, L), lambda b, c: (b, c, 0, 0)),
            vec_r, vec_c, vec_r, vec_c]


def _scan_out_dtype(L):
    return BF16 if L % 16 == 0 else F32


BF16_ROWS = 16


def _slab_plan(rows, n_steps):
    period = 1
    while (rows * period) % (n_steps * BF16_ROWS):
        period *= 2
        assert period <= n_steps
    return rows * period // n_steps, period


def _mlstm(proj, gates, gates_t, gvecs, norm_g, batch, nc, L, nb, init, side_cast=()):
    has_init = init is not None
    n_steps = (batch // nb) * nc
    in_specs = [_tok_spec(L, nb, nc, M_WIDTH, j) for j in range(4)] + _gate_specs(L, nb, nc)
    in_specs.append(pl.BlockSpec((1, M_WIDTH), lambda b, c: (0, 0)))
    c_spec = pl.BlockSpec((nb, M_HEADS, M_HEAD_DIM, M_HEAD_DIM), lambda b, c: (b, 0, 0, 0))
    n_spec = pl.BlockSpec((nb, M_HEADS, M_HEAD_DIM), lambda b, c: (b, 0, 0))
    m_spec = pl.BlockSpec((nb, 1, GATE_PAD), lambda b, c: (b, 0, 0))
    args = [proj, proj, proj, proj, gates, gates_t, *gvecs, norm_g.reshape(1, M_WIDTH)]
    if has_init:
        in_specs += [c_spec, n_spec, m_spec]
        args += list(init)
    cast_specs, cast_periods = [], []
    for w in side_cast:
        slab, period = _slab_plan(w.shape[0], n_steps)
        cast_specs.append(pl.BlockSpec((slab, w.shape[1]), lambda b, c, period=period: ((b * nc + c) // period, 0)))
        cast_periods.append(period)
    outs = pl.pallas_call(
        functools.partial(_mlstm_kernel, L=L, nb=nb, nc=nc, has_init=has_init, cast_periods=tuple(cast_periods)),
        grid=(batch // nb, nc),
        in_specs=in_specs + cast_specs,
        out_specs=[_tok_spec(L, nb, nc, M_WIDTH, 0), c_spec, n_spec, m_spec] + cast_specs,
        out_shape=[jax.ShapeDtypeStruct(proj.shape[:-1] + (M_WIDTH,), _scan_out_dtype(L)),
                   jax.ShapeDtypeStruct((batch, M_HEADS, M_HEAD_DIM, M_HEAD_DIM), F32),
                   jax.ShapeDtypeStruct((batch, M_HEADS, M_HEAD_DIM), F32),
                   jax.ShapeDtypeStruct((batch, 1, GATE_PAD), F32)]
        + [jax.ShapeDtypeStruct(w.shape, BF16) for w in side_cast],
        compiler_params=_cparams(("arbitrary", "arbitrary")),
        name="mlstm",
    )(*args, *side_cast)
    return outs[:4], outs[4:]


def _inv_unit_lower_small(a_list, L):
    _, _, eye = _chunk_masks(L)
    p = [-a for a in a_list]
    t = [eye.astype(F32) + x for x in p]
    for _ in range(max(1, (L - 1).bit_length()) - 1):
        p = [_mm_small(x, x) for x in p]
        t = [ti + _mm_small(ti, pi) for ti, pi in zip(t, p)]
    return t


def _gate_rows_paired(gp, bias_p, alog_p, L):
    rid = lax.broadcasted_iota(jnp.int32, gp.shape, 0)
    x = gp + bias_p
    is_f = (rid >= GF0 // 2) & (rid < GB0 // 2)
    is_a = (rid >= GA0 // 2) & (rid < (GA0 + G_HEADS) // 2)
    inc = jnp.where(is_f, jax.nn.log_sigmoid(x), jnp.where(is_a, -jnp.exp(alog_p) * jax.nn.softplus(x), 0.0))
    r = lax.broadcasted_iota(jnp.int32, (2 * L, 2 * L), 0)
    c = lax.broadcasted_iota(jnp.int32, (2 * L, 2 * L), 1)
    triu2 = (((r < L) == (c < L)) & (r <= c)).astype(BF16)
    p1, p2, p3 = _split3(inc)
    return _dot(p1, triu2) + (_dot(p2, triu2) + _dot(p3, triu2))


def _gdn_heads_paired(qn, kn, vv, gates_c, cum_p, s_ref, z_ref, h_ref, ng_ref, L, nb):
    dh = G_HEAD_DIM
    row =lax.broadcasted_iota(jnp.int32, (L, 2 * L), 0)
    lane = lax.broadcasted_iota(jnp.int32, (L, 2 * L), 1)
    left = lane < L
    colp = jnp.where(left, lane, lane - L)
    causal_p, strict_p, eye_p = row >= colp, row > colp, (row == colp).astype(F32)
    left2 = lax.broadcasted_iota(jnp.int32, (L, 2 * dh), 1) < dh
    lane4 = lax.broadcasted_iota(jnp.int32, (L, 4 * dh), 1)
    first4 = (lane4 // dh) % 2 == 0

    def bdiag(y, first):
        z = jnp.zeros_like(y)
        return jnp.concatenate([jnp.where(first, y, z), jnp.where(first, z, y)], axis=0)

    pairs = [(b, p) for b in range(nb) for p in range(G_HEADS // 2)]
    rng = range(len(pairs))
    q2 = [qn[b][:, 2 * p * dh:(2 * p + 2) * dh] for b, p in pairs]
    k2 = [kn[b][:, 2 * p * dh:(2 * p + 2) * dh] for b, p in pairs]
    v2 = [vv[b][:, 2 * p * dh:(2 * p + 2) * dh] for b, p in pairs]
    q2b = [x.astype(BF16) for x in q2]
    k2b = [x.astype(BF16) for x in k2]
    kbd = [bdiag(x, left2) for x in k2b]
    b_ca = [gates_c[b][1][:, GA0 + 2 * p:GA0 + 2 * p + 1] for b, p in pairs]
    b_cb = [gates_c[b][1][:, GA0 + 2 * p + 1:GA0 + 2 * p + 2] for b, p in pairs]
    bet_a = [gates_c[b][2][:, GB0 + 2 * p:GB0 + 2 * p + 1] for b, p in pairs]
    bet_b = [gates_c[b][2][:, GB0 + 2 * p + 1:GB0 + 2 * p + 2] for b, p in pairs]
    b_r = [cum_p[b][GA0 // 2 + p:GA0 // 2 + p + 1, :] for b, p in pairs]
    decay = [jnp.exp(jnp.where(causal_p, jnp.where(left, b_ca[i], b_cb[i]) - b_r[i], -jnp.inf)) for i in rng]
    kk = [_dot_nt(k2b[i], kbd[i]) for i in rng]
    qk = [(_dot_nt(q2b[i], kbd[i]) * decay[i]).astype(BF16) for i in rng]
    s_old = [(s_ref[b, 2 * p], s_ref[b, 2 * p + 1]) for b, p in pairs]
    zero = jnp.zeros((dh, dh), BF16)
    sbd = [jnp.concatenate([jnp.concatenate([sa.astype(BF16), zero], axis=1),
                            jnp.concatenate([zero, sb.astype(BF16)], axis=1)], axis=0) for sa, sb in s_old]
    qs = [_dot(q2b[i], sbd[i]) for i in rng]

    a = [jnp.where(strict_p, jnp.where(left, bet_a[i], bet_b[i]) * kk[i] * decay[i], 0.0) for i in rng]
    p = [-x for x in a]
    t = [eye_p + x for x in p]
    ps = [_split2(x) for x in p]
    p = [_dot3(ps[i], (bdiag(ps[i][0], left), bdiag(ps[i][1], left))) for i in rng]
    for lvl in range(1, (L - 1).bit_length()):
        last = lvl == (L - 1).bit_length() - 1
        ps = [_split2(x) for x in p]
        pbd = [(bdiag(hi, left), bdiag(lo, left)) for hi, lo in ps]
        if last:
            t = [t[i] + _dot3(_split2(t[i]), pbd[i]) for i in rng]
        else:
            ts = [_split2(x) for x in t]
            both = [_dot3((jnp.concatenate([ts[i][0], ps[i][0]], axis=0),
                           jnp.concatenate([ts[i][1], ps[i][1]], axis=0)), pbd[i]) for i in rng]
            t = [t[i] + both[i][0:L] for i in rng]
            p = [both[i][L:2 * L] for i in rng]

    bet2 = [jnp.where(left2, bet_a[i], bet_b[i]) for i in rng]
    eb2 = [jnp.where(left2, jnp.exp(b_ca[i]), jnp.exp(b_cb[i])) for i in rng]
    rhs = [_split2(jnp.concatenate([bet2[i] * v2[i], (bet2[i] * eb2[i]) * k2[i]], axis=-1)) for i in rng]
    sol = [_dot3(_split2(t[i]), (bdiag(rhs[i][0], first4), bdiag(rhs[i][1], first4))) for i in rng]
    u = [sol[i][:, 0:2 * dh] - _dot(sol[i][:, 2 * dh:4 * dh].astype(BF16), sbd[i]) for i in rng]
    ub = [x.astype(BF16) for x in u]
    o = [eb2[i] * qs[i] + _dot(qk[i], bdiag(ub[i], left2)) for i in rng]
    bl_a = [b_ca[i][L - 1:L, :] for i in rng]
    bl_b = [b_cb[i][L - 1:L, :] for i in rng]
    wk = [(jnp.where(left2, jnp.exp(bl_a[i] - b_ca[i]), jnp.exp(bl_b[i] - b_cb[i])) * k2[i]).astype(BF16)
          for i in rng]
    ds = [_dot_tn(wk[i], ub[i]) for i in rng]
    for i, (b, p) in enumerate(pairs):
        for j, bl in enumerate((bl_a[i], bl_b[i])):
            h = 2 * p + j
            blk = slice(j * dh, (j + 1) * dh)
            cols = slice(h * dh, (h + 1) * dh)
            s_ref[b, h] = jnp.exp(bl) * s_old[i][j] + ds[i][blk, blk]
            z = _tok_get(z_ref, b, L, cols)
            _tok_set(h_ref, b, L, cols,
                     _rms_gate(o[i][:, blk], ng_ref[:, cols], z * jax.nn.sigmoid(z)).astype(h_ref.dtype))


CONV_HIST = CONV_W - 1
CONV_BASE = SUBLANES - CONV_HIST


def _unit(x):
    return x * lax.rsqrt(jnp.sum(x * x, axis=-1, keepdims=True) + RMS_EPS)


def _gdn_conv_act(b, xq_ref, xk_ref, xv_ref, cw_ref, conv_ref, buf, L, carry):
    buf[b, SUBLANES:SUBLANES + L, 0:G_WIDTH] = _tok_get(xq_ref, b, L, ALL)
    buf[b, SUBLANES:SUBLANES + L, G_WIDTH:2 * G_WIDTH] = _tok_get(xk_ref, b, L, ALL)
    buf[b, SUBLANES:SUBLANES + L, 2 * G_WIDTH:3 * G_WIDTH] = _tok_get(xv_ref, b, L, ALL)
    y = cw_ref[0:1, :] * buf[b, CONV_BASE:CONV_BASE + L, :]
    for j in range(1, CONV_W):
        y = y + cw_ref[j:j + 1, :] * buf[b, CONV_BASE + j:CONV_BASE + j + L, :]
    conv_ref[b] = buf[b, SUBLANES + L - CONV_HIST:SUBLANES + L, :]
    if carry:
        buf[b, 0:SUBLANES, :] = buf[b, L:L + SUBLANES, :]
    return y * jax.nn.sigmoid(y)


def _gdn_paired_kernel(xq_ref, xk_ref, xv_ref, z_ref, gcol_ref, gb_row_ref, al_row_ref, gp_ref, gbp_ref,
                       alp_ref, cw_ref, ng_ref, h_ref, s_ref, conv_ref, buf, *, L, nb):
    @pl.when(pl.program_id(1) == 0)
    def _():
        buf[:, 0:SUBLANES, :] = jnp.zeros((nb, SUBLANES, 3 * G_WIDTH), F32)
        s_ref[...] = jnp.zeros_like(s_ref)

    scale = G_HEAD_DIM ** -0.5
    qn, kn, vv = [], [], []
    for b in range(nb):
        act = _gdn_conv_act(b, xq_ref, xk_ref, xv_ref, cw_ref, conv_ref, buf, L, True)
        qn.append(jnp.concatenate(
            [_unit(act[:, h * G_HEAD_DIM:(h + 1) * G_HEAD_DIM]) * scale for h in range(G_HEADS)], axis=-1))
        kn.append(jnp.concatenate(
            [_unit(act[:, G_WIDTH + h * G_HEAD_DIM:G_WIDTH + (h + 1) * G_HEAD_DIM]) for h in range(G_HEADS)],
            axis=-1))
        vv.append(act[:, 2 * G_WIDTH:3 * G_WIDTH])
    gates_c = [_gates_col_form(gcol_ref[b], gb_row_ref, al_row_ref, L) for b in range(nb)]
    cum_p = [_gate_rows_paired(gp_ref[b, 0], gbp_ref[...], alp_ref[...], L) for b in range(nb)]
    _gdn_heads_paired(qn, kn, vv, gates_c, cum_p, s_ref, z_ref, h_ref, ng_ref, L, nb)


def _gdn_kernel(*refs, L, nb, nc, has_init):
    xq_ref, xk_ref, xv_ref, z_ref, gcol_ref, grow_ref, gb_row_ref, gb_col_ref, al_row_ref, al_col_ref = refs[:10]
    if has_init:
        cw_ref, ng_ref, s0_ref, conv0_ref, h_ref, s_ref, conv_ref, buf = refs[10:]
    else:
        cw_ref, ng_ref, h_ref, s_ref, conv_ref, buf = refs[10:]

    def cols(h):
        return slice(h * G_HEAD_DIM, (h + 1) * G_HEAD_DIM)

    s_src = s0_ref if (has_init and nc == 1) else s_ref

    @pl.when(pl.program_id(1) == 0)
    def _():
        buf[:, 0:SUBLANES, :] = jnp.zeros((nb, SUBLANES, 3 * G_WIDTH), F32)
        if has_init:
            for b in range(nb):
                buf[b, CONV_BASE:SUBLANES, :] = conv0_ref[b]
            if nc > 1:
                s_ref[...] = s0_ref[...]
        else:
            s_ref[...] = jnp.zeros_like(s_ref)

    act = [_gdn_conv_act(b, xq_ref, xk_ref, xv_ref, cw_ref, conv_ref, buf, L, nc > 1) for b in range(nb)]
    causal, strict, _ = _chunk_masks(L)
    scale = G_HEAD_DIM ** -0.5
    gates = [_gates_both_forms(_tok_get(gcol_ref, b, L, ALL), grow_ref[b, 0], gb_row_ref, gb_col_ref,
                               al_row_ref, al_col_ref, L) for b in range(nb)]
    probs = [(b, h) for b in range(nb) for h in range(G_HEADS)]
    rng = range(len(probs))

    q = [_unit(act[b][:, h * G_HEAD_DIM:(h + 1) * G_HEAD_DIM]) * scale for b, h in probs]
    k = [_unit(act[b][:, G_WIDTH + h * G_HEAD_DIM:G_WIDTH + (h + 1) * G_HEAD_DIM]) for b, h in probs]
    v = [act[b][:, 2 * G_WIDTH + h * G_HEAD_DIM:2 * G_WIDTH + (h + 1) * G_HEAD_DIM] for b, h in probs]
    qb = [x.astype(BF16) for x in q]
    kb = [x.astype(BF16) for x in k]
    b_c = [gates[b][1][:, GA0 + h:GA0 + h + 1] for b, h in probs]
    b_r = [gates[b][4][GA0 + h:GA0 + h + 1, :] for b, h in probs]
    bet = [gates[b][2][:, GB0 + h:GB0 + h + 1] for b, h in probs]
    decay = [jnp.exp(jnp.where(causal, b_c[i] - b_r[i], -jnp.inf)) for i in rng]
    eb = [jnp.exp(b_c[i]) for i in rng]
    kk = [_dot_nt(kb[i], kb[i]) for i in rng]
    qk = [_dot_nt(qb[i], kb[i]) * decay[i] for i in rng]
    s_old = [s_src[b, h] for b, h in probs]
    sb = [x.astype(BF16) for x in s_old]
    qs = [_dot(qb[i], sb[i]) for i in rng]

    t = _inv_unit_lower_small([jnp.where(strict, bet[i] * kk[i] * decay[i], 0.0) for i in rng], L)
    rhs = [jnp.concatenate([bet[i] * v[i], (bet[i] * eb[i]) * k[i]], axis=-1) for i in rng]
    sol = [_mm_small(t[i], rhs[i]) for i in rng]
    u = [sol[i][:, 0:G_HEAD_DIM] - _dot(sol[i][:, G_HEAD_DIM:2 * G_HEAD_DIM].astype(BF16), sb[i]) for i in rng]
    ub = [x.astype(BF16) for x in u]
    o = [eb[i] * qs[i] + _dot(qk[i].astype(BF16), ub[i]) for i in rng]
    b_last = [b_c[i][L - 1:L, :] for i in rng]
    wk = [(jnp.exp(b_last[i] - b_c[i]) * k[i]).astype(BF16) for i in rng]
    ds = [_dot_tn(wk[i], ub[i]) for i in rng]
    for i, (b, h) in enumerate(probs):
        s_ref[b, h] = jnp.exp(b_last[i]) * s_old[i] + ds[i]
        z = _tok_get(z_ref, b, L, cols(h))
        _tok_set(h_ref, b, L, cols(h),
                 _rms_gate(o[i], ng_ref[:, cols(h)], z * jax.nn.sigmoid(z)).astype(h_ref.dtype))


def _gdn_paired(proj, gates, gates_t, gvecs, conv_w, norm_g, batch, nc, L, nb):
    assert 2 * L == GATE_PAD and nc > 1
    blk0 = 4 * M_WIDTH // G_WIDTH

    def tok(width, col_block):
        return pl.BlockSpec((nb, L, width), lambda b, c: (b, c, col_block))

    vec_r = pl.BlockSpec((1, GATE_PAD), lambda b, c: (0, 0))
    tab = pl.BlockSpec((PAIR_ROWS, GATE_PAD), lambda b, c: (0, 0))
    gb_row, _, al_row, _ = gvecs
    pair = lambda vec: jnp.repeat(vec.reshape(GATE_PAD)[:2 * PAIR_ROWS], L).reshape(PAIR_ROWS, GATE_PAD)
    in_specs = [tok(G_WIDTH, blk0 + j) for j in range(4)]
    in_specs += [tok(GATE_PAD, 0), vec_r, vec_r,
                 pl.BlockSpec((nb, 1, PAIR_ROWS, GATE_PAD), lambda b, c: (b, c, 0, 0)), tab, tab,
                 pl.BlockSpec((CONV_W, 3 * G_WIDTH), lambda b, c: (0, 0)),
                 pl.BlockSpec((1, G_WIDTH), lambda b, c: (0, 0))]
    args = [proj, proj, proj, proj, gates, gb_row, al_row,
            gates_t.reshape(batch, nc, GATE_PAD // 2, GATE_PAD), pair(gb_row), pair(al_row),
            conv_w, norm_g.reshape(1, G_WIDTH)]
    s_spec = pl.BlockSpec((nb, G_HEADS, G_HEAD_DIM, G_HEAD_DIM), lambda b, c: (b, 0, 0, 0))
    conv_spec = pl.BlockSpec((nb, CONV_HIST, 3 * G_WIDTH), lambda b, c: (b, 0, 0))
    return pl.pallas_call(
        functools.partial(_gdn_paired_kernel, L=L, nb=nb),
        grid=(batch // nb, nc),
        in_specs=in_specs,
        out_specs=[tok(G_WIDTH, 0), s_spec, conv_spec],
        out_shape=[jax.ShapeDtypeStruct(proj.shape[:-1] + (G_WIDTH,), _scan_out_dtype(L)),
                   jax.ShapeDtypeStruct((batch, G_HEADS, G_HEAD_DIM, G_HEAD_DIM), F32),
                   jax.ShapeDtypeStruct((batch, CONV_HIST, 3 * G_WIDTH), F32)],
        scratch_shapes=[pltpu.VMEM((nb, SUBLANES + L, 3 * G_WIDTH), F32)],
        compiler_params=_cparams(("arbitrary", "arbitrary")),
        name="gdn",
    )(*args)


def _gdn(proj, gates, gates_t, gvecs, conv_w, norm_g, batch, nc, L, nb, init):
    if init is None and L > SMALL_L:
        return _gdn_paired(proj, gates, gates_t, gvecs, conv_w, norm_g, batch, nc, L, nb)
    assert L <= SMALL_L
    has_init = init is not None
    blk0 = 4 * M_WIDTH // G_WIDTH
    in_specs = [_tok_spec(L, nb, nc, G_WIDTH, blk0 + j) for j in range(4)] + _gate_specs(L, nb, nc)
    in_specs += [pl.BlockSpec((CONV_W, 3 * G_WIDTH), lambda b, c: (0, 0)),
                 pl.BlockSpec((1, G_WIDTH), lambda b, c: (0, 0))]
    s_spec = pl.BlockSpec((nb, G_HEADS, G_HEAD_DIM, G_HEAD_DIM), lambda b, c: (b, 0, 0, 0))
    conv_spec = pl.BlockSpec((nb, CONV_HIST, 3 * G_WIDTH), lambda b, c: (b, 0, 0))
    args = [proj, proj, proj, proj, gates, gates_t, *gvecs, conv_w, norm_g.reshape(1, G_WIDTH)]
    if has_init:
        in_specs += [s_spec, conv_spec]
        args += list(init)
    return pl.pallas_call(
        functools.partial(_gdn_kernel, L=L, nb=nb, nc=nc, has_init=has_init),
        grid=(batch // nb, nc),
        in_specs=in_specs,
        out_specs=[_tok_spec(L, nb, nc, G_WIDTH, 0), s_spec, conv_spec],
        out_shape=[jax.ShapeDtypeStruct(proj.shape[:-1] + (G_WIDTH,), _scan_out_dtype(L)),
                   jax.ShapeDtypeStruct((batch, G_HEADS, G_HEAD_DIM, G_HEAD_DIM), F32),
                   jax.ShapeDtypeStruct((batch, CONV_HIST, 3 * G_WIDTH), F32)],
        scratch_shapes=[pltpu.VMEM((nb, 2 * SUBLANES, 3 * G_WIDTH), F32)],
        compiler_params=_cparams(("arbitrary", "arbitrary")),
        name="gdn",
    )(*args)


def _layer_norm(y, g, b):
    mu = jnp.mean(y, axis=-1, keepdims=True)
    yc = y - mu
    var = jnp.mean(yc * yc, axis=-1, keepdims=True)
    return yc * lax.rsqrt(var + LN_EPS) * g + b


def _outproj_kernel(hm_ref, hg_ref, x_ref, gt_ref, w_ref, g_ref, b_ref, o_ref):
    tm = x_ref.shape[0]
    half = tm // OUTPROJ_SLABS
    for r in range(OUTPROJ_SLABS):
        rows = slice(r * half, (r + 1) * half)
        mix = (_dot(hm_ref[rows, :].astype(BF16), w_ref[0:M_WIDTH, :])
               + _dot(hg_ref[rows, :].astype(BF16), w_ref[M_WIDTH:M_WIDTH + G_WIDTH, :]))
        gt = gt_ref[0] if gt_ref.shape[1] == 1 else gt_ref[0, rows, :]
        y = DEEPNORM_ALPHA * x_ref[rows, :] + (1.0 + gt) * mix
        o_ref[rows, :] = _layer_norm(y, g_ref[...], b_ref[...])


def _outproj(hm, hg, x2d, gt, w_out, ln_g, ln_b, tm, tiles_per_mod):
    m, d = x2d.shape
    vec = pl.BlockSpec((1, d), lambda i: (0, 0))
    return pl.pallas_call(
        _outproj_kernel,
        grid=(m // tm,),
        in_specs=[pl.BlockSpec((tm, M_WIDTH), lambda i: (i, 0)),
                  pl.BlockSpec((tm, G_WIDTH), lambda i: (i, 0)),
                  pl.BlockSpec((tm, d), lambda i: (i, 0)),
                  _mod_spec(gt, tiles_per_mod, 1),
                  pl.BlockSpec((d, d), lambda i: (0, 0)),
                  vec, vec],
        out_specs=pl.BlockSpec((tm, d), lambda i: (i, 0)),
        out_shape=jax.ShapeDtypeStruct((m, d), F32),
        compiler_params=_cparams(("arbitrary",)),
        name="outproj",
    )(hm, hg, x2d, gt, w_out, ln_g.reshape(1, d), ln_b.reshape(1, d))


def _ffn_kernel(x_ref, sc_ref, sh_ref, gt_ref, wg_ref, wu_ref, wd_ref, g_ref, b_ref, o_ref, h_scr, acc):
    f = pl.program_id(1)

    @pl.when(f == 0)
    def _():
        h_scr[...] = (x_ref[...] * (1.0 + sc_ref[0]) + sh_ref[0]).astype(BF16)
        acc[...] = jnp.zeros_like(acc)

    h = h_scr[...]
    gate = _dot(h, wg_ref[...])
    up = _dot(h, wu_ref[...])
    act = (gate * jax.nn.sigmoid(gate) * up).astype(BF16)
    acc[...] += _dot(act, wd_ref[...])

    @pl.when(f == pl.num_programs(1) - 1)
    def _():
        y = DEEPNORM_ALPHA * x_ref[...] + (1.0 + gt_ref[0]) * acc[...]
        o_ref[...] = _layer_norm(y, g_ref[...], b_ref[...])


def _ffn(x2d, sc, sh, gt, w_gu, w_down, ln_g, ln_b, tm, tiles_per_mod):
    m, d = x2d.shape
    tf = TF_FFN
    nf = D_FF // tf
    mod_spec = _mod_spec(sc, tiles_per_mod, 2)
    vec = pl.BlockSpec((1, d), lambda i, f: (0, 0))
    return pl.pallas_call(
        _ffn_kernel,
        grid=(m // tm, nf),
        in_specs=[pl.BlockSpec((tm, d), lambda i, f: (i, 0)),
                  mod_spec, mod_spec, mod_spec,
                  pl.BlockSpec((d, tf), lambda i, f: (0, f)),
                  pl.BlockSpec((d, tf), lambda i, f: (0, nf + f)),
                  pl.BlockSpec((tf, d), lambda i, f: (f, 0)),
                  vec, vec],
        out_specs=pl.BlockSpec((tm, d), lambda i, f: (i, 0)),
        out_shape=jax.ShapeDtypeStruct((m, d), F32),
        scratch_shapes=[pltpu.VMEM((tm, d), BF16), pltpu.VMEM((tm, d), F32)],
        compiler_params=_cparams(("arbitrary", "arbitrary")),
        name="ffn",
    )(x2d, sc, sh, gt, w_gu, w_gu, w_down, ln_g.reshape(1, d), ln_b.reshape(1, d))


def _layer(x, ada6, weights, init_m, init_g, nb):
    batch, seq, d = x.shape
    L = CHUNK if seq % CHUNK == 0 else seq
    nc = seq // L
    m = batch * seq
    x2d = x.reshape(m, d)

    def tiling(tm_cap):
        tm = min(tm_cap, m)
        if seq % tm == 0:
            return tm, seq // tm, lambda a: a.reshape(batch, 1, d)
        return tm, 1, lambda a: jnp.repeat(a, seq, axis=0).reshape(m // tm, tm, d)

    sh1, sc1, gt1, sh2, sc2, gt2 = ada6
    tm, tpm, mod = tiling(TM_INPROJ)
    proj, gates = _inproj(x2d, mod(sc1), mod(sh1), weights["w_in_main"], weights["w_in_gate"], tm, tpm)
    gates_t = jnp.swapaxes(gates.reshape(batch, nc, L, GATE_PAD), 2, 3)
    proj_v, gates_v = _tok_view(proj, batch, nc), _tok_view(gates, batch, nc)
    gvecs = weights["gvecs"]
    pending = [name for name in ("w_out", "w_gu", "w_down") if weights[name].dtype != BF16]
    (hm, c1, n1, m1), converted = _mlstm(proj_v, gates_v, gates_t, gvecs, weights["m_norm_g"], batch, nc, L,
                                         nb["mlstm"], init_m, [weights[name] for name in pending])
    weights.update(zip(pending, converted))
    hg, s1, conv1 = _gdn(proj_v, gates_v, gates_t, gvecs, weights["conv_w"], weights["g_norm_g"], batch, nc, L,
                         nb["gdn"], init_g)
    tm, tpm, mod = tiling(TM_OUTPROJ)
    x1 = _outproj(hm.reshape(m, M_WIDTH), hg.reshape(m, G_WIDTH), x2d, mod(gt1), weights["w_out"],
                  weights["ln1_g"], weights["ln1_b"], tm, tpm)
    tm, tpm, mod = tiling(TM_FFN)
    y = _ffn(x1, mod(sc2), mod(sh2), mod(gt2), weights["w_gu"], weights["w_down"], weights["ln2_g"],
             weights["ln2_b"], tm, tpm)
    return y.reshape(batch, seq, d), c1, n1, m1[:, 0, :M_HEADS], s1, conv1


def kernel(x_prompt, x_sample, state_mlstm_C, state_mlstm_n, state_mlstm_m, state_gdn_S, state_gdn_conv,
           c_prompt, c_sample, w_ada, b_ada, w_in, m_i_bias, m_f_bias, m_norm_g, conv_w, g_dt_bias,
           g_A_log, g_norm_g, w_out, ln1_g, ln1_b, w_gu, w_down, ln2_g, ln2_b):
    bp, seq_p, d = x_prompt.shape
    bs, seq_s, _ = x_sample.shape

    ada = _ada(jnp.concatenate([c_prompt, c_sample], axis=0), w_ada, b_ada)
    ada6 = [ada[:, i * d:(i + 1) * d] for i in range(6)]

    gate_bias = jnp.zeros((GATE_PAD,), F32)
    gate_bias = gate_bias.at[GI0:GI0 + M_HEADS].set(m_i_bias).at[GF0:GF0 + M_HEADS].set(m_f_bias)
    gate_bias = gate_bias.at[GA0:GA0 + G_HEADS].set(g_dt_bias)
    a_log = jnp.zeros((GATE_PAD,), F32).at[GA0:GA0 + G_HEADS].set(g_A_log)
    weights = {
        "w_in_main": _cast_bf16(w_in.T, MAIN_COLS, d),
        "w_in_gate": jnp.pad(w_in.T[MAIN_COLS:], ((0, GATE_PAD - GATE_COLS), (0, 0))).astype(BF16),
        "gvecs": (gate_bias.reshape(1, GATE_PAD), gate_bias.reshape(GATE_PAD, 1),
                  a_log.reshape(1, GATE_PAD), a_log.reshape(GATE_PAD, 1)),
        "m_norm_g": m_norm_g, "conv_w": conv_w, "g_norm_g": g_norm_g,
        "w_out": w_out, "w_gu": w_gu, "w_down": w_down,
        "ln1_g": ln1_g, "ln1_b": ln1_b, "ln2_g": ln2_g, "ln2_b": ln2_b,
    }

    y_p, p_c, p_n, p_m, p_s, p_conv = _layer(x_prompt, [a[:bp] for a in ada6], weights, None, None,
                                             SEQS_PER_STEP_PROMPT)
    m0 = jnp.pad(state_mlstm_m, ((0, 0), (0, GATE_PAD - M_HEADS))).reshape(bs, 1, GATE_PAD)
    y_s, s_c, s_n, s_m, s_s, s_conv = _layer(
        x_sample, [a[bp:] for a in ada6], weights, (state_mlstm_C, state_mlstm_n, m0),
        (state_gdn_S, state_gdn_conv), SEQS_PER_STEP_SAMPLE)
    return (y_p, y_s, p_c, p_n, p_m, p_s, p_conv, s_c, s_n, s_m, s_s, s_conv)
```

```python
import functools

import jax
import jax.numpy as jnp
from jax import lax
from jax.experimental import pallas as pl
from jax.experimental.pallas import tpu as pltpu

F32 = jnp.float32
BF16 = jnp.bfloat16

D_MODEL = 2048
M_HEADS = 4
M_HEAD_DIM = 256
M_WIDTH = M_HEADS * M_HEAD_DIM
G_HEADS = 8
G_HEAD_DIM = 128
G_WIDTH = G_HEADS * G_HEAD_DIM
CONV_W = 4
CHUNK = 64
D_FF = 5632
MAIN_COLS = 4 * M_WIDTH + 3 * G_WIDTH + G_WIDTH
GATE_COLS = 2 * M_HEADS + 2 * G_HEADS
GATE_PAD = 128
DEEPNORM_ALPHA = 2.0 ** 0.25
LN_EPS = 1e-5
RMS_EPS = 1e-6
GI0, GF0, GB0, GA0 = 0, M_HEADS, 2 * M_HEADS, 2 * M_HEADS + G_HEADS

SUBLANES = 8
VMEM_LIMIT_BYTES = 56 * 1024 * 1024
TM_INPROJ, TM_OUTPROJ, TM_FFN = 1024, 512, 512
TN_INPROJ, TN_ADA, TF_FFN = 1024, 1024, 512
OUTPROJ_SLABS = 2
CAST_ROWS, CAST_COLS = 2048, 1024
SEQS_PER_STEP_PROMPT = {"mlstm": 1, "gdn": 2}
SEQS_PER_STEP_SAMPLE = {"mlstm": 4, "gdn": 4}
SMALL_L = 8
PAIR_ROWS = 16


def _cparams(sem, flags=None):
    return pltpu.CompilerParams(dimension_semantics=sem, vmem_limit_bytes=VMEM_LIMIT_BYTES, flags=flags)


def _dot(a, b):
    return jnp.dot(a, b, preferred_element_type=F32)


def _dot_nt(a, b):
    return lax.dot_general(a, b, (((1,), (1,)), ((), ())), preferred_element_type=F32)


def _dot_tn(a, b):
    return lax.dot_general(a, b, (((0,), (0,)), ((), ())), preferred_element_type=F32)


def _split2(x):
    hi = x.astype(BF16)
    return hi, (x - hi.astype(F32)).astype(BF16)


def _split3(x):
    hi = x.astype(BF16)
    r = x - hi.astype(F32)
    mid = r.astype(BF16)
    return hi, mid, (r - mid.astype(F32)).astype(BF16)


def _dot3(a, b):
    return _dot(a[0], b[0]) + (_dot(a[0], b[1]) + _dot(a[1], b[0]))


def _mm_small(a, b):
    out = a[:, 0:1] * b[0:1, :]
    for i in range(1, a.shape[1]):
        out = out + a[:, i:i + 1] * b[i:i + 1, :]
    return out


def _tok_get(ref, b, L, cols):
    if len(ref.shape) == 3:
        return ref[b, :, cols]
    return ref[b * L:(b + 1) * L, cols]


def _tok_set(ref, b, L, cols, val):
    if len(ref.shape) == 3:
        ref[b, :, cols] = val
    else:
        ref[b * L:(b + 1) * L, cols] = val


ALL = slice(None)


def _cast_kernel(x_ref, o_ref):
    o_ref[...] = x_ref[...].astype(BF16)


def _cast_bf16(w, rows, cols):
    bc = min(cols, CAST_COLS)
    br = max(r for r in range(SUBLANES, min(rows, CAST_ROWS) + 1, SUBLANES) if rows % r == 0)
    return pl.pallas_call(
        _cast_kernel,
        grid=(rows // br, cols // bc),
        in_specs=[pl.BlockSpec((br, bc), lambda i, j: (i, j))],
        out_specs=pl.BlockSpec((br, bc), lambda i, j: (i, j)),
        out_shape=jax.ShapeDtypeStruct((rows, cols), BF16),
        compiler_params=_cparams(("arbitrary", "arbitrary")),
        name="cast",
    )(w)


def _ada_kernel(c_ref, w_ref, b_ref, o_ref):
    c = c_ref[...]
    a = (c * jax.nn.sigmoid(c)).astype(BF16)
    o_ref[...] = _dot(a, w_ref[...].astype(BF16)) + b_ref[...]


def _ada(c_all, w_ada, b_ada):
    n_rows, d = c_all.shape
    n_cols = w_ada.shape[1]
    tn = TN_ADA
    return pl.pallas_call(
        _ada_kernel,
        grid=(n_cols // tn,),
        in_specs=[pl.BlockSpec((n_rows, d), lambda j: (0, 0)),
                  pl.BlockSpec((d, tn), lambda j: (0, j)),
                  pl.BlockSpec((1, tn), lambda j: (0, j))],
        out_specs=pl.BlockSpec((n_rows, tn), lambda j: (0, j)),
        out_shape=jax.ShapeDtypeStruct((n_rows, n_cols), F32),
        compiler_params=_cparams(("arbitrary",)),
        name="ada",
    )(c_all, w_ada, b_ada.reshape(1, n_cols))


def _inproj_kernel(x_ref, sc_ref, sh_ref, wt_ref, wgt_ref, o_ref, og_ref, h_scr):
    @pl.when(pl.program_id(1) == 0)
    def _():
        h = (x_ref[...] * (1.0 + sc_ref[0]) + sh_ref[0]).astype(BF16)
        h_scr[...] = h
        og_ref[...] = _dot_nt(h, wgt_ref[...])

    o_ref[...] = _dot_nt(h_scr[...], wt_ref[...])


def _mod_spec(mod, tiles_per_mod, grid_rank):
    _, mod_rows, d = mod.shape
    if grid_rank == 1:
        index_map = lambda i: (i // tiles_per_mod, 0, 0)
    else:
        index_map = lambda i, j: (i // tiles_per_mod, 0, 0)
    if mod_rows == 1:
        return pl.BlockSpec((1, 1, d), index_map)
    return pl.BlockSpec((1, mod_rows, d), index_map, pipeline_mode=pl.Buffered(1))


def _inproj(x2d, sc, sh, w_main, w_gate, tm, tiles_per_mod):
    m, d = x2d.shape
    tn = TN_INPROJ
    mod_spec = _mod_spec(sc, tiles_per_mod, 2)
    return pl.pallas_call(
        _inproj_kernel,
        grid=(m // tm, MAIN_COLS // tn),
        in_specs=[pl.BlockSpec((tm, d), lambda i, j: (i, 0)),
                  mod_spec, mod_spec,
                  pl.BlockSpec((tn, d), lambda i, j: (j, 0)),
                  pl.BlockSpec((GATE_PAD, d), lambda i, j: (0, 0))],
        out_specs=[pl.BlockSpec((tm, tn), lambda i, j: (i, j)),
                   pl.BlockSpec((tm, GATE_PAD), lambda i, j: (i, 0))],
        out_shape=[jax.ShapeDtypeStruct((m, MAIN_COLS), F32),
                   jax.ShapeDtypeStruct((m, GATE_PAD), F32)],
        scratch_shapes=[pltpu.VMEM((tm, d), BF16)],
        compiler_params=_cparams(("arbitrary", "arbitrary")),
        name="inproj",
    )(x2d, sc, sh, w_main, w_gate)


def _gate_tables(g, bias, alog, gid):
    x = g + bias
    is_f = (gid >= GF0) & (gid < GB0)
    is_a = (gid >= GA0) & (gid < GA0 + G_HEADS)
    log_f = jax.nn.log_sigmoid(x)
    log_a = -jnp.exp(alog) * jax.nn.softplus(x)
    inc = jnp.where(is_f, log_f, jnp.where(is_a, log_a, 0.0))
    return x, inc, jax.nn.sigmoid(x)


def _chunk_masks(L):
    row = lax.broadcasted_iota(jnp.int32, (L, L), 0)
    col = lax.broadcasted_iota(jnp.int32, (L, L), 1)
    return row >= col, row > col, row == col


def _gates_col_form(g_col, gb_row_ref, al_row_ref, L):
    causal, _, _ = _chunk_masks(L)
    tril = causal.astype(F32)
    gid_c = lax.broadcasted_iota(jnp.int32, (L, GATE_PAD), 1)
    x_c, inc_c, beta_c = _gate_tables(g_col, gb_row_ref[...], al_row_ref[...], gid_c)
    if L <= SMALL_L:
        cum_c = _mm_small(tril, inc_c)
    else:
        tril_b = tril.astype(BF16)
        c1, c2, c3 = _split3(inc_c)
        cum_c = _dot(tril_b, c1) + (_dot(tril_b, c2) + _dot(tril_b, c3))
    return x_c, cum_c, beta_c


def _gates_both_forms(g_col, g_row, gb_row_ref, gb_col_ref, al_row_ref, al_col_ref, L):
    x_c, cum_c, beta_c = _gates_col_form(g_col, gb_row_ref, al_row_ref, L)
    triu = (lax.broadcasted_iota(jnp.int32, (L, L), 0) <= lax.broadcasted_iota(jnp.int32, (L, L), 1)).astype(F32)
    gid_r = lax.broadcasted_iota(jnp.int32, (GATE_PAD, L), 0)
    x_r, inc_r, _ = _gate_tables(g_row, gb_col_ref[...], al_col_ref[...], gid_r)
    if L <= SMALL_L:
        cum_r = _mm_small(inc_r, triu)
    else:
        triu_b = triu.astype(BF16)
        r1, r2, r3 = _split3(inc_r)
        cum_r = _dot(r1, triu_b) + (_dot(r2, triu_b) + _dot(r3, triu_b))
    return x_c, cum_c, beta_c, x_r, cum_r


def _rms_gate(h, gain, gate):
    return h * lax.rsqrt(jnp.mean(h * h, axis=-1, keepdims=True) + RMS_EPS) * gain * gate


def _mlstm_kernel(*refs, L, nb, nc, has_init):
    if has_init:
        (q_ref, k_ref, v_ref, o_ref, gcol_ref, grow_ref, gb_row_ref, gb_col_ref, al_row_ref, al_col_ref,
         ng_ref, c0_ref, n0_ref, m0_ref, h_ref, c_ref, n_ref, m_ref) = refs
    else:
        (q_ref, k_ref, v_ref, o_ref, gcol_ref, grow_ref, gb_row_ref, gb_col_ref, al_row_ref, al_col_ref,
         ng_ref, h_ref, c_ref, n_ref, m_ref) = refs

    if has_init and nc == 1:
        c_src, n_src, m_src = c0_ref, n0_ref, m0_ref
    else:
        c_src, n_src, m_src = c_ref, n_ref, m_ref

        @pl.when(pl.program_id(1) == 0)
        def _():
            if has_init:
                c_ref[...] = c0_ref[...]
                n_ref[...] = n0_ref[...]
                m_ref[...] = m0_ref[...]
            else:
                c_ref[...] = jnp.zeros_like(c_ref)
                n_ref[...] = jnp.zeros_like(n_ref)
                m_ref[...] = jnp.zeros_like(m_ref)

    causal, _, _ = _chunk_masks(L)
    lane = lax.broadcasted_iota(jnp.int32, (1, GATE_PAD), 1)
    scale = M_HEAD_DIM ** -0.5
    gates = [_gates_both_forms(_tok_get(gcol_ref, b, L, ALL), grow_ref[b, 0], gb_row_ref, gb_col_ref,
                               al_row_ref, al_col_ref, L) for b in range(nb)]
    m_all = [m_src[b] for b in range(nb)]
    probs = [(b, h) for b in range(nb) for h in range(M_HEADS)]

    def cols(h):
        return slice(h * M_HEAD_DIM, (h + 1) * M_HEAD_DIM)

    q = [_tok_get(q_ref, b, L, cols(h)) for b, h in probs]
    k = [_tok_get(k_ref, b, L, cols(h)) * scale for b, h in probs]
    vb = [_tok_get(v_ref, b, L, cols(h)).astype(BF16) for b, h in probs]
    qb = [x.astype(BF16) for x in q]
    kb = [x.astype(BF16) for x in k]
    ig_c = [gates[b][0][:, GI0 + h:GI0 + h + 1] for b, h in probs]
    ig_r = [gates[b][3][GI0 + h:GI0 + h + 1, :] for b, h in probs]
    bt_c = [gates[b][1][:, GF0 + h:GF0 + h + 1] for b, h in probs]
    bt_r = [gates[b][4][GF0 + h:GF0 + h + 1, :] for b, h in probs]
    m0 = [jnp.sum(jnp.where(lane == h, m_all[b], 0.0), axis=1, keepdims=True) for b, h in probs]
    n_p = len(probs)
    rng = range(n_p)

    log_d = [jnp.where(causal, bt_c[i] - bt_r[i] + ig_r[i], -jnp.inf) for i in rng]
    inter = [bt_c[i] + m0[i] for i in rng]
    m_t = [jnp.maximum(inter[i], jnp.max(log_d[i], axis=1, keepdims=True)) for i in rng]
    inter_w = [jnp.exp(inter[i] - m_t[i]) for i in rng]
    qk = [_dot_nt(qb[i], kb[i]) for i in rng]
    c_old = [c_src[b, h] for b, h in probs]
    n_old = [n_src[b, h:h + 1, :] for b, h in probs]
    qc = [_dot(qb[i], c_old[i].astype(BF16)) for i in rng]
    s = [qk[i] * jnp.exp(log_d[i] - m_t[i]) for i in rng]
    sv = [_dot(s[i].astype(BF16), vb[i]) for i in rng]

    b_last = [bt_c[i][L - 1:L, :] for i in rng]
    m_new = [jnp.maximum(b_last[i] + m0[i], jnp.max(b_last[i] - bt_r[i] + ig_r[i], axis=1, keepdims=True))
             for i in rng]
    kw = [k[i] * jnp.exp(b_last[i] - bt_c[i] + ig_c[i] - m_new[i]) for i in rng]
    decay = [jnp.exp(b_last[i] + m0[i] - m_new[i]) for i in rng]
    kv = [_dot_tn(kw[i].astype(BF16), vb[i]) for i in rng]

    m_next = list(m_all)
    for i, (b, h) in enumerate(probs):
        num = inter_w[i] * qc[i] + sv[i]
        den = inter_w[i] * jnp.sum(q[i] * n_old[i], axis=1, keepdims=True) + jnp.sum(s[i], axis=1, keepdims=True)
        hh = num / jnp.maximum(jnp.abs(den), jnp.exp(-m_t[i]))
        c_ref[b, h] = decay[i] * c_old[i] + kv[i]
        n_ref[b, h:h + 1, :] = decay[i] * n_old[i] + jnp.sum(kw[i], axis=0, keepdims=True)
        m_next[b] = jnp.where(lane == h, m_new[i], m_next[b])
        gate = jax.nn.sigmoid(_tok_get(o_ref, b, L, cols(h)))
        _tok_set(h_ref, b, L, cols(h), _rms_gate(hh, ng_ref[:, cols(h)], gate).astype(h_ref.dtype))
    for b in range(nb):
        m_ref[b] = m_next[b]


def _tok_spec(L, nb, nc, width, col_block):
    if nc == 1:
        return pl.BlockSpec((nb * L, width), lambda b, c: (b, col_block))
    return pl.BlockSpec((nb, L, width), lambda b, c: (b, c, col_block))


def _tok_view(x2d, batch, nc):
    return x2d if nc == 1 else x2d.reshape(batch, x2d.shape[0] // batch, x2d.shape[1])


def _gate_specs(L, nb, nc):
    vec_r = pl.BlockSpec((1, GATE_PAD), lambda b, c: (0, 0))
    vec_c = pl.BlockSpec((GATE_PAD, 1), lambda b, c: (0, 0))
    return [_tok_spec(L, nb, nc, GATE_PAD, 0),
            pl.BlockSpec((nb, 1, GATE_PAD, L), lambda b, c: (b, c, 0, 0)),
            vec_r, vec_c, vec_r, vec_c]


def _scan_out_dtype(L):
    return BF16 if L % 16 == 0 else F32


BF16_ROWS = 16


def _slab_plan(rows, n_steps):
    period = 1
    while (rows * period) % (n_steps * BF16_ROWS):
        period *= 2
        assert period <= n_steps
    return rows * period // n_steps, period


def _side_cast_specs(side_cast, n_steps, nc):
    specs, periods = [], []
    for w in side_cast:
        slab, period = _slab_plan(w.shape[0], n_steps)
        specs.append(pl.BlockSpec((slab, w.shape[1]), lambda b, c, period=period: ((b * nc + c) // period, 0)))
        periods.append(period)
    return specs, tuple(periods)


def _run_side_casts(cast_src, cast_dst, periods, nc):
    step = pl.program_id(0) * nc + pl.program_id(1)
    for src, dst, period in zip(cast_src, cast_dst, periods):
        if period == 1:
            dst[...] = src[...].astype(BF16)
        else:
            @pl.when(step % period == 0)
            def _(src=src, dst=dst):
                dst[...] = src[...].astype(BF16)


def _mlstm(proj, gates, gates_t, gvecs, norm_g, batch, nc, L, nb, init):
    has_init = init is not None
    in_specs = [_tok_spec(L, nb, nc, M_WIDTH, j) for j in range(4)] + _gate_specs(L, nb, nc)
    in_specs.append(pl.BlockSpec((1, M_WIDTH), lambda b, c: (0, 0)))
    c_spec = pl.BlockSpec((nb, M_HEADS, M_HEAD_DIM, M_HEAD_DIM), lambda b, c: (b, 0, 0, 0))
    n_spec = pl.BlockSpec((nb, M_HEADS, M_HEAD_DIM), lambda b, c: (b, 0, 0))
    m_spec = pl.BlockSpec((nb, 1, GATE_PAD), lambda b, c: (b, 0, 0))
    args = [proj, proj, proj, proj, gates, gates_t, *gvecs, norm_g.reshape(1, M_WIDTH)]
    if has_init:
        in_specs += [c_spec, n_spec, m_spec]
        args += list(init)
    return pl.pallas_call(
        functools.partial(_mlstm_kernel, L=L, nb=nb, nc=nc, has_init=has_init),
        grid=(batch // nb, nc),
        in_specs=in_specs,
        out_specs=[_tok_spec(L, nb, nc, M_WIDTH, 0), c_spec, n_spec, m_spec],
        out_shape=[jax.ShapeDtypeStruct(proj.shape[:-1] + (M_WIDTH,), _scan_out_dtype(L)),
                   jax.ShapeDtypeStruct((batch, M_HEADS, M_HEAD_DIM, M_HEAD_DIM), F32),
                   jax.ShapeDtypeStruct((batch, M_HEADS, M_HEAD_DIM), F32),
                   jax.ShapeDtypeStruct((batch, 1, GATE_PAD), F32)],
        compiler_params=_cparams(("arbitrary", "arbitrary")),
        name="mlstm",
    )(*args)


def _inv_unit_lower_small(a_list, L):
    _, _, eye = _chunk_masks(L)
    p = [-a for a in a_list]
    t = [eye.astype(F32) + x for x in p]
    for _ in range(max(1, (L - 1).bit_length()) - 1):
        p = [_mm_small(x, x) for x in p]
        t = [ti + _mm_small(ti, pi) for ti, pi in zip(t, p)]
    return t


def _gate_rows_paired(gp, bias_p, alog_p, L):
    rid = lax.broadcasted_iota(jnp.int32, gp.shape, 0)
    x = gp + bias_p
    is_f = (rid >= GF0 // 2) & (rid < GB0 // 2)
    is_a = (rid >= GA0 // 2) & (rid < (GA0 + G_HEADS) // 2)
    inc = jnp.where(is_f, jax.nn.log_sigmoid(x), jnp.where(is_a, -jnp.exp(alog_p) * jax.nn.softplus(x), 0.0))
    r = lax.broadcasted_iota(jnp.int32, (2 * L, 2 * L), 0)
    c = lax.broadcasted_iota(jnp.int32, (2 * L, 2 * L), 1)
    triu2 = (((r < L) == (c < L)) & (r <= c)).astype(BF16)
    p1, p2, p3 = _split3(inc)
    return _dot(p1, triu2) + (_dot(p2, triu2) + _dot(p3, triu2))


def _gdn_heads_paired(qn, kn, vv, gates_c, cum_p, s_ref, z_ref, h_ref, ng_ref, L, nb):
    dh = G_HEAD_DIM
    row =lax.broadcasted_iota(jnp.int32, (L, 2 * L), 0)
    lane = lax.broadcasted_iota(jnp.int32, (L, 2 * L), 1)
    left = lane < L
    colp = jnp.where(left, lane, lane - L)
    causal_p, strict_p, eye_p = row >= colp, row > colp, (row == colp).astype(F32)
    left2 = lax.broadcasted_iota(jnp.int32, (L, 2 * dh), 1) < dh
    lane4 = lax.broadcasted_iota(jnp.int32, (L, 4 * dh), 1)
    first4 = (lane4 // dh) % 2 == 0

    def bdiag(y, first):
        z = jnp.zeros_like(y)
        return jnp.concatenate([jnp.where(first, y, z), jnp.where(first, z, y)], axis=0)

    pairs = [(b, p) for b in range(nb) for p in range(G_HEADS // 2)]
    rng = range(len(pairs))
    q2 = [qn[b][:, 2 * p * dh:(2 * p + 2) * dh] for b, p in pairs]
    k2 = [kn[b][:, 2 * p * dh:(2 * p + 2) * dh] for b, p in pairs]
    v2 = [vv[b][:, 2 * p * dh:(2 * p + 2) * dh] for b, p in pairs]
    q2b = [x.astype(BF16) for x in q2]
    k2b = [x.astype(BF16) for x in k2]
    kbd = [bdiag(x, left2) for x in k2b]
    b_ca = [gates_c[b][1][:, GA0 + 2 * p:GA0 + 2 * p + 1] for b, p in pairs]
    b_cb = [gates_c[b][1][:, GA0 + 2 * p + 1:GA0 + 2 * p + 2] for b, p in pairs]
    bet_a = [gates_c[b][2][:, GB0 + 2 * p:GB0 + 2 * p + 1] for b, p in pairs]
    bet_b = [gates_c[b][2][:, GB0 + 2 * p + 1:GB0 + 2 * p + 2] for b, p in pairs]
    b_r = [cum_p[b][GA0 // 2 + p:GA0 // 2 + p + 1, :] for b, p in pairs]
    decay = [jnp.exp(jnp.where(causal_p, jnp.where(left, b_ca[i], b_cb[i]) - b_r[i], -jnp.inf)) for i in rng]
    kk = [_dot_nt(k2b[i], kbd[i]) for i in rng]
    qk = [(_dot_nt(q2b[i], kbd[i]) * decay[i]).astype(BF16) for i in rng]
    s_old = [(s_ref[b, 2 * p], s_ref[b, 2 * p + 1]) for b, p in pairs]
    zero = jnp.zeros((dh, dh), BF16)
    sbd = [jnp.concatenate([jnp.concatenate([sa.astype(BF16), zero], axis=1),
                            jnp.concatenate([zero, sb.astype(BF16)], axis=1)], axis=0) for sa, sb in s_old]
    qs = [_dot(q2b[i], sbd[i]) for i in rng]

    a = [jnp.where(strict_p, jnp.where(left, bet_a[i], bet_b[i]) * kk[i] * decay[i], 0.0) for i in rng]
    p = [-x for x in a]
    t = [eye_p + x for x in p]
    ps = [_split2(x) for x in p]
    p = [_dot3(ps[i], (bdiag(ps[i][0], left), bdiag(ps[i][1], left))) for i in rng]
    for lvl in range(1, (L - 1).bit_length()):
        last = lvl == (L - 1).bit_length() - 1
        ps = [_split2(x) for x in p]
        pbd = [(bdiag(hi, left), bdiag(lo, left)) for hi, lo in ps]
        if last:
            t = [t[i] + _dot3(_split2(t[i]), pbd[i]) for i in rng]
        else:
            ts = [_split2(x) for x in t]
            both = [_dot3((jnp.concatenate([ts[i][0], ps[i][0]], axis=0),
                           jnp.concatenate([ts[i][1], ps[i][1]], axis=0)), pbd[i]) for i in rng]
            t = [t[i] + both[i][0:L] for i in rng]
            p = [both[i][L:2 * L] for i in rng]

    bet2 = [jnp.where(left2, bet_a[i], bet_b[i]) for i in rng]
    eb2 = [jnp.where(left2, jnp.exp(b_ca[i]), jnp.exp(b_cb[i])) for i in rng]
    rhs = [_split2(jnp.concatenate([bet2[i] * v2[i], (bet2[i] * eb2[i]) * k2[i]], axis=-1)) for i in rng]
    sol = [_dot3(_split2(t[i]), (bdiag(rhs[i][0], first4), bdiag(rhs[i][1], first4))) for i in rng]
    u = [sol[i][:, 0:2 * dh] - _dot(sol[i][:, 2 * dh:4 * dh].astype(BF16), sbd[i]) for i in rng]
    ub = [x.astype(BF16) for x in u]
    o = [eb2[i] * qs[i] + _dot(qk[i], bdiag(ub[i], left2)) for i in rng]
    bl_a = [b_ca[i][L - 1:L, :] for i in rng]
    bl_b = [b_cb[i][L - 1:L, :] for i in rng]
    wk = [(jnp.where(left2, jnp.exp(bl_a[i] - b_ca[i]), jnp.exp(bl_b[i] - b_cb[i])) * k2[i]).astype(BF16)
          for i in rng]
    ds = [_dot_tn(wk[i], ub[i]) for i in rng]
    for i, (b, p) in enumerate(pairs):
        for j, bl in enumerate((bl_a[i], bl_b[i])):
            h = 2 * p + j
            blk = slice(j * dh, (j + 1) * dh)
            cols = slice(h * dh, (h + 1) * dh)
            s_ref[b, h] = jnp.exp(bl) * s_old[i][j] + ds[i][blk, blk]
            z = _tok_get(z_ref, b, L, cols)
            _tok_set(h_ref, b, L, cols,
                     _rms_gate(o[i][:, blk], ng_ref[:, cols], z * jax.nn.sigmoid(z)).astype(h_ref.dtype))


CONV_HIST = CONV_W - 1
CONV_BASE = SUBLANES - CONV_HIST


def _unit(x):
    return x * lax.rsqrt(jnp.sum(x * x, axis=-1, keepdims=True) + RMS_EPS)


def _gdn_conv_act(b, xq_ref, xk_ref, xv_ref, cw_ref, conv_ref, buf, L, carry):
    buf[b, SUBLANES:SUBLANES + L, 0:G_WIDTH] = _tok_get(xq_ref, b, L, ALL)
    buf[b, SUBLANES:SUBLANES + L, G_WIDTH:2 * G_WIDTH] = _tok_get(xk_ref, b, L, ALL)
    buf[b, SUBLANES:SUBLANES + L, 2 * G_WIDTH:3 * G_WIDTH] = _tok_get(xv_ref, b, L, ALL)
    y = cw_ref[0:1, :] * buf[b, CONV_BASE:CONV_BASE + L, :]
    for j in range(1, CONV_W):
        y = y + cw_ref[j:j + 1, :] * buf[b, CONV_BASE + j:CONV_BASE + j + L, :]
    conv_ref[b] = buf[b, SUBLANES + L - CONV_HIST:SUBLANES + L, :]
    if carry:
        buf[b, 0:SUBLANES, :] = buf[b, L:L + SUBLANES, :]
    return y * jax.nn.sigmoid(y)


def _gdn_paired_kernel(*refs, L, nb, nc, cast_periods):
    (xq_ref, xk_ref, xv_ref, z_ref, gcol_ref, gb_row_ref, al_row_ref, gp_ref, gbp_ref, alp_ref, cw_ref,
     ng_ref) = refs[:12]
    n_cast = len(cast_periods)
    cast_src = refs[12:12 + n_cast]
    h_ref, s_ref, conv_ref = refs[12 + n_cast:15 + n_cast]
    cast_dst = refs[15 + n_cast:15 + 2 * n_cast]
    buf = refs[15 + 2 * n_cast]
    _run_side_casts(cast_src, cast_dst, cast_periods, nc)

    @pl.when(pl.program_id(1) == 0)
    def _():
        buf[:, 0:SUBLANES, :] = jnp.zeros((nb, SUBLANES, 3 * G_WIDTH), F32)
        s_ref[...] = jnp.zeros_like(s_ref)

    scale = G_HEAD_DIM ** -0.5
    qn, kn, vv = [], [], []
    for b in range(nb):
        act = _gdn_conv_act(b, xq_ref, xk_ref, xv_ref, cw_ref, conv_ref, buf, L, True)
        qn.append(jnp.concatenate(
            [_unit(act[:, h * G_HEAD_DIM:(h + 1) * G_HEAD_DIM]) * scale for h in range(G_HEADS)], axis=-1))
        kn.append(jnp.concatenate(
            [_unit(act[:, G_WIDTH + h * G_HEAD_DIM:G_WIDTH + (h + 1) * G_HEAD_DIM]) for h in range(G_HEADS)],
            axis=-1))
        vv.append(act[:, 2 * G_WIDTH:3 * G_WIDTH])
    gates_c = [_gates_col_form(gcol_ref[b], gb_row_ref, al_row_ref, L) for b in range(nb)]
    cum_p = [_gate_rows_paired(gp_ref[b, 0], gbp_ref[...], alp_ref[...], L) for b in range(nb)]
    _gdn_heads_paired(qn, kn, vv, gates_c, cum_p, s_ref, z_ref, h_ref, ng_ref, L, nb)


def _gdn_kernel(*refs, L, nb, nc, has_init):
    xq_ref, xk_ref, xv_ref, z_ref, gcol_ref, grow_ref, gb_row_ref, gb_col_ref, al_row_ref, al_col_ref = refs[:10]
    if has_init:
        cw_ref, ng_ref, s0_ref, conv0_ref, h_ref, s_ref, conv_ref, buf = refs[10:]
    else:
        cw_ref, ng_ref, h_ref, s_ref, conv_ref, buf = refs[10:]

    def cols(h):
        return slice(h * G_HEAD_DIM, (h + 1) * G_HEAD_DIM)

    s_src = s0_ref if (has_init and nc == 1) else s_ref

    @pl.when(pl.program_id(1) == 0)
    def _():
        buf[:, 0:SUBLANES, :] = jnp.zeros((nb, SUBLANES, 3 * G_WIDTH), F32)
        if has_init:
            for b in range(nb):
                buf[b, CONV_BASE:SUBLANES, :] = conv0_ref[b]
            if nc > 1:
                s_ref[...] = s0_ref[...]
        else:
            s_ref[...] = jnp.zeros_like(s_ref)

    act = [_gdn_conv_act(b, xq_ref, xk_ref, xv_ref, cw_ref, conv_ref, buf, L, nc > 1) for b in range(nb)]
    causal, strict, _ = _chunk_masks(L)
    scale = G_HEAD_DIM ** -0.5
    gates = [_gates_both_forms(_tok_get(gcol_ref, b, L, ALL), grow_ref[b, 0], gb_row_ref, gb_col_ref,
                               al_row_ref, al_col_ref, L) for b in range(nb)]
    probs = [(b, h) for b in range(nb) for h in range(G_HEADS)]
    rng = range(len(probs))

    q = [_unit(act[b][:, h * G_HEAD_DIM:(h + 1) * G_HEAD_DIM]) * scale for b, h in probs]
    k = [_unit(act[b][:, G_WIDTH + h * G_HEAD_DIM:G_WIDTH + (h + 1) * G_HEAD_DIM]) for b, h in probs]
    v = [act[b][:, 2 * G_WIDTH + h * G_HEAD_DIM:2 * G_WIDTH + (h + 1) * G_HEAD_DIM] for b, h in probs]
    qb = [x.astype(BF16) for x in q]
    kb = [x.astype(BF16) for x in k]
    b_c = [gates[b][1][:, GA0 + h:GA0 + h + 1] for b, h in probs]
    b_r = [gates[b][4][GA0 + h:GA0 + h + 1, :] for b, h in probs]
    bet = [gates[b][2][:, GB0 + h:GB0 + h + 1] for b, h in probs]
    decay = [jnp.exp(jnp.where(causal, b_c[i] - b_r[i], -jnp.inf)) for i in rng]
    eb = [jnp.exp(b_c[i]) for i in rng]
    kk = [_dot_nt(kb[i], kb[i]) for i in rng]
    qk = [_dot_nt(qb[i], kb[i]) * decay[i] for i in rng]
    s_old = [s_src[b, h] for b, h in probs]
    sb = [x.astype(BF16) for x in s_old]
    qs = [_dot(qb[i], sb[i]) for i in rng]

    t = _inv_unit_lower_small([jnp.where(strict, bet[i] * kk[i] * decay[i], 0.0) for i in rng], L)
    rhs = [jnp.concatenate([bet[i] * v[i], (bet[i] * eb[i]) * k[i]], axis=-1) for i in rng]
    sol = [_mm_small(t[i], rhs[i]) for i in rng]
    u = [sol[i][:, 0:G_HEAD_DIM] - _dot(sol[i][:, G_HEAD_DIM:2 * G_HEAD_DIM].astype(BF16), sb[i]) for i in rng]
    ub = [x.astype(BF16) for x in u]
    o = [eb[i] * qs[i] + _dot(qk[i].astype(BF16), ub[i]) for i in rng]
    b_last = [b_c[i][L - 1:L, :] for i in rng]
    wk = [(jnp.exp(b_last[i] - b_c[i]) * k[i]).astype(BF16) for i in rng]
    ds = [_dot_tn(wk[i], ub[i]) for i in rng]
    for i, (b, h) in enumerate(probs):
        s_ref[b, h] = jnp.exp(b_last[i]) * s_old[i] + ds[i]
        z = _tok_get(z_ref, b, L, cols(h))
        _tok_set(h_ref, b, L, cols(h),
                 _rms_gate(o[i], ng_ref[:, cols(h)], z * jax.nn.sigmoid(z)).astype(h_ref.dtype))


def _gdn_paired(proj, gates, gates_t, gvecs, conv_w, norm_g, batch, nc, L, nb, side_cast):
    assert 2 * L == GATE_PAD and nc > 1
    cast_specs, cast_periods = _side_cast_specs(side_cast, (batch // nb) * nc, nc)
    blk0 = 4 * M_WIDTH // G_WIDTH

    def tok(width, col_block):
        return pl.BlockSpec((nb, L, width), lambda b, c: (b, c, col_block))

    vec_r = pl.BlockSpec((1, GATE_PAD), lambda b, c: (0, 0))
    tab = pl.BlockSpec((PAIR_ROWS, GATE_PAD), lambda b, c: (0, 0))
    gb_row, _, al_row, _ = gvecs
    pair = lambda vec: jnp.repeat(vec.reshape(GATE_PAD)[:2 * PAIR_ROWS], L).reshape(PAIR_ROWS, GATE_PAD)
    in_specs = [tok(G_WIDTH, blk0 + j) for j in range(4)]
    in_specs += [tok(GATE_PAD, 0), vec_r, vec_r,
                 pl.BlockSpec((nb, 1, PAIR_ROWS, GATE_PAD), lambda b, c: (b, c, 0, 0)), tab, tab,
                 pl.BlockSpec((CONV_W, 3 * G_WIDTH), lambda b, c: (0, 0)),
                 pl.BlockSpec((1, G_WIDTH), lambda b, c: (0, 0))]
    args = [proj, proj, proj, proj, gates, gb_row, al_row,
            gates_t.reshape(batch, nc, GATE_PAD // 2, GATE_PAD), pair(gb_row), pair(al_row),
            conv_w, norm_g.reshape(1, G_WIDTH)]
    s_spec = pl.BlockSpec((nb, G_HEADS, G_HEAD_DIM, G_HEAD_DIM), lambda b, c: (b, 0, 0, 0))
    conv_spec = pl.BlockSpec((nb, CONV_HIST, 3 * G_WIDTH), lambda b, c: (b, 0, 0))
    outs = pl.pallas_call(
        functools.partial(_gdn_paired_kernel, L=L, nb=nb, nc=nc, cast_periods=cast_periods),
        grid=(batch // nb, nc),
        in_specs=in_specs + cast_specs,
        out_specs=[tok(G_WIDTH, 0), s_spec, conv_spec] + cast_specs,
        out_shape=[jax.ShapeDtypeStruct(proj.shape[:-1] + (G_WIDTH,), _scan_out_dtype(L)),
                   jax.ShapeDtypeStruct((batch, G_HEADS, G_HEAD_DIM, G_HEAD_DIM), F32),
                   jax.ShapeDtypeStruct((batch, CONV_HIST, 3 * G_WIDTH), F32)]
        + [jax.ShapeDtypeStruct(w.shape, BF16) for w in side_cast],
        scratch_shapes=[pltpu.VMEM((nb, SUBLANES + L, 3 * G_WIDTH), F32)],
        compiler_params=_cparams(("arbitrary", "arbitrary")),
        name="gdn",
    )(*args, *side_cast)
    return outs[:3], outs[3:]


def _gdn(proj, gates, gates_t, gvecs, conv_w, norm_g, batch, nc, L, nb, init, side_cast=()):
    if init is None and L > SMALL_L:
        return _gdn_paired(proj, gates, gates_t, gvecs, conv_w, norm_g, batch, nc, L, nb, side_cast)
    assert L <= SMALL_L and not side_cast
    has_init = init is not None
    blk0 = 4 * M_WIDTH // G_WIDTH
    in_specs = [_tok_spec(L, nb, nc, G_WIDTH, blk0 + j) for j in range(4)] + _gate_specs(L, nb, nc)
    in_specs += [pl.BlockSpec((CONV_W, 3 * G_WIDTH), lambda b, c: (0, 0)),
                 pl.BlockSpec((1, G_WIDTH), lambda b, c: (0, 0))]
    s_spec = pl.BlockSpec((nb, G_HEADS, G_HEAD_DIM, G_HEAD_DIM), lambda b, c: (b, 0, 0, 0))
    conv_spec = pl.BlockSpec((nb, CONV_HIST, 3 * G_WIDTH), lambda b, c: (b, 0, 0))
    args = [proj, proj, proj, proj, gates, gates_t, *gvecs, conv_w, norm_g.reshape(1, G_WIDTH)]
    if has_init:
        in_specs += [s_spec, conv_spec]
        args += list(init)
    outs = pl.pallas_call(
        functools.partial(_gdn_kernel, L=L, nb=nb, nc=nc, has_init=has_init),
        grid=(batch // nb, nc),
        in_specs=in_specs,
        out_specs=[_tok_spec(L, nb, nc, G_WIDTH, 0), s_spec, conv_spec],
        out_shape=[jax.ShapeDtypeStruct(proj.shape[:-1] + (G_WIDTH,), _scan_out_dtype(L)),
                   jax.ShapeDtypeStruct((batch, G_HEADS, G_HEAD_DIM, G_HEAD_DIM), F32),
                   jax.ShapeDtypeStruct((batch, CONV_HIST, 3 * G_WIDTH), F32)],
        scratch_shapes=[pltpu.VMEM((nb, 2 * SUBLANES, 3 * G_WIDTH), F32)],
        compiler_params=_cparams(("arbitrary", "arbitrary")),
        name="gdn",
    )(*args)
    return outs, []


def _layer_norm(y, g, b):
    mu = jnp.mean(y, axis=-1, keepdims=True)
    yc = y - mu
    var = jnp.mean(yc * yc, axis=-1, keepdims=True)
    return yc * lax.rsqrt(var + LN_EPS) * g + b


def _outproj_kernel(hm_ref, hg_ref, x_ref, gt_ref, w_ref, g_ref, b_ref, o_ref):
    tm = x_ref.shape[0]
    half = tm // OUTPROJ_SLABS
    for r in range(OUTPROJ_SLABS):
        rows = slice(r * half, (r + 1) * half)
        mix = (_dot(hm_ref[rows, :].astype(BF16), w_ref[0:M_WIDTH, :])
               + _dot(hg_ref[rows, :].astype(BF16), w_ref[M_WIDTH:M_WIDTH + G_WIDTH, :]))
        gt = gt_ref[0] if gt_ref.shape[1] == 1 else gt_ref[0, rows, :]
        y = DEEPNORM_ALPHA * x_ref[rows, :] + (1.0 + gt) * mix
        o_ref[rows, :] = _layer_norm(y, g_ref[...], b_ref[...])


def _outproj(hm, hg, x2d, gt, w_out, ln_g, ln_b, tm, tiles_per_mod):
    m, d = x2d.shape
    vec = pl.BlockSpec((1, d), lambda i: (0, 0))
    return pl.pallas_call(
        _outproj_kernel,
        grid=(m // tm,),
        in_specs=[pl.BlockSpec((tm, M_WIDTH), lambda i: (i, 0)),
                  pl.BlockSpec((tm, G_WIDTH), lambda i: (i, 0)),
                  pl.BlockSpec((tm, d), lambda i: (i, 0)),
                  _mod_spec(gt, tiles_per_mod, 1),
                  pl.BlockSpec((d, d), lambda i: (0, 0)),
                  vec, vec],
        out_specs=pl.BlockSpec((tm, d), lambda i: (i, 0)),
        out_shape=jax.ShapeDtypeStruct((m, d), F32),
        compiler_params=_cparams(("arbitrary",)),
        name="outproj",
    )(hm, hg, x2d, gt, w_out, ln_g.reshape(1, d), ln_b.reshape(1, d))


def _ffn_kernel(x_ref, sc_ref, sh_ref, gt_ref, wg_ref, wu_ref, wd_ref, g_ref, b_ref, o_ref, h_scr, acc):
    f = pl.program_id(1)

    @pl.when(f == 0)
    def _():
        h_scr[...] = (x_ref[...] * (1.0 + sc_ref[0]) + sh_ref[0]).astype(BF16)
        acc[...] = jnp.zeros_like(acc)

    h = h_scr[...]
    gate = _dot(h, wg_ref[...])
    up = _dot(h, wu_ref[...])
    act = (gate * jax.nn.sigmoid(gate) * up).astype(BF16)
    acc[...] += _dot(act, wd_ref[...])

    @pl.when(f == pl.num_programs(1) - 1)
    def _():
        y = DEEPNORM_ALPHA * x_ref[...] + (1.0 + gt_ref[0]) * acc[...]
        o_ref[...] = _layer_norm(y, g_ref[...], b_ref[...])


def _ffn(x2d, sc, sh, gt, w_gu, w_down, ln_g, ln_b, tm, tiles_per_mod):
    m, d = x2d.shape
    tf = TF_FFN
    nf = D_FF // tf
    mod_spec = _mod_spec(sc, tiles_per_mod, 2)
    vec = pl.BlockSpec((1, d), lambda i, f: (0, 0))
    return pl.pallas_call(
        _ffn_kernel,
        grid=(m // tm, nf),
        in_specs=[pl.BlockSpec((tm, d), lambda i, f: (i, 0)),
                  mod_spec, mod_spec, mod_spec,
                  pl.BlockSpec((d, tf), lambda i, f: (0, f)),
                  pl.BlockSpec((d, tf), lambda i, f: (0, nf + f)),
                  pl.BlockSpec((tf, d), lambda i, f: (f, 0)),
                  vec, vec],
        out_specs=pl.BlockSpec((tm, d), lambda i, f: (i, 0)),
        out_shape=jax.ShapeDtypeStruct((m, d), F32),
        scratch_shapes=[pltpu.VMEM((tm, d), BF16), pltpu.VMEM((tm, d), F32)],
        compiler_params=_cparams(("arbitrary", "arbitrary")),
        name="ffn",
    )(x2d, sc, sh, gt, w_gu, w_gu, w_down, ln_g.reshape(1, d), ln_b.reshape(1, d))


def _layer(x, ada6, weights, init_m, init_g, nb):
    batch, seq, d = x.shape
    L = CHUNK if seq % CHUNK == 0 else seq
    nc = seq // L
    m = batch * seq
    x2d = x.reshape(m, d)

    def tiling(tm_cap):
        tm = min(tm_cap, m)
        if seq % tm == 0:
            return tm, seq // tm, lambda a: a.reshape(batch, 1, d)
        return tm, 1, lambda a: jnp.repeat(a, seq, axis=0).reshape(m // tm, tm, d)

    sh1, sc1, gt1, sh2, sc2, gt2 = ada6
    tm, tpm, mod = tiling(TM_INPROJ)
    proj, gates = _inproj(x2d, mod(sc1), mod(sh1), weights["w_in_main"], weights["w_in_gate"], tm, tpm)
    gates_t = jnp.swapaxes(gates.reshape(batch, nc, L, GATE_PAD), 2, 3)
    proj_v, gates_v = _tok_view(proj, batch, nc), _tok_view(gates, batch, nc)
    gvecs = weights["gvecs"]
    hm, c1, n1, m1 = _mlstm(proj_v, gates_v, gates_t, gvecs, weights["m_norm_g"], batch, nc, L, nb["mlstm"],
                            init_m)
    pending = [name for name in ("w_out", "w_gu", "w_down") if weights[name].dtype != BF16]
    (hg, s1, conv1), converted = _gdn(proj_v, gates_v, gates_t, gvecs, weights["conv_w"], weights["g_norm_g"],
                                      batch, nc, L, nb["gdn"], init_g, [weights[name] for name in pending])
    weights.update(zip(pending, converted))
    tm, tpm, mod = tiling(TM_OUTPROJ)
    x1 = _outproj(hm.reshape(m, M_WIDTH), hg.reshape(m, G_WIDTH), x2d, mod(gt1), weights["w_out"],
                  weights["ln1_g"], weights["ln1_b"], tm, tpm)
    tm, tpm, mod = tiling(TM_FFN)
    y = _ffn(x1, mod(sc2), mod(sh2), mod(gt2), weights["w_gu"], weights["w_down"], weights["ln2_g"],
             weights["ln2_b"], tm, tpm)
    return y.reshape(batch, seq, d), c1, n1, m1[:, 0, :M_HEADS], s1, conv1


def kernel(x_prompt, x_sample, state_mlstm_C, state_mlstm_n, state_mlstm_m, state_gdn_S, state_gdn_conv,
           c_prompt, c_sample, w_ada, b_ada, w_in, m_i_bias, m_f_bias, m_norm_g, conv_w, g_dt_bias,
           g_A_log, g_norm_g, w_out, ln1_g, ln1_b, w_gu, w_down, ln2_g, ln2_b):
    bp, seq_p, d = x_prompt.shape
    bs, seq_s, _ = x_sample.shape

    ada = _ada(jnp.concatenate([c_prompt, c_sample], axis=0), w_ada, b_ada)
    ada6 = [ada[:, i * d:(i + 1) * d] for i in range(6)]

    gate_bias = jnp.zeros((GATE_PAD,), F32)
    gate_bias = gate_bias.at[GI0:GI0 + M_HEADS].set(m_i_bias).at[GF0:GF0 + M_HEADS].set(m_f_bias)
    gate_bias = gate_bias.at[GA0:GA0 + G_HEADS].set(g_dt_bias)
    a_log = jnp.zeros((GATE_PAD,), F32).at[GA0:GA0 + G_HEADS].set(g_A_log)
    weights = {
        "w_in_main": _cast_bf16(w_in.T, MAIN_COLS, d),
        "w_in_gate": jnp.pad(w_in.T[MAIN_COLS:], ((0, GATE_PAD - GATE_COLS), (0, 0))).astype(BF16),
        "gvecs": (gate_bias.reshape(1, GATE_PAD), gate_bias.reshape(GATE_PAD, 1),
                  a_log.reshape(1, GATE_PAD), a_log.reshape(GATE_PAD, 1)),
        "m_norm_g": m_norm_g, "conv_w": conv_w, "g_norm_g": g_norm_g,
        "w_out": w_out, "w_gu": w_gu, "w_down": w_down,
        "ln1_g": ln1_g, "ln1_b": ln1_b, "ln2_g": ln2_g, "ln2_b": ln2_b,
    }

    y_p, p_c, p_n, p_m, p_s, p_conv = _layer(x_prompt, [a[:bp] for a in ada6], weights, None, None,
                                             SEQS_PER_STEP_PROMPT)
    m0 = jnp.pad(state_mlstm_m, ((0, 0), (0, GATE_PAD - M_HEADS))).reshape(bs, 1, GATE_PAD)
    y_s, s_c, s_n, s_m, s_s, s_conv = _layer(
        x_sample, [a[bp:] for a in ada6], weights, (state_mlstm_C, state_mlstm_n, m0),
        (state_gdn_S, state_gdn_conv), SEQS_PER_STEP_SAMPLE)
    return (y_p, y_s, p_c, p_n, p_m, p_s, p_conv, s_c, s_n, s_m, s_s, s_conv)
```

```python
import functools

import jax
import jax.numpy as jnp
from jax import lax
from jax.experimental import pallas as pl
from jax.experimental.pallas import tpu as pltpu

F32 = jnp.float32
BF16 = jnp.bfloat16

D_MODEL = 2048
M_HEADS = 4
M_HEAD_DIM = 256
M_WIDTH = M_HEADS * M_HEAD_DIM
G_HEADS = 8
G_HEAD_DIM = 128
G_WIDTH = G_HEADS * G_HEAD_DIM
CONV_W = 4
CHUNK = 64
D_FF = 5632
MAIN_COLS = 4 * M_WIDTH + 3 * G_WIDTH + G_WIDTH
GATE_COLS = 2 * M_HEADS + 2 * G_HEADS
GATE_PAD = 128
DEEPNORM_ALPHA = 2.0 ** 0.25
LN_EPS = 1e-5
RMS_EPS = 1e-6
GI0, GF0, GB0, GA0 = 0, M_HEADS, 2 * M_HEADS, 2 * M_HEADS + G_HEADS

SUBLANES = 8
VMEM_LIMIT_BYTES = 56 * 1024 * 1024
TM_INPROJ, TM_OUTPROJ, TM_FFN = 1024, 512, 512
TN_INPROJ, TN_ADA, TF_FFN = 1024, 1024, 512
OUTPROJ_SLABS = 2
CAST_ROWS, CAST_COLS = 2048, 1024
SEQS_PER_STEP_PROMPT = {"mlstm": 1, "gdn": 2}
SEQS_PER_STEP_SAMPLE = {"mlstm": 4, "gdn": 8}
MLSTM_GROUP_LONG, MLSTM_GROUP_SHORT = 1, 2
SMALL_L = 8
PAIR_ROWS = 16


def _cparams(sem, flags=None):
    return pltpu.CompilerParams(dimension_semantics=sem, vmem_limit_bytes=VMEM_LIMIT_BYTES, flags=flags)


def _dot(a, b):
    return jnp.dot(a, b, preferred_element_type=F32)


def _dot_nt(a, b):
    return lax.dot_general(a, b, (((1,), (1,)), ((), ())), preferred_element_type=F32)


def _dot_tn(a, b):
    return lax.dot_general(a, b, (((0,), (0,)), ((), ())), preferred_element_type=F32)


def _split2(x):
    hi = x.astype(BF16)
    return hi, (x - hi.astype(F32)).astype(BF16)


def _split3(x):
    hi = x.astype(BF16)
    r = x - hi.astype(F32)
    mid = r.astype(BF16)
    return hi, mid, (r - mid.astype(F32)).astype(BF16)


def _dot3(a, b):
    return _dot(a[0], b[0]) + (_dot(a[0], b[1]) + _dot(a[1], b[0]))


def _mm_small(a, b):
    out = a[:, 0:1] * b[0:1, :]
    for i in range(1, a.shape[1]):
        out = out + a[:, i:i + 1] * b[i:i + 1, :]
    return out


def _tok_get(ref, b, L, cols):
    if len(ref.shape) == 3:
        return ref[b, :, cols]
    return ref[b * L:(b + 1) * L, cols]


def _tok_set(ref, b, L, cols, val):
    if len(ref.shape) == 3:
        ref[b, :, cols] = val
    else:
        ref[b * L:(b + 1) * L, cols] = val


ALL = slice(None)


def _cast_kernel(x_ref, o_ref):
    o_ref[...] = x_ref[...].astype(BF16)


def _cast_bf16(w, rows, cols):
    bc = min(cols, CAST_COLS)
    br = max(r for r in range(SUBLANES, min(rows, CAST_ROWS) + 1, SUBLANES) if rows % r == 0)
    return pl.pallas_call(
        _cast_kernel,
        grid=(rows // br, cols // bc),
        in_specs=[pl.BlockSpec((br, bc), lambda i, j: (i, j))],
        out_specs=pl.BlockSpec((br, bc), lambda i, j: (i, j)),
        out_shape=jax.ShapeDtypeStruct((rows, cols), BF16),
        compiler_params=_cparams(("arbitrary", "arbitrary")),
        name="cast",
    )(w)


def _ada_kernel(c_ref, w_ref, b_ref, o_ref):
    c = c_ref[...]
    a = (c * jax.nn.sigmoid(c)).astype(BF16)
    o_ref[...] = _dot(a, w_ref[...].astype(BF16)) + b_ref[...]


def _ada(c_all, w_ada, b_ada):
    n_rows, d = c_all.shape
    n_cols = w_ada.shape[1]
    tn = TN_ADA
    return pl.pallas_call(
        _ada_kernel,
        grid=(n_cols // tn,),
        in_specs=[pl.BlockSpec((n_rows, d), lambda j: (0, 0)),
                  pl.BlockSpec((d, tn), lambda j: (0, j)),
                  pl.BlockSpec((1, tn), lambda j: (0, j))],
        out_specs=pl.BlockSpec((n_rows, tn), lambda j: (0, j)),
        out_shape=jax.ShapeDtypeStruct((n_rows, n_cols), F32),
        compiler_params=_cparams(("arbitrary",)),
        name="ada",
    )(c_all, w_ada, b_ada.reshape(1, n_cols))


def _inproj_kernel(x_ref, sc_ref, sh_ref, wt_ref, wgt_ref, o_ref, og_ref, h_scr):
    @pl.when(pl.program_id(1) == 0)
    def _():
        h = (x_ref[...] * (1.0 + sc_ref[0]) + sh_ref[0]).astype(BF16)
        h_scr[...] = h
        og_ref[...] = _dot_nt(h, wgt_ref[...])

    o_ref[...] = _dot_nt(h_scr[...], wt_ref[...])


def _mod_spec(mod, tiles_per_mod, grid_rank):
    _, mod_rows, d = mod.shape
    if grid_rank == 1:
        index_map = lambda i: (i // tiles_per_mod, 0, 0)
    else:
        index_map = lambda i, j: (i // tiles_per_mod, 0, 0)
    if mod_rows == 1:
        return pl.BlockSpec((1, 1, d), index_map)
    return pl.BlockSpec((1, mod_rows, d), index_map, pipeline_mode=pl.Buffered(1))


def _inproj(x2d, sc, sh, w_main, w_gate, tm, tiles_per_mod):
    m, d = x2d.shape
    tn = TN_INPROJ
    mod_spec = _mod_spec(sc, tiles_per_mod, 2)
    return pl.pallas_call(
        _inproj_kernel,
        grid=(m // tm, MAIN_COLS // tn),
        in_specs=[pl.BlockSpec((tm, d), lambda i, j: (i, 0)),
                  mod_spec, mod_spec,
                  pl.BlockSpec((tn, d), lambda i, j: (j, 0)),
                  pl.BlockSpec((GATE_PAD, d), lambda i, j: (0, 0))],
        out_specs=[pl.BlockSpec((tm, tn), lambda i, j: (i, j)),
                   pl.BlockSpec((tm, GATE_PAD), lambda i, j: (i, 0))],
        out_shape=[jax.ShapeDtypeStruct((m, MAIN_COLS), F32),
                   jax.ShapeDtypeStruct((m, GATE_PAD), F32)],
        scratch_shapes=[pltpu.VMEM((tm, d), BF16)],
        compiler_params=_cparams(("arbitrary", "arbitrary")),
        name="inproj",
    )(x2d, sc, sh, w_main, w_gate)


def _gate_tables(g, bias, alog, gid):
    x = g + bias
    is_f = (gid >= GF0) & (gid < GB0)
    is_a = (gid >= GA0) & (gid < GA0 + G_HEADS)
    log_f = jax.nn.log_sigmoid(x)
    log_a = -jnp.exp(alog) * jax.nn.softplus(x)
    inc = jnp.where(is_f, log_f, jnp.where(is_a, log_a, 0.0))
    return x, inc, jax.nn.sigmoid(x)


def _chunk_masks(L):
    row = lax.broadcasted_iota(jnp.int32, (L, L), 0)
    col = lax.broadcasted_iota(jnp.int32, (L, L), 1)
    return row >= col, row > col, row == col


def _gates_col_form(g_col, gb_row_ref, al_row_ref, L):
    causal, _, _ = _chunk_masks(L)
    tril = causal.astype(F32)
    gid_c = lax.broadcasted_iota(jnp.int32, (L, GATE_PAD), 1)
    x_c, inc_c, beta_c = _gate_tables(g_col, gb_row_ref[...], al_row_ref[...], gid_c)
    if L <= SMALL_L:
        cum_c = _mm_small(tril, inc_c)
    else:
        tril_b = tril.astype(BF16)
        c1, c2, c3 = _split3(inc_c)
        cum_c = _dot(tril_b, c1) + (_dot(tril_b, c2) + _dot(tril_b, c3))
    return x_c, cum_c, beta_c


def _gates_both_forms(g_col, g_row, gb_row_ref, gb_col_ref, al_row_ref, al_col_ref, L):
    x_c, cum_c, beta_c = _gates_col_form(g_col, gb_row_ref, al_row_ref, L)
    triu = (lax.broadcasted_iota(jnp.int32, (L, L), 0) <= lax.broadcasted_iota(jnp.int32, (L, L), 1)).astype(F32)
    gid_r = lax.broadcasted_iota(jnp.int32, (GATE_PAD, L), 0)
    x_r, inc_r, _ = _gate_tables(g_row, gb_col_ref[...], al_col_ref[...], gid_r)
    if L <= SMALL_L:
        cum_r = _mm_small(inc_r, triu)
    else:
        triu_b = triu.astype(BF16)
        r1, r2, r3 = _split3(inc_r)
        cum_r = _dot(r1, triu_b) + (_dot(r2, triu_b) + _dot(r3, triu_b))
    return x_c, cum_c, beta_c, x_r, cum_r


def _rms_gate(h, gain, gate):
    return h * lax.rsqrt(jnp.mean(h * h, axis=-1, keepdims=True) + RMS_EPS) * gain * gate


def _mlstm_kernel(*refs, L, nb, nc, has_init, group):
    if has_init:
        (q_ref, k_ref, v_ref, o_ref, gcol_ref, grow_ref, gb_row_ref, gb_col_ref, al_row_ref, al_col_ref,
         ng_ref, c0_ref, n0_ref, m0_ref, h_ref, c_ref, n_ref, m_ref) = refs
    else:
        (q_ref, k_ref, v_ref, o_ref, gcol_ref, grow_ref, gb_row_ref, gb_col_ref, al_row_ref, al_col_ref,
         ng_ref, h_ref, c_ref, n_ref, m_ref) = refs

    if has_init and nc == 1:
        c_src, n_src, m_src = c0_ref, n0_ref, m0_ref
    else:
        c_src, n_src, m_src = c_ref, n_ref, m_ref

        @pl.when(pl.program_id(1) == 0)
        def _():
            if has_init:
                c_ref[...] = c0_ref[...]
                n_ref[...] = n0_ref[...]
                m_ref[...] = m0_ref[...]
            else:
                c_ref[...] = jnp.zeros_like(c_ref)
                n_ref[...] = jnp.zeros_like(n_ref)
                m_ref[...] = jnp.zeros_like(m_ref)

    for b0 in range(0, nb, group):
        _mlstm_group(range(b0, min(b0 + group, nb)), L, q_ref, k_ref, v_ref, o_ref, gcol_ref, grow_ref, gb_row_ref,
                     gb_col_ref, al_row_ref, al_col_ref, ng_ref, c_src, n_src, m_src, h_ref, c_ref, n_ref, m_ref)


def _mlstm_group(seqs, L, q_ref, k_ref, v_ref, o_ref, gcol_ref, grow_ref, gb_row_ref, gb_col_ref, al_row_ref,
                 al_col_ref, ng_ref, c_src, n_src, m_src, h_ref, c_ref, n_ref, m_ref):
    causal, _, _ = _chunk_masks(L)
    lane = lax.broadcasted_iota(jnp.int32, (1, GATE_PAD), 1)
    scale = M_HEAD_DIM ** -0.5
    gates = {b: _gates_both_forms(_tok_get(gcol_ref, b, L, ALL), grow_ref[b, 0], gb_row_ref, gb_col_ref,
                                  al_row_ref, al_col_ref, L) for b in seqs}
    m_all = {b: m_src[b] for b in seqs}
    probs = [(b, h) for b in seqs for h in range(M_HEADS)]

    def cols(h):
        return slice(h * M_HEAD_DIM, (h + 1) * M_HEAD_DIM)

    q = [_tok_get(q_ref, b, L, cols(h)) for b, h in probs]
    k = [_tok_get(k_ref, b, L, cols(h)) * scale for b, h in probs]
    vb = [_tok_get(v_ref, b, L, cols(h)).astype(BF16) for b, h in probs]
    qb = [x.astype(BF16) for x in q]
    kb = [x.astype(BF16) for x in k]
    ig_c = [gates[b][0][:, GI0 + h:GI0 + h + 1] for b, h in probs]
    ig_r = [gates[b][3][GI0 + h:GI0 + h + 1, :] for b, h in probs]
    bt_c = [gates[b][1][:, GF0 + h:GF0 + h + 1] for b, h in probs]
    bt_r = [gates[b][4][GF0 + h:GF0 + h + 1, :] for b, h in probs]
    m0 = [jnp.sum(jnp.where(lane == h, m_all[b], 0.0), axis=1, keepdims=True) for b, h in probs]
    n_p = len(probs)
    rng = range(n_p)

    log_d = [jnp.where(causal, bt_c[i] - bt_r[i] + ig_r[i], -jnp.inf) for i in rng]
    inter = [bt_c[i] + m0[i] for i in rng]
    m_t = [jnp.maximum(inter[i], jnp.max(log_d[i], axis=1, keepdims=True)) for i in rng]
    inter_w = [jnp.exp(inter[i] - m_t[i]) for i in rng]
    qk = [_dot_nt(qb[i], kb[i]) for i in rng]
    c_old = [c_src[b, h] for b, h in probs]
    n_old = [n_src[b, h:h + 1, :] for b, h in probs]
    qc = [_dot(qb[i], c_old[i].astype(BF16)) for i in rng]
    s = [qk[i] * jnp.exp(log_d[i] - m_t[i]) for i in rng]
    sv = [_dot(s[i].astype(BF16), vb[i]) for i in rng]

    b_last = [bt_c[i][L - 1:L, :] for i in rng]
    m_new = [jnp.maximum(b_last[i] + m0[i], jnp.max(b_last[i] - bt_r[i] + ig_r[i], axis=1, keepdims=True))
             for i in rng]
    kw = [k[i] * jnp.exp(b_last[i] - bt_c[i] + ig_c[i] - m_new[i]) for i in rng]
    decay = [jnp.exp(b_last[i] + m0[i] - m_new[i]) for i in rng]
    kv = [_dot_tn(kw[i].astype(BF16), vb[i]) for i in rng]

    m_next = dict(m_all)
    for i, (b, h) in enumerate(probs):
        num = inter_w[i] * qc[i] + sv[i]
        den = inter_w[i] * jnp.sum(q[i] * n_old[i], axis=1, keepdims=True) + jnp.sum(s[i], axis=1, keepdims=True)
        hh = num / jnp.maximum(jnp.abs(den), jnp.exp(-m_t[i]))
        c_ref[b, h] = decay[i] * c_old[i] + kv[i]
        n_ref[b, h:h + 1, :] = decay[i] * n_old[i] + jnp.sum(kw[i], axis=0, keepdims=True)
        m_next[b] = jnp.where(lane == h, m_new[i], m_next[b])
        gate = jax.nn.sigmoid(_tok_get(o_ref, b, L, cols(h)))
        _tok_set(h_ref, b, L, cols(h), _rms_gate(hh, ng_ref[:, cols(h)], gate).astype(h_ref.dtype))
    for b in seqs:
        m_ref[b] = m_next[b]


def _tok_spec(L, nb, nc, width, col_block):
    if nc == 1:
        return pl.BlockSpec((nb * L, width), lambda b, c: (b, col_block))
    return pl.BlockSpec((nb, L, width), lambda b, c: (b, c, col_block))


def _tok_view(x2d, batch, nc):
    return x2d if nc == 1 else x2d.reshape(batch, x2d.shape[0] // batch, x2d.shape[1])


def _gate_specs(L, nb, nc):
    vec_r = pl.BlockSpec((1, GATE_PAD), lambda b, c: (0, 0))
    vec_c = pl.BlockSpec((GATE_PAD, 1), lambda b, c: (0, 0))
    return [_tok_spec(L, nb, nc, GATE_PAD, 0),
            pl.BlockSpec((nb, 1, GATE_PAD, L), lambda b, c: (b, c, 0, 0)),
            vec_r, vec_c, vec_r, vec_c]


def _scan_out_dtype(L):
    return BF16 if L % 16 == 0 else F32


BF16_ROWS = 16


def _slab_plan(rows, n_steps):
    period = 1
    while (rows * period) % (n_steps * BF16_ROWS):
        period *= 2
        assert period <= n_steps
    return rows * period // n_steps, period


def _side_cast_specs(side_cast, n_steps, nc):
    specs, periods = [], []
    for w in side_cast:
        slab, period = _slab_plan(w.shape[0], n_steps)
        specs.append(pl.BlockSpec((slab, w.shape[1]), lambda b, c, period=period: ((b * nc + c) // period, 0)))
        periods.append(period)
    return specs, tuple(periods)


def _run_side_casts(cast_src, cast_dst, periods, nc):
    step = pl.program_id(0) * nc + pl.program_id(1)
    for src, dst, period in zip(cast_src, cast_dst, periods):
        if period == 1:
            dst[...] = src[...].astype(BF16)
        else:
            @pl.when(step % period == 0)
            def _(src=src, dst=dst):
                dst[...] = src[...].astype(BF16)


def _mlstm(proj, gates, gates_t, gvecs, norm_g, batch, nc, L, nb, init):
    has_init = init is not None
    in_specs = [_tok_spec(L, nb, nc, M_WIDTH, j) for j in range(4)] + _gate_specs(L, nb, nc)
    in_specs.append(pl.BlockSpec((1, M_WIDTH), lambda b, c: (0, 0)))
    c_spec = pl.BlockSpec((nb, M_HEADS, M_HEAD_DIM, M_HEAD_DIM), lambda b, c: (b, 0, 0, 0))
    n_spec = pl.BlockSpec((nb, M_HEADS, M_HEAD_DIM), lambda b, c: (b, 0, 0))
    m_spec = pl.BlockSpec((nb, 1, GATE_PAD), lambda b, c: (b, 0, 0))
    args = [proj, proj, proj, proj, gates, gates_t, *gvecs, norm_g.reshape(1, M_WIDTH)]
    if has_init:
        in_specs += [c_spec, n_spec, m_spec]
        args += list(init)
    return pl.pallas_call(
        functools.partial(_mlstm_kernel, L=L, nb=nb, nc=nc, has_init=has_init,
                          group=MLSTM_GROUP_SHORT if L <= SMALL_L else MLSTM_GROUP_LONG),
        grid=(batch // nb, nc),
        in_specs=in_specs,
        out_specs=[_tok_spec(L, nb, nc, M_WIDTH, 0), c_spec, n_spec, m_spec],
        out_shape=[jax.ShapeDtypeStruct(proj.shape[:-1] + (M_WIDTH,), _scan_out_dtype(L)),
                   jax.ShapeDtypeStruct((batch, M_HEADS, M_HEAD_DIM, M_HEAD_DIM), F32),
                   jax.ShapeDtypeStruct((batch, M_HEADS, M_HEAD_DIM), F32),
                   jax.ShapeDtypeStruct((batch, 1, GATE_PAD), F32)],
        compiler_params=_cparams(("arbitrary", "arbitrary")),
        name="mlstm",
    )(*args)


def _inv_unit_lower_small(a_list, L):
    _, _, eye = _chunk_masks(L)
    p = [-a for a in a_list]
    t = [eye.astype(F32) + x for x in p]
    for _ in range(max(1, (L - 1).bit_length()) - 1):
        p = [_mm_small(x, x) for x in p]
        t = [ti + _mm_small(ti, pi) for ti, pi in zip(t, p)]
    return t


def _gate_rows_paired(gp, bias_p, alog_p, L):
    rid = lax.broadcasted_iota(jnp.int32, gp.shape, 0)
    x = gp + bias_p
    is_f = (rid >= GF0 // 2) & (rid < GB0 // 2)
    is_a = (rid >= GA0 // 2) & (rid < (GA0 + G_HEADS) // 2)
    inc = jnp.where(is_f, jax.nn.log_sigmoid(x), jnp.where(is_a, -jnp.exp(alog_p) * jax.nn.softplus(x), 0.0))
    r = lax.broadcasted_iota(jnp.int32, (2 * L, 2 * L), 0)
    c = lax.broadcasted_iota(jnp.int32, (2 * L, 2 * L), 1)
    triu2 = (((r < L) == (c < L)) & (r <= c)).astype(BF16)
    p1, p2, p3 = _split3(inc)
    return _dot(p1, triu2) + (_dot(p2, triu2) + _dot(p3, triu2))


def _gdn_heads_paired(qn, kn, vv, gates_c, cum_p, s_ref, z_ref, h_ref, ng_ref, L, nb):
    dh = G_HEAD_DIM
    row =lax.broadcasted_iota(jnp.int32, (L, 2 * L), 0)
    lane = lax.broadcasted_iota(jnp.int32, (L, 2 * L), 1)
    left = lane < L
    colp = jnp.where(left, lane, lane - L)
    causal_p, strict_p, eye_p = row >= colp, row > colp, (row == colp).astype(F32)
    left2 = lax.broadcasted_iota(jnp.int32, (L, 2 * dh), 1) < dh
    lane4 = lax.broadcasted_iota(jnp.int32, (L, 4 * dh), 1)
    first4 = (lane4 // dh) % 2 == 0

    def bdiag(y, first):
        z = jnp.zeros_like(y)
        return jnp.concatenate([jnp.where(first, y, z), jnp.where(first, z, y)], axis=0)

    pairs = [(b, p) for b in range(nb) for p in range(G_HEADS // 2)]
    rng = range(len(pairs))
    q2 = [qn[b][:, 2 * p * dh:(2 * p + 2) * dh] for b, p in pairs]
    k2 = [kn[b][:, 2 * p * dh:(2 * p + 2) * dh] for b, p in pairs]
    v2 = [vv[b][:, 2 * p * dh:(2 * p + 2) * dh] for b, p in pairs]
    q2b = [x.astype(BF16) for x in q2]
    k2b = [x.astype(BF16) for x in k2]
    kbd = [bdiag(x, left2) for x in k2b]
    b_ca = [gates_c[b][1][:, GA0 + 2 * p:GA0 + 2 * p + 1] for b, p in pairs]
    b_cb = [gates_c[b][1][:, GA0 + 2 * p + 1:GA0 + 2 * p + 2] for b, p in pairs]
    bet_a = [gates_c[b][2][:, GB0 + 2 * p:GB0 + 2 * p + 1] for b, p in pairs]
    bet_b = [gates_c[b][2][:, GB0 + 2 * p + 1:GB0 + 2 * p + 2] for b, p in pairs]
    b_r = [cum_p[b][GA0 // 2 + p:GA0 // 2 + p + 1, :] for b, p in pairs]
    decay = [jnp.exp(jnp.where(causal_p, jnp.where(left, b_ca[i], b_cb[i]) - b_r[i], -jnp.inf)) for i in rng]
    kk = [_dot_nt(k2b[i], kbd[i]) for i in rng]
    qk = [(_dot_nt(q2b[i], kbd[i]) * decay[i]).astype(BF16) for i in rng]
    s_old = [(s_ref[b, 2 * p], s_ref[b, 2 * p + 1]) for b, p in pairs]
    zero = jnp.zeros((dh, dh), BF16)
    sbd = [jnp.concatenate([jnp.concatenate([sa.astype(BF16), zero], axis=1),
                            jnp.concatenate([zero, sb.astype(BF16)], axis=1)], axis=0) for sa, sb in s_old]
    qs = [_dot(q2b[i], sbd[i]) for i in rng]

    a = [jnp.where(strict_p, jnp.where(left, bet_a[i], bet_b[i]) * kk[i] * decay[i], 0.0) for i in rng]
    p = [-x for x in a]
    t = [eye_p + x for x in p]
    ps = [_split2(x) for x in p]
    p = [_dot3(ps[i], (bdiag(ps[i][0], left), bdiag(ps[i][1], left))) for i in rng]
    for lvl in range(1, (L - 1).bit_length()):
        last = lvl == (L - 1).bit_length() - 1
        ps = [_split2(x) for x in p]
        pbd = [(bdiag(hi, left), bdiag(lo, left)) for hi, lo in ps]
        if last:
            t = [t[i] + _dot3(_split2(t[i]), pbd[i]) for i in rng]
        else:
            ts = [_split2(x) for x in t]
            both = [_dot3((jnp.concatenate([ts[i][0], ps[i][0]], axis=0),
                           jnp.concatenate([ts[i][1], ps[i][1]], axis=0)), pbd[i]) for i in rng]
            t = [t[i] + both[i][0:L] for i in rng]
            p = [both[i][L:2 * L] for i in rng]

    bet2 = [jnp.where(left2, bet_a[i], bet_b[i]) for i in rng]
    eb2 = [jnp.where(left2, jnp.exp(b_ca[i]), jnp.exp(b_cb[i])) for i in rng]
    rhs = [_split2(jnp.concatenate([bet2[i] * v2[i], (bet2[i] * eb2[i]) * k2[i]], axis=-1)) for i in rng]
    sol = [_dot3(_split2(t[i]), (bdiag(rhs[i][0], first4), bdiag(rhs[i][1], first4))) for i in rng]
    u = [sol[i][:, 0:2 * dh] - _dot(sol[i][:, 2 * dh:4 * dh].astype(BF16), sbd[i]) for i in rng]
    ub = [x.astype(BF16) for x in u]
    o = [eb2[i] * qs[i] + _dot(qk[i], bdiag(ub[i], left2)) for i in rng]
    bl_a = [b_ca[i][L - 1:L, :] for i in rng]
    bl_b = [b_cb[i][L - 1:L, :] for i in rng]
    wk = [(jnp.where(left2, jnp.exp(bl_a[i] - b_ca[i]), jnp.exp(bl_b[i] - b_cb[i])) * k2[i]).astype(BF16)
          for i in rng]
    ds = [_dot_tn(wk[i], ub[i]) for i in rng]
    for i, (b, p) in enumerate(pairs):
        for j, bl in enumerate((bl_a[i], bl_b[i])):
            h = 2 * p + j
            blk = slice(j * dh, (j + 1) * dh)
            cols = slice(h * dh, (h + 1) * dh)
            s_ref[b, h] = jnp.exp(bl) * s_old[i][j] + ds[i][blk, blk]
            z = _tok_get(z_ref, b, L, cols)
            _tok_set(h_ref, b, L, cols,
                     _rms_gate(o[i][:, blk], ng_ref[:, cols], z * jax.nn.sigmoid(z)).astype(h_ref.dtype))


CONV_HIST = CONV_W - 1
CONV_BASE = SUBLANES - CONV_HIST


def _unit(x):
    return x * lax.rsqrt(jnp.sum(x * x, axis=-1, keepdims=True) + RMS_EPS)


def _gdn_conv_act(b, xq_ref, xk_ref, xv_ref, cw_ref, conv_ref, buf, L, carry):
    buf[b, SUBLANES:SUBLANES + L, 0:G_WIDTH] = _tok_get(xq_ref, b, L, ALL)
    buf[b, SUBLANES:SUBLANES + L, G_WIDTH:2 * G_WIDTH] = _tok_get(xk_ref, b, L, ALL)
    buf[b, SUBLANES:SUBLANES + L, 2 * G_WIDTH:3 * G_WIDTH] = _tok_get(xv_ref, b, L, ALL)
    if L % SUBLANES == 0:
        rows = buf[b, 0:SUBLANES + L, :]
        y = cw_ref[CONV_HIST:CONV_W, :] * rows[SUBLANES:, :]
        for shift in range(1, CONV_W):
            j = CONV_HIST - shift
            y = y + cw_ref[j:j + 1, :] * pltpu.roll(rows, shift, 0)[SUBLANES:, :]
    else:
        y = cw_ref[0:1, :] * buf[b, CONV_BASE:CONV_BASE + L, :]
        for j in range(1, CONV_W):
            y = y + cw_ref[j:j + 1, :] * buf[b, CONV_BASE + j:CONV_BASE + j + L, :]
    conv_ref[b] = buf[b, SUBLANES + L - CONV_HIST:SUBLANES + L, :]
    if carry:
        buf[b, 0:SUBLANES, :] = buf[b, L:L + SUBLANES, :]
    return y * jax.nn.sigmoid(y)


def _gdn_paired_kernel(*refs, L, nb, nc, cast_periods):
    (xq_ref, xk_ref, xv_ref, z_ref, gcol_ref, gb_row_ref, al_row_ref, gp_ref, gbp_ref, alp_ref, cw_ref,
     ng_ref) = refs[:12]
    n_cast = len(cast_periods)
    cast_src = refs[12:12 + n_cast]
    h_ref, s_ref, conv_ref = refs[12 + n_cast:15 + n_cast]
    cast_dst = refs[15 + n_cast:15 + 2 * n_cast]
    buf = refs[15 + 2 * n_cast]
    _run_side_casts(cast_src, cast_dst, cast_periods, nc)

    @pl.when(pl.program_id(1) == 0)
    def _():
        buf[:, 0:SUBLANES, :] = jnp.zeros((nb, SUBLANES, 3 * G_WIDTH), F32)
        s_ref[...] = jnp.zeros_like(s_ref)

    scale = G_HEAD_DIM ** -0.5
    qn, kn, vv = [], [], []
    for b in range(nb):
        act = _gdn_conv_act(b, xq_ref, xk_ref, xv_ref, cw_ref, conv_ref, buf, L, True)
        qn.append(jnp.concatenate(
            [_unit(act[:, h * G_HEAD_DIM:(h + 1) * G_HEAD_DIM]) * scale for h in range(G_HEADS)], axis=-1))
        kn.append(jnp.concatenate(
            [_unit(act[:, G_WIDTH + h * G_HEAD_DIM:G_WIDTH + (h + 1) * G_HEAD_DIM]) for h in range(G_HEADS)],
            axis=-1))
        vv.append(act[:, 2 * G_WIDTH:3 * G_WIDTH])
    gates_c = [_gates_col_form(gcol_ref[b], gb_row_ref, al_row_ref, L) for b in range(nb)]
    cum_p = [_gate_rows_paired(gp_ref[b, 0], gbp_ref[...], alp_ref[...], L) for b in range(nb)]
    _gdn_heads_paired(qn, kn, vv, gates_c, cum_p, s_ref, z_ref, h_ref, ng_ref, L, nb)


def _gdn_kernel(*refs, L, nb, nc, has_init):
    xq_ref, xk_ref, xv_ref, z_ref, gcol_ref, grow_ref, gb_row_ref, gb_col_ref, al_row_ref, al_col_ref = refs[:10]
    if has_init:
        cw_ref, ng_ref, s0_ref, conv0_ref, h_ref, s_ref, conv_ref, buf = refs[10:]
    else:
        cw_ref, ng_ref, h_ref, s_ref, conv_ref, buf = refs[10:]

    def cols(h):
        return slice(h * G_HEAD_DIM, (h + 1) * G_HEAD_DIM)

    s_src = s0_ref if (has_init and nc == 1) else s_ref

    @pl.when(pl.program_id(1) == 0)
    def _():
        buf[:, 0:SUBLANES, :] = jnp.zeros((nb, SUBLANES, 3 * G_WIDTH), F32)
        if has_init:
            for b in range(nb):
                buf[b, CONV_BASE:SUBLANES, :] = conv0_ref[b]
            if nc > 1:
                s_ref[...] = s0_ref[...]
        else:
            s_ref[...] = jnp.zeros_like(s_ref)

    act = [_gdn_conv_act(b, xq_ref, xk_ref, xv_ref, cw_ref, conv_ref, buf, L, nc > 1) for b in range(nb)]
    causal, strict, _ = _chunk_masks(L)
    scale = G_HEAD_DIM ** -0.5
    gates = [_gates_both_forms(_tok_get(gcol_ref, b, L, ALL), grow_ref[b, 0], gb_row_ref, gb_col_ref,
                               al_row_ref, al_col_ref, L) for b in range(nb)]
    probs = [(b, h) for b in range(nb) for h in range(G_HEADS)]
    rng = range(len(probs))

    q = [_unit(act[b][:, h * G_HEAD_DIM:(h + 1) * G_HEAD_DIM]) * scale for b, h in probs]
    k = [_unit(act[b][:, G_WIDTH + h * G_HEAD_DIM:G_WIDTH + (h + 1) * G_HEAD_DIM]) for b, h in probs]
    v = [act[b][:, 2 * G_WIDTH + h * G_HEAD_DIM:2 * G_WIDTH + (h + 1) * G_HEAD_DIM] for b, h in probs]
    qb = [x.astype(BF16) for x in q]
    kb = [x.astype(BF16) for x in k]
    b_c = [gates[b][1][:, GA0 + h:GA0 + h + 1] for b, h in probs]
    b_r = [gates[b][4][GA0 + h:GA0 + h + 1, :] for b, h in probs]
    bet = [gates[b][2][:, GB0 + h:GB0 + h + 1] for b, h in probs]
    decay = [jnp.exp(jnp.where(causal, b_c[i] - b_r[i], -jnp.inf)) for i in rng]
    eb = [jnp.exp(b_c[i]) for i in rng]
    kk = [_dot_nt(kb[i], kb[i]) for i in rng]
    qk = [_dot_nt(qb[i], kb[i]) * decay[i] for i in rng]
    s_old = [s_src[b, h] for b, h in probs]
    sb = [x.astype(BF16) for x in s_old]
    qs = [_dot(qb[i], sb[i]) for i in rng]

    t = _inv_unit_lower_small([jnp.where(strict, bet[i] * kk[i] * decay[i], 0.0) for i in rng], L)
    rhs = [jnp.concatenate([bet[i] * v[i], (bet[i] * eb[i]) * k[i]], axis=-1) for i in rng]
    sol = [_mm_small(t[i], rhs[i]) for i in rng]
    u = [sol[i][:, 0:G_HEAD_DIM] - _dot(sol[i][:, G_HEAD_DIM:2 * G_HEAD_DIM].astype(BF16), sb[i]) for i in rng]
    ub = [x.astype(BF16) for x in u]
    o = [eb[i] * qs[i] + _dot(qk[i].astype(BF16), ub[i]) for i in rng]
    b_last = [b_c[i][L - 1:L, :] for i in rng]
    wk = [(jnp.exp(b_last[i] - b_c[i]) * k[i]).astype(BF16) for i in rng]
    ds = [_dot_tn(wk[i], ub[i]) for i in rng]
    for i, (b, h) in enumerate(probs):
        s_ref[b, h] = jnp.exp(b_last[i]) * s_old[i] + ds[i]
        z = _tok_get(z_ref, b, L, cols(h))
        _tok_set(h_ref, b, L, cols(h),
                 _rms_gate(o[i], ng_ref[:, cols(h)], z * jax.nn.sigmoid(z)).astype(h_ref.dtype))


def _gdn_paired(proj, gates, gates_t, gvecs, conv_w, norm_g, batch, nc, L, nb, side_cast):
    assert 2 * L == GATE_PAD and nc > 1
    cast_specs, cast_periods = _side_cast_specs(side_cast, (batch // nb) * nc, nc)
    blk0 = 4 * M_WIDTH // G_WIDTH

    def tok(width, col_block):
        return pl.BlockSpec((nb, L, width), lambda b, c: (b, c, col_block))

    vec_r = pl.BlockSpec((1, GATE_PAD), lambda b, c: (0, 0))
    tab = pl.BlockSpec((PAIR_ROWS, GATE_PAD), lambda b, c: (0, 0))
    gb_row, _, al_row, _ = gvecs
    pair = lambda vec: jnp.repeat(vec.reshape(GATE_PAD)[:2 * PAIR_ROWS], L).reshape(PAIR_ROWS, GATE_PAD)
    in_specs = [tok(G_WIDTH, blk0 + j) for j in range(4)]
    in_specs += [tok(GATE_PAD, 0), vec_r, vec_r,
                 pl.BlockSpec((nb, 1, PAIR_ROWS, GATE_PAD), lambda b, c: (b, c, 0, 0)), tab, tab,
                 pl.BlockSpec((CONV_W, 3 * G_WIDTH), lambda b, c: (0, 0)),
                 pl.BlockSpec((1, G_WIDTH), lambda b, c: (0, 0))]
    args = [proj, proj, proj, proj, gates, gb_row, al_row,
            gates_t.reshape(batch, nc, GATE_PAD // 2, GATE_PAD), pair(gb_row), pair(al_row),
            conv_w, norm_g.reshape(1, G_WIDTH)]
    s_spec = pl.BlockSpec((nb, G_HEADS, G_HEAD_DIM, G_HEAD_DIM), lambda b, c: (b, 0, 0, 0))
    conv_spec = pl.BlockSpec((nb, CONV_HIST, 3 * G_WIDTH), lambda b, c: (b, 0, 0))
    outs = pl.pallas_call(
        functools.partial(_gdn_paired_kernel, L=L, nb=nb, nc=nc, cast_periods=cast_periods),
        grid=(batch // nb, nc),
        in_specs=in_specs + cast_specs,
        out_specs=[tok(G_WIDTH, 0), s_spec, conv_spec] + cast_specs,
        out_shape=[jax.ShapeDtypeStruct(proj.shape[:-1] + (G_WIDTH,), _scan_out_dtype(L)),
                   jax.ShapeDtypeStruct((batch, G_HEADS, G_HEAD_DIM, G_HEAD_DIM), F32),
                   jax.ShapeDtypeStruct((batch, CONV_HIST, 3 * G_WIDTH), F32)]
        + [jax.ShapeDtypeStruct(w.shape, BF16) for w in side_cast],
        scratch_shapes=[pltpu.VMEM((nb, SUBLANES + L, 3 * G_WIDTH), F32)],
        compiler_params=_cparams(("arbitrary", "arbitrary")),
        name="gdn",
    )(*args, *side_cast)
    return outs[:3], outs[3:]


def _gdn(proj, gates, gates_t, gvecs, conv_w, norm_g, batch, nc, L, nb, init, side_cast=()):
    if init is None and L > SMALL_L:
        return _gdn_paired(proj, gates, gates_t, gvecs, conv_w, norm_g, batch, nc, L, nb, side_cast)
    assert L <= SMALL_L and not side_cast
    has_init = init is not None
    blk0 = 4 * M_WIDTH // G_WIDTH
    in_specs = [_tok_spec(L, nb, nc, G_WIDTH, blk0 + j) for j in range(4)] + _gate_specs(L, nb, nc)
    in_specs += [pl.BlockSpec((CONV_W, 3 * G_WIDTH), lambda b, c: (0, 0)),
                 pl.BlockSpec((1, G_WIDTH), lambda b, c: (0, 0))]
    s_spec = pl.BlockSpec((nb, G_HEADS, G_HEAD_DIM, G_HEAD_DIM), lambda b, c: (b, 0, 0, 0))
    conv_spec = pl.BlockSpec((nb, CONV_HIST, 3 * G_WIDTH), lambda b, c: (b, 0, 0))
    args = [proj, proj, proj, proj, gates, gates_t, *gvecs, conv_w, norm_g.reshape(1, G_WIDTH)]
    if has_init:
        in_specs += [s_spec, conv_spec]
        args += list(init)
    outs = pl.pallas_call(
        functools.partial(_gdn_kernel, L=L, nb=nb, nc=nc, has_init=has_init),
        grid=(batch // nb, nc),
        in_specs=in_specs,
        out_specs=[_tok_spec(L, nb, nc, G_WIDTH, 0), s_spec, conv_spec],
        out_shape=[jax.ShapeDtypeStruct(proj.shape[:-1] + (G_WIDTH,), _scan_out_dtype(L)),
                   jax.ShapeDtypeStruct((batch, G_HEADS, G_HEAD_DIM, G_HEAD_DIM), F32),
                   jax.ShapeDtypeStruct((batch, CONV_HIST, 3 * G_WIDTH), F32)],
        scratch_shapes=[pltpu.VMEM((nb, 2 * SUBLANES, 3 * G_WIDTH), F32)],
        compiler_params=_cparams(("arbitrary", "arbitrary")),
        name="gdn",
    )(*args)
    return outs, []


def _layer_norm(y, g, b):
    mu = jnp.mean(y, axis=-1, keepdims=True)
    yc = y - mu
    var = jnp.mean(yc * yc, axis=-1, keepdims=True)
    return yc * lax.rsqrt(var + LN_EPS) * g + b


def _outproj_kernel(hm_ref, hg_ref, x_ref, gt_ref, w_ref, g_ref, b_ref, o_ref):
    tm = x_ref.shape[0]
    half = tm // OUTPROJ_SLABS
    for r in range(OUTPROJ_SLABS):
        rows = slice(r * half, (r + 1) * half)
        mix = (_dot(hm_ref[rows, :].astype(BF16), w_ref[0:M_WIDTH, :])
               + _dot(hg_ref[rows, :].astype(BF16), w_ref[M_WIDTH:M_WIDTH + G_WIDTH, :]))
        gt = gt_ref[0] if gt_ref.shape[1] == 1 else gt_ref[0, rows, :]
        y = DEEPNORM_ALPHA * x_ref[rows, :] + (1.0 + gt) * mix
        o_ref[rows, :] = _layer_norm(y, g_ref[...], b_ref[...])


def _outproj(hm, hg, x2d, gt, w_out, ln_g, ln_b, tm, tiles_per_mod):
    m, d = x2d.shape
    vec = pl.BlockSpec((1, d), lambda i: (0, 0))
    return pl.pallas_call(
        _outproj_kernel,
        grid=(m // tm,),
        in_specs=[pl.BlockSpec((tm, M_WIDTH), lambda i: (i, 0)),
                  pl.BlockSpec((tm, G_WIDTH), lambda i: (i, 0)),
                  pl.BlockSpec((tm, d), lambda i: (i, 0)),
                  _mod_spec(gt, tiles_per_mod, 1),
                  pl.BlockSpec((d, d), lambda i: (0, 0)),
                  vec, vec],
        out_specs=pl.BlockSpec((tm, d), lambda i: (i, 0)),
        out_shape=jax.ShapeDtypeStruct((m, d), F32),
        compiler_params=_cparams(("arbitrary",)),
        name="outproj",
    )(hm, hg, x2d, gt, w_out, ln_g.reshape(1, d), ln_b.reshape(1, d))


def _ffn_kernel(x_ref, sc_ref, sh_ref, gt_ref, wg_ref, wu_ref, wd_ref, g_ref, b_ref, o_ref, h_scr, acc):
    f = pl.program_id(1)

    @pl.when(f == 0)
    def _():
        h_scr[...] = (x_ref[...] * (1.0 + sc_ref[0]) + sh_ref[0]).astype(BF16)
        acc[...] = jnp.zeros_like(acc)

    h = h_scr[...]
    gate = _dot(h, wg_ref[...])
    up = _dot(h, wu_ref[...])
    act = (gate * jax.nn.sigmoid(gate) * up).astype(BF16)
    acc[...] += _dot(act, wd_ref[...])

    @pl.when(f == pl.num_programs(1) - 1)
    def _():
        y = DEEPNORM_ALPHA * x_ref[...] + (1.0 + gt_ref[0]) * acc[...]
        o_ref[...] = _layer_norm(y, g_ref[...], b_ref[...])


def _ffn(x2d, sc, sh, gt, w_gu, w_down, ln_g, ln_b, tm, tiles_per_mod):
    m, d = x2d.shape
    tf = TF_FFN
    nf = D_FF // tf
    mod_spec = _mod_spec(sc, tiles_per_mod, 2)
    vec = pl.BlockSpec((1, d), lambda i, f: (0, 0))
    return pl.pallas_call(
        _ffn_kernel,
        grid=(m // tm, nf),
        in_specs=[pl.BlockSpec((tm, d), lambda i, f: (i, 0)),
                  mod_spec, mod_spec, mod_spec,
                  pl.BlockSpec((d, tf), lambda i, f: (0, f)),
                  pl.BlockSpec((d, tf), lambda i, f: (0, nf + f)),
                  pl.BlockSpec((tf, d), lambda i, f: (f, 0)),
                  vec, vec],
        out_specs=pl.BlockSpec((tm, d), lambda i, f: (i, 0)),
        out_shape=jax.ShapeDtypeStruct((m, d), F32),
        scratch_shapes=[pltpu.VMEM((tm, d), BF16), pltpu.VMEM((tm, d), F32)],
        compiler_params=_cparams(("arbitrary", "arbitrary")),
        name="ffn",
    )(x2d, sc, sh, gt, w_gu, w_gu, w_down, ln_g.reshape(1, d), ln_b.reshape(1, d))


def _layer(x, ada6, weights, init_m, init_g, nb):
    batch, seq, d = x.shape
    L = CHUNK if seq % CHUNK == 0 else seq
    nc = seq // L
    m = batch * seq
    x2d = x.reshape(m, d)

    def tiling(tm_cap):
        tm = min(tm_cap, m)
        if seq % tm == 0:
            return tm, seq // tm, lambda a: a.reshape(batch, 1, d)
        return tm, 1, lambda a: jnp.repeat(a, seq, axis=0).reshape(m // tm, tm, d)

    sh1, sc1, gt1, sh2, sc2, gt2 = ada6
    tm, tpm, mod = tiling(TM_INPROJ)
    proj, gates = _inproj(x2d, mod(sc1), mod(sh1), weights["w_in_main"], weights["w_in_gate"], tm, tpm)
    gates_t = jnp.swapaxes(gates.reshape(batch, nc, L, GATE_PAD), 2, 3)
    proj_v, gates_v = _tok_view(proj, batch, nc), _tok_view(gates, batch, nc)
    gvecs = weights["gvecs"]
    hm, c1, n1, m1 = _mlstm(proj_v, gates_v, gates_t, gvecs, weights["m_norm_g"], batch, nc, L, nb["mlstm"],
                            init_m)
    pending = [name for name in ("w_out", "w_gu", "w_down") if weights[name].dtype != BF16]
    (hg, s1, conv1), converted = _gdn(proj_v, gates_v, gates_t, gvecs, weights["conv_w"], weights["g_norm_g"],
                                      batch, nc, L, nb["gdn"], init_g, [weights[name] for name in pending])
    weights.update(zip(pending, converted))
    tm, tpm, mod = tiling(TM_OUTPROJ)
    x1 = _outproj(hm.reshape(m, M_WIDTH), hg.reshape(m, G_WIDTH), x2d, mod(gt1), weights["w_out"],
                  weights["ln1_g"], weights["ln1_b"], tm, tpm)
    tm, tpm, mod = tiling(TM_FFN)
    y = _ffn(x1, mod(sc2), mod(sh2), mod(gt2), weights["w_gu"], weights["w_down"], weights["ln2_g"],
             weights["ln2_b"], tm, tpm)
    return y.reshape(batch, seq, d), c1, n1, m1[:, 0, :M_HEADS], s1, conv1


def kernel(x_prompt, x_sample, state_mlstm_C, state_mlstm_n, state_mlstm_m, state_gdn_S, state_gdn_conv,
           c_prompt, c_sample, w_ada, b_ada, w_in, m_i_bias, m_f_bias, m_norm_g, conv_w, g_dt_bias,
           g_A_log, g_norm_g, w_out, ln1_g, ln1_b, w_gu, w_down, ln2_g, ln2_b):
    bp, seq_p, d = x_prompt.shape
    bs, seq_s, _ = x_sample.shape

    ada = _ada(jnp.concatenate([c_prompt, c_sample], axis=0), w_ada, b_ada)
    ada6 = [ada[:, i * d:(i + 1) * d] for i in range(6)]

    gate_bias = jnp.zeros((GATE_PAD,), F32)
    gate_bias = gate_bias.at[GI0:GI0 + M_HEADS].set(m_i_bias).at[GF0:GF0 + M_HEADS].set(m_f_bias)
    gate_bias = gate_bias.at[GA0:GA0 + G_HEADS].set(g_dt_bias)
    a_log = jnp.zeros((GATE_PAD,), F32).at[GA0:GA0 + G_HEADS].set(g_A_log)
    weights = {
        "w_in_main": _cast_bf16(w_in.T, MAIN_COLS, d),
        "w_in_gate": jnp.pad(w_in.T[MAIN_COLS:], ((0, GATE_PAD - GATE_COLS), (0, 0))).astype(BF16),
        "gvecs": (gate_bias.reshape(1, GATE_PAD), gate_bias.reshape(GATE_PAD, 1),
                  a_log.reshape(1, GATE_PAD), a_log.reshape(GATE_PAD, 1)),
        "m_norm_g": m_norm_g, "conv_w": conv_w, "g_norm_g": g_norm_g,
        "w_out": w_out, "w_gu": w_gu, "w_down": w_down,
        "ln1_g": ln1_g, "ln1_b": ln1_b, "ln2_g": ln2_g, "ln2_b": ln2_b,
    }

    y_p, p_c, p_n, p_m, p_s, p_conv = _layer(x_prompt, [a[:bp] for a in ada6], weights, None, None,
                                             SEQS_PER_STEP_PROMPT)
    m0 = jnp.pad(state_mlstm_m, ((0, 0), (0, GATE_PAD - M_HEADS))).reshape(bs, 1, GATE_PAD)
    y_s, s_c, s_n, s_m, s_s, s_conv = _layer(
        x_sample, [a[bp:] for a in ada6], weights, (state_mlstm_C, state_mlstm_n, m0),
        (state_gdn_S, state_gdn_conv), SEQS_PER_STEP_SAMPLE)
    return (y_p, y_s, p_c, p_n, p_m, p_s, p_conv, s_c, s_n, s_m, s_s, s_conv)
```

```python
import functools

import jax
import jax.numpy as jnp
from jax import lax
from jax.experimental import pallas as pl
from jax.experimental.pallas import tpu as pltpu

F32 = jnp.float32
BF16 = jnp.bfloat16

D_MODEL = 2048
M_HEADS = 4
M_HEAD_DIM = 256
M_WIDTH = M_HEADS * M_HEAD_DIM
G_HEADS = 8
G_HEAD_DIM = 128
G_WIDTH = G_HEADS * G_HEAD_DIM
CONV_W = 4
CHUNK = 64
D_FF = 5632
MAIN_COLS = 4 * M_WIDTH + 3 * G_WIDTH + G_WIDTH
GATE_COLS = 2 * M_HEADS + 2 * G_HEADS
GATE_PAD = 128
DEEPNORM_ALPHA = 2.0 ** 0.25
LN_EPS = 1e-5
RMS_EPS = 1e-6
GI0, GF0, GB0, GA0 = 0, M_HEADS, 2 * M_HEADS, 2 * M_HEADS + G_HEADS

SUBLANES = 8
VMEM_LIMIT_BYTES = 56 * 1024 * 1024
TM_INPROJ, TM_OUTPROJ, TM_FFN = 1024, 512, 512
TN_INPROJ, TN_ADA, TF_FFN = 1024, 1024, 512
OUTPROJ_SLABS = 2
CAST_ROWS, CAST_COLS = 2048, 1024
SEQS_PER_STEP_PROMPT = {"mlstm": 1, "gdn": 2}
SEQS_PER_STEP_SAMPLE = {"mlstm": 4, "gdn": 8}
MLSTM_GROUP_LONG, MLSTM_GROUP_SHORT = 1, 2
SMALL_L = 8
PAIR_ROWS = 16


def _cparams(sem):
    return pltpu.CompilerParams(dimension_semantics=sem, vmem_limit_bytes=VMEM_LIMIT_BYTES)


def _dot(a, b):
    return jnp.dot(a, b, preferred_element_type=F32)


def _dot_nt(a, b):
    return lax.dot_general(a, b, (((1,), (1,)), ((), ())), preferred_element_type=F32)


def _dot_tn(a, b):
    return lax.dot_general(a, b, (((0,), (0,)), ((), ())), preferred_element_type=F32)


def _split2(x):
    hi = x.astype(BF16)
    return hi, (x - hi.astype(F32)).astype(BF16)


def _split3(x):
    hi = x.astype(BF16)
    r = x - hi.astype(F32)
    mid = r.astype(BF16)
    return hi, mid, (r - mid.astype(F32)).astype(BF16)


def _dot3_stacked(a, b):
    m = a[0].shape[0]
    both = _dot(jnp.concatenate([a[0], a[1]], axis=0), b[0])
    return both[0:m] + (_dot(a[0], b[1]) + both[m:2 * m])


def _mm_small(a, b):
    out = a[:, 0:1] * b[0:1, :]
    for i in range(1, a.shape[1]):
        out = out + a[:, i:i + 1] * b[i:i + 1, :]
    return out


def _tok_get(ref, b, L, cols):
    if len(ref.shape) == 3:
        return ref[b, :, cols]
    return ref[b * L:(b + 1) * L, cols]


def _tok_set(ref, b, L, cols, val):
    if len(ref.shape) == 3:
        ref[b, :, cols] = val
    else:
        ref[b * L:(b + 1) * L, cols] = val


ALL = slice(None)


def _cast_kernel(x_ref, o_ref):
    o_ref[...] = x_ref[...].astype(BF16)


def _cast_bf16(w, rows, cols):
    bc = min(cols, CAST_COLS)
    br = max(r for r in range(SUBLANES, min(rows, CAST_ROWS) + 1, SUBLANES) if rows % r == 0)
    return pl.pallas_call(
        _cast_kernel,
        grid=(rows // br, cols // bc),
        in_specs=[pl.BlockSpec((br, bc), lambda i, j: (i, j))],
        out_specs=pl.BlockSpec((br, bc), lambda i, j: (i, j)),
        out_shape=jax.ShapeDtypeStruct((rows, cols), BF16),
        compiler_params=_cparams(("arbitrary", "arbitrary")),
        name="cast",
    )(w)


def _ada_kernel(c_ref, w_ref, b_ref, o_ref):
    c = c_ref[...]
    a = (c * jax.nn.sigmoid(c)).astype(BF16)
    o_ref[...] = _dot(a, w_ref[...].astype(BF16)) + b_ref[...]


def _ada(c_all, w_ada, b_ada):
    n_rows, d = c_all.shape
    n_cols = w_ada.shape[1]
    tn = TN_ADA
    return pl.pallas_call(
        _ada_kernel,
        grid=(n_cols // tn,),
        in_specs=[pl.BlockSpec((n_rows, d), lambda j: (0, 0)),
                  pl.BlockSpec((d, tn), lambda j: (0, j)),
                  pl.BlockSpec((1, tn), lambda j: (0, j))],
        out_specs=pl.BlockSpec((n_rows, tn), lambda j: (0, j)),
        out_shape=jax.ShapeDtypeStruct((n_rows, n_cols), F32),
        compiler_params=_cparams(("arbitrary",)),
        name="ada",
    )(c_all, w_ada, b_ada.reshape(1, n_cols))


def _inproj_kernel(x_ref, sc_ref, sh_ref, wt_ref, wgt_ref, o_ref, og_ref, h_scr):
    @pl.when(pl.program_id(1) == 0)
    def _():
        h = (x_ref[...] * (1.0 + sc_ref[0]) + sh_ref[0]).astype(BF16)
        h_scr[...] = h
        og_ref[...] = _dot_nt(h, wgt_ref[...])

    o_ref[...] = _dot_nt(h_scr[...], wt_ref[...])


def _mod_spec(mod, tiles_per_mod, grid_rank):
    _, mod_rows, d = mod.shape
    if grid_rank == 1:
        index_map = lambda i: (i // tiles_per_mod, 0, 0)
    else:
        index_map = lambda i, j: (i // tiles_per_mod, 0, 0)
    if mod_rows == 1:
        return pl.BlockSpec((1, 1, d), index_map)
    return pl.BlockSpec((1, mod_rows, d), index_map, pipeline_mode=pl.Buffered(1))


def _inproj(x2d, sc, sh, w_main, w_gate, tm, tiles_per_mod):
    m, d = x2d.shape
    tn = TN_INPROJ
    mod_spec = _mod_spec(sc, tiles_per_mod, 2)
    return pl.pallas_call(
        _inproj_kernel,
        grid=(m // tm, MAIN_COLS // tn),
        in_specs=[pl.BlockSpec((tm, d), lambda i, j: (i, 0)),
                  mod_spec, mod_spec,
                  pl.BlockSpec((tn, d), lambda i, j: (j, 0)),
                  pl.BlockSpec((GATE_PAD, d), lambda i, j: (0, 0))],
        out_specs=[pl.BlockSpec((tm, tn), lambda i, j: (i, j)),
                   pl.BlockSpec((tm, GATE_PAD), lambda i, j: (i, 0))],
        out_shape=[jax.ShapeDtypeStruct((m, MAIN_COLS), F32),
                   jax.ShapeDtypeStruct((m, GATE_PAD), F32)],
        scratch_shapes=[pltpu.VMEM((tm, d), BF16)],
        compiler_params=_cparams(("arbitrary", "arbitrary")),
        name="inproj",
    )(x2d, sc, sh, w_main, w_gate)


def _gate_tables(g, bias, alog, gid):
    x = g + bias
    is_f = (gid >= GF0) & (gid < GB0)
    is_a = (gid >= GA0) & (gid < GA0 + G_HEADS)
    log_f = jax.nn.log_sigmoid(x)
    log_a = -jnp.exp(alog) * jax.nn.softplus(x)
    inc = jnp.where(is_f, log_f, jnp.where(is_a, log_a, 0.0))
    return x, inc, jax.nn.sigmoid(x)


def _chunk_masks(L):
    row = lax.broadcasted_iota(jnp.int32, (L, L), 0)
    col = lax.broadcasted_iota(jnp.int32, (L, L), 1)
    return row >= col, row > col, row == col


def _gates_col_form(g_col, gb_row_ref, al_row_ref, L):
    causal, _, _ = _chunk_masks(L)
    tril = causal.astype(F32)
    gid_c = lax.broadcasted_iota(jnp.int32, (L, GATE_PAD), 1)
    x_c, inc_c, beta_c = _gate_tables(g_col, gb_row_ref[...], al_row_ref[...], gid_c)
    if L <= SMALL_L:
        cum_c = _mm_small(tril, inc_c)
    else:
        tril_b = tril.astype(BF16)
        c1, c2, c3 = _split3(inc_c)
        cum_c = _dot(tril_b, c1) + (_dot(tril_b, c2) + _dot(tril_b, c3))
    return x_c, cum_c, beta_c


def _gates_both_forms(g_col, g_row, gb_row_ref, gb_col_ref, al_row_ref, al_col_ref, L):
    x_c, cum_c, beta_c = _gates_col_form(g_col, gb_row_ref, al_row_ref, L)
    triu = (lax.broadcasted_iota(jnp.int32, (L, L), 0) <= lax.broadcasted_iota(jnp.int32, (L, L), 1)).astype(F32)
    gid_r = lax.broadcasted_iota(jnp.int32, (GATE_PAD, L), 0)
    x_r, inc_r, _ = _gate_tables(g_row, gb_col_ref[...], al_col_ref[...], gid_r)
    if L <= SMALL_L:
        cum_r = _mm_small(inc_r, triu)
    else:
        triu_b = triu.astype(BF16)
        r1, r2, r3 = _split3(inc_r)
        cum_r = _dot(r1, triu_b) + (_dot(r2, triu_b) + _dot(r3, triu_b))
    return x_c, cum_c, beta_c, x_r, cum_r


def _rms_gate(h, gain, gate):
    return h * lax.rsqrt(jnp.mean(h * h, axis=-1, keepdims=True) + RMS_EPS) * gain * gate


def _mlstm_kernel(*refs, L, nb, nc, has_init, group):
    if has_init:
        (q_ref, k_ref, v_ref, o_ref, gcol_ref, grow_ref, gb_row_ref, gb_col_ref, al_row_ref, al_col_ref,
         ng_ref, c0_ref, n0_ref, m0_ref, h_ref, c_ref, n_ref, m_ref) = refs
    else:
        (q_ref, k_ref, v_ref, o_ref, gcol_ref, grow_ref, gb_row_ref, gb_col_ref, al_row_ref, al_col_ref,
         ng_ref, h_ref, c_ref, n_ref, m_ref) = refs

    if has_init and nc == 1:
        c_src, n_src, m_src = c0_ref, n0_ref, m0_ref
    else:
        c_src, n_src, m_src = c_ref, n_ref, m_ref

        @pl.when(pl.program_id(1) == 0)
        def _():
            if has_init:
                c_ref[...] = c0_ref[...]
                n_ref[...] = n0_ref[...]
                m_ref[...] = m0_ref[...]
            else:
                c_ref[...] = jnp.zeros_like(c_ref)
                n_ref[...] = jnp.zeros_like(n_ref)
                m_ref[...] = jnp.zeros_like(m_ref)

    for b0 in range(0, nb, group):
        _mlstm_group(range(b0, min(b0 + group, nb)), L, q_ref, k_ref, v_ref, o_ref, gcol_ref, grow_ref, gb_row_ref,
                     gb_col_ref, al_row_ref, al_col_ref, ng_ref, c_src, n_src, m_src, h_ref, c_ref, n_ref, m_ref)


def _mlstm_group(seqs, L, q_ref, k_ref, v_ref, o_ref, gcol_ref, grow_ref, gb_row_ref, gb_col_ref, al_row_ref,
                 al_col_ref, ng_ref, c_src, n_src, m_src, h_ref, c_ref, n_ref, m_ref):
    causal, _, _ = _chunk_masks(L)
    lane = lax.broadcasted_iota(jnp.int32, (1, GATE_PAD), 1)
    scale = M_HEAD_DIM ** -0.5
    gates = {b: _gates_both_forms(_tok_get(gcol_ref, b, L, ALL), grow_ref[b, 0], gb_row_ref, gb_col_ref,
                                  al_row_ref, al_col_ref, L) for b in seqs}
    m_all = {b: m_src[b] for b in seqs}
    probs = [(b, h) for b in seqs for h in range(M_HEADS)]

    def cols(h):
        return slice(h * M_HEAD_DIM, (h + 1) * M_HEAD_DIM)

    q = [_tok_get(q_ref, b, L, cols(h)) for b, h in probs]
    k = [_tok_get(k_ref, b, L, cols(h)) * scale for b, h in probs]
    vb = [_tok_get(v_ref, b, L, cols(h)).astype(BF16) for b, h in probs]
    qb = [x.astype(BF16) for x in q]
    kb = [x.astype(BF16) for x in k]
    ig_c = [gates[b][0][:, GI0 + h:GI0 + h + 1] for b, h in probs]
    ig_r = [gates[b][3][GI0 + h:GI0 + h + 1, :] for b, h in probs]
    bt_c = [gates[b][1][:, GF0 + h:GF0 + h + 1] for b, h in probs]
    bt_r = [gates[b][4][GF0 + h:GF0 + h + 1, :] for b, h in probs]
    m0 = [jnp.sum(jnp.where(lane == h, m_all[b], 0.0), axis=1, keepdims=True) for b, h in probs]
    n_p = len(probs)
    rng = range(n_p)

    log_d = [jnp.where(causal, bt_c[i] - bt_r[i] + ig_r[i], -jnp.inf) for i in rng]
    inter = [bt_c[i] + m0[i] for i in rng]
    m_t = [jnp.maximum(inter[i], jnp.max(log_d[i], axis=1, keepdims=True)) for i in rng]
    inter_w = [jnp.exp(inter[i] - m_t[i]) for i in rng]
    qk = [_dot_nt(qb[i], kb[i]) for i in rng]
    c_old = [c_src[b, h] for b, h in probs]
    n_old = [n_src[b, h:h + 1, :] for b, h in probs]
    qc = [_dot(qb[i], c_old[i].astype(BF16)) for i in rng]
    s = [qk[i] * jnp.exp(log_d[i] - m_t[i]) for i in rng]
    sv = [_dot(s[i].astype(BF16), vb[i]) for i in rng]

    b_last = [bt_c[i][L - 1:L, :] for i in rng]
    m_new = [jnp.maximum(b_last[i] + m0[i], jnp.max(b_last[i] - bt_r[i] + ig_r[i], axis=1, keepdims=True))
             for i in rng]
    kw = [k[i] * jnp.exp(b_last[i] - bt_c[i] + ig_c[i] - m_new[i]) for i in rng]
    decay = [jnp.exp(b_last[i] + m0[i] - m_new[i]) for i in rng]
    kv = [_dot_tn(kw[i].astype(BF16), vb[i]) for i in rng]

    m_next = dict(m_all)
    for i, (b, h) in enumerate(probs):
        num = inter_w[i] * qc[i] + sv[i]
        den = inter_w[i] * jnp.sum(q[i] * n_old[i], axis=1, keepdims=True) + jnp.sum(s[i], axis=1, keepdims=True)
        hh = num / jnp.maximum(jnp.abs(den), jnp.exp(-m_t[i]))
        c_ref[b, h] = decay[i] * c_old[i] + kv[i]
        n_ref[b, h:h + 1, :] = decay[i] * n_old[i] + jnp.sum(kw[i], axis=0, keepdims=True)
        m_next[b] = jnp.where(lane == h, m_new[i], m_next[b])
        gate = jax.nn.sigmoid(_tok_get(o_ref, b, L, cols(h)))
        _tok_set(h_ref, b, L, cols(h), _rms_gate(hh, ng_ref[:, cols(h)], gate).astype(h_ref.dtype))
    for b in seqs:
        m_ref[b] = m_next[b]


def _tok_spec(L, nb, nc, width, col_block):
    if nc == 1:
        return pl.BlockSpec((nb * L, width), lambda b, c: (b, col_block))
    return pl.BlockSpec((nb, L, width), lambda b, c: (b, c, col_block))


def _tok_view(x2d, batch, nc):
    return x2d if nc == 1 else x2d.reshape(batch, x2d.shape[0] // batch, x2d.shape[1])


def _gate_specs(L, nb, nc):
    vec_r = pl.BlockSpec((1, GATE_PAD), lambda b, c: (0, 0))
    vec_c = pl.BlockSpec((GATE_PAD, 1), lambda b, c: (0, 0))
    return [_tok_spec(L, nb, nc, GATE_PAD, 0),
            pl.BlockSpec((nb, 1, GATE_PAD, L), lambda b, c: (b, c, 0, 0)),
            vec_r, vec_c, vec_r, vec_c]


def _scan_out_dtype(L):
    return BF16 if L % 16 == 0 else F32


BF16_ROWS = 16


def _slab_plan(rows, n_steps):
    period = 1
    while (rows * period) % (n_steps * BF16_ROWS):
        period *= 2
        assert period <= n_steps
    return rows * period // n_steps, period


def _side_cast_specs(side_cast, n_steps, nc):
    specs, periods = [], []
    for w in side_cast:
        slab, period = _slab_plan(w.shape[0], n_steps)
        specs.append(pl.BlockSpec((slab, w.shape[1]), lambda b, c, period=period: ((b * nc + c) // period, 0)))
        periods.append(period)
    return specs, tuple(periods)


def _run_side_casts(cast_src, cast_dst, periods, nc):
    step = pl.program_id(0) * nc + pl.program_id(1)
    for src, dst, period in zip(cast_src, cast_dst, periods):
        if period == 1:
            dst[...] = src[...].astype(BF16)
        else:
            @pl.when(step % period == 0)
            def _(src=src, dst=dst):
                dst[...] = src[...].astype(BF16)


def _mlstm(proj, gates, gates_t, gvecs, norm_g, batch, nc, L, nb, init):
    has_init = init is not None
    in_specs = [_tok_spec(L, nb, nc, M_WIDTH, j) for j in range(4)] + _gate_specs(L, nb, nc)
    in_specs.append(pl.BlockSpec((1, M_WIDTH), lambda b, c: (0, 0)))
    c_spec = pl.BlockSpec((nb, M_HEADS, M_HEAD_DIM, M_HEAD_DIM), lambda b, c: (b, 0, 0, 0))
    n_spec = pl.BlockSpec((nb, M_HEADS, M_HEAD_DIM), lambda b, c: (b, 0, 0))
    m_spec = pl.BlockSpec((nb, 1, GATE_PAD), lambda b, c: (b, 0, 0))
    args = [proj, proj, proj, proj, gates, gates_t, *gvecs, norm_g.reshape(1, M_WIDTH)]
    if has_init:
        in_specs += [c_spec, n_spec, m_spec]
        args += list(init)
    return pl.pallas_call(
        functools.partial(_mlstm_kernel, L=L, nb=nb, nc=nc, has_init=has_init,
                          group=MLSTM_GROUP_SHORT if L <= SMALL_L else MLSTM_GROUP_LONG),
        grid=(batch // nb, nc),
        in_specs=in_specs,
        out_specs=[_tok_spec(L, nb, nc, M_WIDTH, 0), c_spec, n_spec, m_spec],
        out_shape=[jax.ShapeDtypeStruct(proj.shape[:-1] + (M_WIDTH,), _scan_out_dtype(L)),
                   jax.ShapeDtypeStruct((batch, M_HEADS, M_HEAD_DIM, M_HEAD_DIM), F32),
                   jax.ShapeDtypeStruct((batch, M_HEADS, M_HEAD_DIM), F32),
                   jax.ShapeDtypeStruct((batch, 1, GATE_PAD), F32)],
        compiler_params=_cparams(("arbitrary", "arbitrary")),
        name="mlstm",
    )(*args)


def _inv_unit_lower_small(a_list, L):
    _, _, eye = _chunk_masks(L)
    p = [-a for a in a_list]
    t = [eye.astype(F32) + x for x in p]
    for _ in range(max(1, (L - 1).bit_length()) - 1):
        p = [_mm_small(x, x) for x in p]
        t = [ti + _mm_small(ti, pi) for ti, pi in zip(t, p)]
    return t


def _gate_rows_paired(gp, bias_p, alog_p, L):
    rid = lax.broadcasted_iota(jnp.int32, gp.shape, 0)
    x = gp + bias_p
    is_f = (rid >= GF0 // 2) & (rid < GB0 // 2)
    is_a = (rid >= GA0 // 2) & (rid < (GA0 + G_HEADS) // 2)
    inc = jnp.where(is_f, jax.nn.log_sigmoid(x), jnp.where(is_a, -jnp.exp(alog_p) * jax.nn.softplus(x), 0.0))
    r = lax.broadcasted_iota(jnp.int32, (2 * L, 2 * L), 0)
    c = lax.broadcasted_iota(jnp.int32, (2 * L, 2 * L), 1)
    triu2 = (((r < L) == (c < L)) & (r <= c)).astype(BF16)
    p1, p2, p3 = _split3(inc)
    return _dot(p1, triu2) + (_dot(p2, triu2) + _dot(p3, triu2))


def _gdn_heads_paired(qn, kn, vv, gates_c, cum_p, s_ref, z_ref, h_ref, ng_ref, L, nb):
    dh = G_HEAD_DIM
    row =lax.broadcasted_iota(jnp.int32, (L, 2 * L), 0)
    lane = lax.broadcasted_iota(jnp.int32, (L, 2 * L), 1)
    left = lane < L
    colp = jnp.where(left, lane, lane - L)
    causal_p, strict_p, eye_p = row >= colp, row > colp, (row == colp).astype(F32)
    left2 = lax.broadcasted_iota(jnp.int32, (L, 2 * dh), 1) < dh
    lane4 = lax.broadcasted_iota(jnp.int32, (L, 4 * dh), 1)
    first4 = (lane4 // dh) % 2 == 0

    def bdiag(y, first):
        z = jnp.zeros_like(y)
        return jnp.concatenate([jnp.where(first, y, z), jnp.where(first, z, y)], axis=0)

    pairs = [(b, p) for b in range(nb) for p in range(G_HEADS // 2)]
    rng = range(len(pairs))
    q2 = [qn[b][:, 2 * p * dh:(2 * p + 2) * dh] for b, p in pairs]
    k2 = [kn[b][:, 2 * p * dh:(2 * p + 2) * dh] for b, p in pairs]
    v2 = [vv[b][:, 2 * p * dh:(2 * p + 2) * dh] for b, p in pairs]
    q2b = [x.astype(BF16) for x in q2]
    k2b = [x.astype(BF16) for x in k2]
    kbd = [bdiag(x, left2) for x in k2b]
    b_ca = [gates_c[b][1][:, GA0 + 2 * p:GA0 + 2 * p + 1] for b, p in pairs]
    b_cb = [gates_c[b][1][:, GA0 + 2 * p + 1:GA0 + 2 * p + 2] for b, p in pairs]
    bet_a = [gates_c[b][2][:, GB0 + 2 * p:GB0 + 2 * p + 1] for b, p in pairs]
    bet_b = [gates_c[b][2][:, GB0 + 2 * p + 1:GB0 + 2 * p + 2] for b, p in pairs]
    b_r = [cum_p[b][GA0 // 2 + p:GA0 // 2 + p + 1, :] for b, p in pairs]
    decay = [jnp.exp(jnp.where(causal_p, jnp.where(left, b_ca[i], b_cb[i]) - b_r[i], -jnp.inf)) for i in rng]
    kk = [_dot_nt(k2b[i], kbd[i]) for i in rng]
    qk = [(_dot_nt(q2b[i], kbd[i]) * decay[i]).astype(BF16) for i in rng]
    s_old = [(s_ref[b, 2 * p], s_ref[b, 2 * p + 1]) for b, p in pairs]
    zero = jnp.zeros((dh, dh), BF16)
    sbd = [jnp.concatenate([jnp.concatenate([sa.astype(BF16), zero], axis=1),
                            jnp.concatenate([zero, sb.astype(BF16)], axis=1)], axis=0) for sa, sb in s_old]
    qs = [_dot(q2b[i], sbd[i]) for i in rng]

    a = [jnp.where(strict_p, jnp.where(left, bet_a[i], bet_b[i]) * kk[i] * decay[i], 0.0) for i in rng]
    p = [-x for x in a]
    t = [eye_p + x for x in p]
    ps = [_split2(x) for x in p]
    p = [_dot3_stacked(ps[i], (bdiag(ps[i][0], left), bdiag(ps[i][1], left))) for i in rng]
    for lvl in range(1, (L - 1).bit_length()):
        last = lvl == (L - 1).bit_length() - 1
        ps = [_split2(x) for x in p]
        pbd = [(bdiag(hi, left), bdiag(lo, left)) for hi, lo in ps]
        if last:
            t = [t[i] + _dot3_stacked(_split2(t[i]), pbd[i]) for i in rng]
        else:
            ts = [_split2(x) for x in t]
            both = [_dot3_stacked((jnp.concatenate([ts[i][0], ps[i][0]], axis=0),
                                   jnp.concatenate([ts[i][1], ps[i][1]], axis=0)), pbd[i]) for i in rng]
            t = [t[i] + both[i][0:L] for i in rng]
            p = [both[i][L:2 * L] for i in rng]

    bet2 = [jnp.where(left2, bet_a[i], bet_b[i]) for i in rng]
    eb2 = [jnp.where(left2, jnp.exp(b_ca[i]), jnp.exp(b_cb[i])) for i in rng]
    rhs = [_split2(jnp.concatenate([bet2[i] * v2[i], (bet2[i] * eb2[i]) * k2[i]], axis=-1)) for i in rng]
    sol = [_dot3_stacked(_split2(t[i]), (bdiag(rhs[i][0], first4), bdiag(rhs[i][1], first4))) for i in rng]
    u = [sol[i][:, 0:2 * dh] - _dot(sol[i][:, 2 * dh:4 * dh].astype(BF16), sbd[i]) for i in rng]
    ub = [x.astype(BF16) for x in u]
    o = [eb2[i] * qs[i] + _dot(qk[i], bdiag(ub[i], left2)) for i in rng]
    bl_a = [b_ca[i][L - 1:L, :] for i in rng]
    bl_b = [b_cb[i][L - 1:L, :] for i in rng]
    wk = [(jnp.where(left2, jnp.exp(bl_a[i] - b_ca[i]), jnp.exp(bl_b[i] - b_cb[i])) * k2[i]).astype(BF16)
          for i in rng]
    ds = [_dot_tn(wk[i], ub[i]) for i in rng]
    for i, (b, p) in enumerate(pairs):
        for j, bl in enumerate((bl_a[i], bl_b[i])):
            h = 2 * p + j
            blk = slice(j * dh, (j + 1) * dh)
            cols = slice(h * dh, (h + 1) * dh)
            s_ref[b, h] = jnp.exp(bl) * s_old[i][j] + ds[i][blk, blk]
            z = _tok_get(z_ref, b, L, cols)
            _tok_set(h_ref, b, L, cols,
                     _rms_gate(o[i][:, blk], ng_ref[:, cols], z * jax.nn.sigmoid(z)).astype(h_ref.dtype))


CONV_HIST = CONV_W - 1
CONV_BASE = SUBLANES - CONV_HIST


def _unit(x):
    return x * lax.rsqrt(jnp.sum(x * x, axis=-1, keepdims=True) + RMS_EPS)


def _gdn_conv_act(b, xq_ref, xk_ref, xv_ref, cw_ref, conv_ref, buf, L, carry):
    buf[b, SUBLANES:SUBLANES + L, 0:G_WIDTH] = _tok_get(xq_ref, b, L, ALL)
    buf[b, SUBLANES:SUBLANES + L, G_WIDTH:2 * G_WIDTH] = _tok_get(xk_ref, b, L, ALL)
    buf[b, SUBLANES:SUBLANES + L, 2 * G_WIDTH:3 * G_WIDTH] = _tok_get(xv_ref, b, L, ALL)
    if L % SUBLANES == 0:
        rows = buf[b, 0:SUBLANES + L, :]
        y = cw_ref[CONV_HIST:CONV_W, :] * rows[SUBLANES:, :]
        for shift in range(1, CONV_W):
            j = CONV_HIST - shift
            y = y + cw_ref[j:j + 1, :] * pltpu.roll(rows, shift, 0)[SUBLANES:, :]
    else:
        y = cw_ref[0:1, :] * buf[b, CONV_BASE:CONV_BASE + L, :]
        for j in range(1, CONV_W):
            y = y + cw_ref[j:j + 1, :] * buf[b, CONV_BASE + j:CONV_BASE + j + L, :]
    conv_ref[b] = buf[b, SUBLANES + L - CONV_HIST:SUBLANES + L, :]
    if carry:
        buf[b, 0:SUBLANES, :] = buf[b, L:L + SUBLANES, :]
    return y * jax.nn.sigmoid(y)


def _gdn_paired_kernel(*refs, L, nb, nc, cast_periods):
    (xq_ref, xk_ref, xv_ref, z_ref, gcol_ref, gb_row_ref, al_row_ref, gp_ref, gbp_ref, alp_ref, cw_ref,
     ng_ref) = refs[:12]
    n_cast = len(cast_periods)
    cast_src = refs[12:12 + n_cast]
    h_ref, s_ref, conv_ref = refs[12 + n_cast:15 + n_cast]
    cast_dst = refs[15 + n_cast:15 + 2 * n_cast]
    buf = refs[15 + 2 * n_cast]
    _run_side_casts(cast_src, cast_dst, cast_periods, nc)

    @pl.when(pl.program_id(1) == 0)
    def _():
        buf[:, 0:SUBLANES, :] = jnp.zeros((nb, SUBLANES, 3 * G_WIDTH), F32)
        s_ref[...] = jnp.zeros_like(s_ref)

    scale = G_HEAD_DIM ** -0.5
    qn, kn, vv = [], [], []
    for b in range(nb):
        act = _gdn_conv_act(b, xq_ref, xk_ref, xv_ref, cw_ref, conv_ref, buf, L, True)
        qn.append(jnp.concatenate(
            [_unit(act[:, h * G_HEAD_DIM:(h + 1) * G_HEAD_DIM]) * scale for h in range(G_HEADS)], axis=-1))
        kn.append(jnp.concatenate(
            [_unit(act[:, G_WIDTH + h * G_HEAD_DIM:G_WIDTH + (h + 1) * G_HEAD_DIM]) for h in range(G_HEADS)],
            axis=-1))
        vv.append(act[:, 2 * G_WIDTH:3 * G_WIDTH])
    gates_c = [_gates_col_form(gcol_ref[b], gb_row_ref, al_row_ref, L) for b in range(nb)]
    cum_p = [_gate_rows_paired(gp_ref[b, 0], gbp_ref[...], alp_ref[...], L) for b in range(nb)]
    _gdn_heads_paired(qn, kn, vv, gates_c, cum_p, s_ref, z_ref, h_ref, ng_ref, L, nb)


def _gdn_kernel(*refs, L, nb, nc, has_init):
    xq_ref, xk_ref, xv_ref, z_ref, gcol_ref, grow_ref, gb_row_ref, gb_col_ref, al_row_ref, al_col_ref = refs[:10]
    if has_init:
        cw_ref, ng_ref, s0_ref, conv0_ref, h_ref, s_ref, conv_ref, buf = refs[10:]
    else:
        cw_ref, ng_ref, h_ref, s_ref, conv_ref, buf = refs[10:]

    def cols(h):
        return slice(h * G_HEAD_DIM, (h + 1) * G_HEAD_DIM)

    s_src = s0_ref if (has_init and nc == 1) else s_ref

    @pl.when(pl.program_id(1) == 0)
    def _():
        buf[:, 0:SUBLANES, :] = jnp.zeros((nb, SUBLANES, 3 * G_WIDTH), F32)
        if has_init:
            for b in range(nb):
                buf[b, CONV_BASE:SUBLANES, :] = conv0_ref[b]
            if nc > 1:
                s_ref[...] = s0_ref[...]
        else:
            s_ref[...] = jnp.zeros_like(s_ref)

    act = [_gdn_conv_act(b, xq_ref, xk_ref, xv_ref, cw_ref, conv_ref, buf, L, nc > 1) for b in range(nb)]
    causal, strict, _ = _chunk_masks(L)
    scale = G_HEAD_DIM ** -0.5
    gates = [_gates_both_forms(_tok_get(gcol_ref, b, L, ALL), grow_ref[b, 0], gb_row_ref, gb_col_ref,
                               al_row_ref, al_col_ref, L) for b in range(nb)]
    probs = [(b, h) for b in range(nb) for h in range(G_HEADS)]
    rng = range(len(probs))

    q = [_unit(act[b][:, h * G_HEAD_DIM:(h + 1) * G_HEAD_DIM]) * scale for b, h in probs]
    k = [_unit(act[b][:, G_WIDTH + h * G_HEAD_DIM:G_WIDTH + (h + 1) * G_HEAD_DIM]) for b, h in probs]
    v = [act[b][:, 2 * G_WIDTH + h * G_HEAD_DIM:2 * G_WIDTH + (h + 1) * G_HEAD_DIM] for b, h in probs]
    qb = [x.astype(BF16) for x in q]
    kb = [x.astype(BF16) for x in k]
    b_c = [gates[b][1][:, GA0 + h:GA0 + h + 1] for b, h in probs]
    b_r = [gates[b][4][GA0 + h:GA0 + h + 1, :] for b, h in probs]
    bet = [gates[b][2][:, GB0 + h:GB0 + h + 1] for b, h in probs]
    decay = [jnp.exp(jnp.where(causal, b_c[i] - b_r[i], -jnp.inf)) for i in rng]
    eb = [jnp.exp(b_c[i]) for i in rng]
    kk = [_dot_nt(kb[i], kb[i]) for i in rng]
    qk = [_dot_nt(qb[i], kb[i]) * decay[i] for i in rng]
    s_old = [s_src[b, h] for b, h in probs]
    sb = [x.astype(BF16) for x in s_old]
    qs = [_dot(qb[i], sb[i]) for i in rng]

    t = _inv_unit_lower_small([jnp.where(strict, bet[i] * kk[i] * decay[i], 0.0) for i in rng], L)
    rhs = [jnp.concatenate([bet[i] * v[i], (bet[i] * eb[i]) * k[i]], axis=-1) for i in rng]
    sol = [_mm_small(t[i], rhs[i]) for i in rng]
    u = [sol[i][:, 0:G_HEAD_DIM] - _dot(sol[i][:, G_HEAD_DIM:2 * G_HEAD_DIM].astype(BF16), sb[i]) for i in rng]
    ub = [x.astype(BF16) for x in u]
    o = [eb[i] * qs[i] + _dot(qk[i].astype(BF16), ub[i]) for i in rng]
    b_last = [b_c[i][L - 1:L, :] for i in rng]
    wk = [(jnp.exp(b_last[i] - b_c[i]) * k[i]).astype(BF16) for i in rng]
    ds = [_dot_tn(wk[i], ub[i]) for i in rng]
    for i, (b, h) in enumerate(probs):
        s_ref[b, h] = jnp.exp(b_last[i]) * s_old[i] + ds[i]
        z = _tok_get(z_ref, b, L, cols(h))
        _tok_set(h_ref, b, L, cols(h),
                 _rms_gate(o[i], ng_ref[:, cols(h)], z * jax.nn.sigmoid(z)).astype(h_ref.dtype))


def _gdn_paired(proj, gates, gates_t, gvecs, conv_w, norm_g, batch, nc, L, nb, side_cast):
    assert 2 * L == GATE_PAD and nc > 1
    cast_specs, cast_periods = _side_cast_specs(side_cast, (batch // nb) * nc, nc)
    blk0 = 4 * M_WIDTH // G_WIDTH

    def tok(width, col_block):
        return pl.BlockSpec((nb, L, width), lambda b, c: (b, c, col_block))

    vec_r = pl.BlockSpec((1, GATE_PAD), lambda b, c: (0, 0))
    tab = pl.BlockSpec((PAIR_ROWS, GATE_PAD), lambda b, c: (0, 0))
    gb_row, _, al_row, _ = gvecs
    pair = lambda vec: jnp.repeat(vec.reshape(GATE_PAD)[:2 * PAIR_ROWS], L).reshape(PAIR_ROWS, GATE_PAD)
    in_specs = [tok(G_WIDTH, blk0 + j) for j in range(4)]
    in_specs += [tok(GATE_PAD, 0), vec_r, vec_r,
                 pl.BlockSpec((nb, 1, PAIR_ROWS, GATE_PAD), lambda b, c: (b, c, 0, 0)), tab, tab,
                 pl.BlockSpec((CONV_W, 3 * G_WIDTH), lambda b, c: (0, 0)),
                 pl.BlockSpec((1, G_WIDTH), lambda b, c: (0, 0))]
    args = [proj, proj, proj, proj, gates, gb_row, al_row,
            gates_t.reshape(batch, nc, GATE_PAD // 2, GATE_PAD), pair(gb_row), pair(al_row),
            conv_w, norm_g.reshape(1, G_WIDTH)]
    s_spec = pl.BlockSpec((nb, G_HEADS, G_HEAD_DIM, G_HEAD_DIM), lambda b, c: (b, 0, 0, 0))
    conv_spec = pl.BlockSpec((nb, CONV_HIST, 3 * G_WIDTH), lambda b, c: (b, 0, 0))
    outs = pl.pallas_call(
        functools.partial(_gdn_paired_kernel, L=L, nb=nb, nc=nc, cast_periods=cast_periods),
        grid=(batch // nb, nc),
        in_specs=in_specs + cast_specs,
        out_specs=[tok(G_WIDTH, 0), s_spec, conv_spec] + cast_specs,
        out_shape=[jax.ShapeDtypeStruct(proj.shape[:-1] + (G_WIDTH,), _scan_out_dtype(L)),
                   jax.ShapeDtypeStruct((batch, G_HEADS, G_HEAD_DIM, G_HEAD_DIM), F32),
                   jax.ShapeDtypeStruct((batch, CONV_HIST, 3 * G_WIDTH), F32)]
        + [jax.ShapeDtypeStruct(w.shape, BF16) for w in side_cast],
        scratch_shapes=[pltpu.VMEM((nb, SUBLANES + L, 3 * G_WIDTH), F32)],
        compiler_params=_cparams(("arbitrary", "arbitrary")),
        name="gdn",
    )(*args, *side_cast)
    return outs[:3], outs[3:]


def _gdn(proj, gates, gates_t, gvecs, conv_w, norm_g, batch, nc, L, nb, init, side_cast=()):
    if init is None and L > SMALL_L:
        return _gdn_paired(proj, gates, gates_t, gvecs, conv_w, norm_g, batch, nc, L, nb, side_cast)
    assert L <= SMALL_L and not side_cast
    has_init = init is not None
    blk0 = 4 * M_WIDTH // G_WIDTH
    in_specs = [_tok_spec(L, nb, nc, G_WIDTH, blk0 + j) for j in range(4)] + _gate_specs(L, nb, nc)
    in_specs += [pl.BlockSpec((CONV_W, 3 * G_WIDTH), lambda b, c: (0, 0)),
                 pl.BlockSpec((1, G_WIDTH), lambda b, c: (0, 0))]
    s_spec = pl.BlockSpec((nb, G_HEADS, G_HEAD_DIM, G_HEAD_DIM), lambda b, c: (b, 0, 0, 0))
    conv_spec = pl.BlockSpec((nb, CONV_HIST, 3 * G_WIDTH), lambda b, c: (b, 0, 0))
    args = [proj, proj, proj, proj, gates, gates_t, *gvecs, conv_w, norm_g.reshape(1, G_WIDTH)]
    if has_init:
        in_specs += [s_spec, conv_spec]
        args += list(init)
    outs = pl.pallas_call(
        functools.partial(_gdn_kernel, L=L, nb=nb, nc=nc, has_init=has_init),
        grid=(batch // nb, nc),
        in_specs=in_specs,
        out_specs=[_tok_spec(L, nb, nc, G_WIDTH, 0), s_spec, conv_spec],
        out_shape=[jax.ShapeDtypeStruct(proj.shape[:-1] + (G_WIDTH,), _scan_out_dtype(L)),
                   jax.ShapeDtypeStruct((batch, G_HEADS, G_HEAD_DIM, G_HEAD_DIM), F32),
                   jax.ShapeDtypeStruct((batch, CONV_HIST, 3 * G_WIDTH), F32)],
        scratch_shapes=[pltpu.VMEM((nb, 2 * SUBLANES, 3 * G_WIDTH), F32)],
        compiler_params=_cparams(("arbitrary", "arbitrary")),
        name="gdn",
    )(*args)
    return outs, []


def _layer_norm(y, g, b):
    mu = jnp.mean(y, axis=-1, keepdims=True)
    yc = y - mu
    var = jnp.mean(yc * yc, axis=-1, keepdims=True)
    return yc * lax.rsqrt(var + LN_EPS) * g + b


def _outproj_kernel(hm_ref, hg_ref, x_ref, gt_ref, w_ref, g_ref, b_ref, o_ref):
    tm = x_ref.shape[0]
    half = tm // OUTPROJ_SLABS
    for r in range(OUTPROJ_SLABS):
        rows = slice(r * half, (r + 1) * half)
        mix = (_dot(hm_ref[rows, :].astype(BF16), w_ref[0:M_WIDTH, :])
               + _dot(hg_ref[rows, :].astype(BF16), w_ref[M_WIDTH:M_WIDTH + G_WIDTH, :]))
        gt = gt_ref[0] if gt_ref.shape[1] == 1 else gt_ref[0, rows, :]
        y = DEEPNORM_ALPHA * x_ref[rows, :] + (1.0 + gt) * mix
        o_ref[rows, :] = _layer_norm(y, g_ref[...], b_ref[...])


def _outproj(hm, hg, x2d, gt, w_out, ln_g, ln_b, tm, tiles_per_mod):
    m, d = x2d.shape
    vec = pl.BlockSpec((1, d), lambda i: (0, 0))
    return pl.pallas_call(
        _outproj_kernel,
        grid=(m // tm,),
        in_specs=[pl.BlockSpec((tm, M_WIDTH), lambda i: (i, 0)),
                  pl.BlockSpec((tm, G_WIDTH), lambda i: (i, 0)),
                  pl.BlockSpec((tm, d), lambda i: (i, 0)),
                  _mod_spec(gt, tiles_per_mod, 1),
                  pl.BlockSpec((d, d), lambda i: (0, 0)),
                  vec, vec],
        out_specs=pl.BlockSpec((tm, d), lambda i: (i, 0)),
        out_shape=jax.ShapeDtypeStruct((m, d), F32),
        compiler_params=_cparams(("arbitrary",)),
        name="outproj",
    )(hm, hg, x2d, gt, w_out, ln_g.reshape(1, d), ln_b.reshape(1, d))


def _ffn_kernel(x_ref, sc_ref, sh_ref, gt_ref, wg_ref, wu_ref, wd_ref, g_ref, b_ref, o_ref, h_scr, acc):
    f = pl.program_id(1)

    @pl.when(f == 0)
    def _():
        h_scr[...] = (x_ref[...] * (1.0 + sc_ref[0]) + sh_ref[0]).astype(BF16)
        acc[...] = jnp.zeros_like(acc)

    h = h_scr[...]
    gate = _dot(h, wg_ref[...])
    up = _dot(h, wu_ref[...])
    act = (gate * jax.nn.sigmoid(gate) * up).astype(BF16)
    acc[...] += _dot(act, wd_ref[...])

    @pl.when(f == pl.num_programs(1) - 1)
    def _():
        y = DEEPNORM_ALPHA * x_ref[...] + (1.0 + gt_ref[0]) * acc[...]
        o_ref[...] = _layer_norm(y, g_ref[...], b_ref[...])


def _ffn(x2d, sc, sh, gt, w_gu, w_down, ln_g, ln_b, tm, tiles_per_mod):
    m, d = x2d.shape
    tf = TF_FFN
    nf = D_FF // tf
    mod_spec = _mod_spec(sc, tiles_per_mod, 2)
    vec = pl.BlockSpec((1, d), lambda i, f: (0, 0))
    return pl.pallas_call(
        _ffn_kernel,
        grid=(m // tm, nf),
        in_specs=[pl.BlockSpec((tm, d), lambda i, f: (i, 0)),
                  mod_spec, mod_spec, mod_spec,
                  pl.BlockSpec((d, tf), lambda i, f: (0, f)),
                  pl.BlockSpec((d, tf), lambda i, f: (0, nf + f)),
                  pl.BlockSpec((tf, d), lambda i, f: (f, 0)),
                  vec, vec],
        out_specs=pl.BlockSpec((tm, d), lambda i, f: (i, 0)),
        out_shape=jax.ShapeDtypeStruct((m, d), F32),
        scratch_shapes=[pltpu.VMEM((tm, d), BF16), pltpu.VMEM((tm, d), F32)],
        compiler_params=_cparams(("arbitrary", "arbitrary")),
        name="ffn",
    )(x2d, sc, sh, gt, w_gu, w_gu, w_down, ln_g.reshape(1, d), ln_b.reshape(1, d))


def _layer(x, ada6, weights, init_m, init_g, nb):
    batch, seq, d = x.shape
    L = CHUNK if seq % CHUNK == 0 else seq
    nc = seq // L
    m = batch * seq
    x2d = x.reshape(m, d)

    def tiling(tm_cap):
        tm = min(tm_cap, m)
        if seq % tm == 0:
            return tm, seq // tm, lambda a: a.reshape(batch, 1, d)
        return tm, 1, lambda a: jnp.repeat(a, seq, axis=0).reshape(m // tm, tm, d)

    sh1, sc1, gt1, sh2, sc2, gt2 = ada6
    tm, tpm, mod = tiling(TM_INPROJ)
    proj, gates = _inproj(x2d, mod(sc1), mod(sh1), weights["w_in_main"], weights["w_in_gate"], tm, tpm)
    gates_t = jnp.swapaxes(gates.reshape(batch, nc, L, GATE_PAD), 2, 3)
    proj_v, gates_v = _tok_view(proj, batch, nc), _tok_view(gates, batch, nc)
    gvecs = weights["gvecs"]
    hm, c1, n1, m1 = _mlstm(proj_v, gates_v, gates_t, gvecs, weights["m_norm_g"], batch, nc, L, nb["mlstm"],
                            init_m)
    pending = [name for name in ("w_out", "w_gu", "w_down") if weights[name].dtype != BF16]
    (hg, s1, conv1), converted = _gdn(proj_v, gates_v, gates_t, gvecs, weights["conv_w"], weights["g_norm_g"],
                                      batch, nc, L, nb["gdn"], init_g, [weights[name] for name in pending])
    weights.update(zip(pending, converted))
    tm, tpm, mod = tiling(TM_OUTPROJ)
    x1 = _outproj(hm.reshape(m, M_WIDTH), hg.reshape(m, G_WIDTH), x2d, mod(gt1), weights["w_out"],
                  weights["ln1_g"], weights["ln1_b"], tm, tpm)
    tm, tpm, mod = tiling(TM_FFN)
    y = _ffn(x1, mod(sc2), mod(sh2), mod(gt2), weights["w_gu"], weights["w_down"], weights["ln2_g"],
             weights["ln2_b"], tm, tpm)
    return y.reshape(batch, seq, d), c1, n1, m1[:, 0, :M_HEADS], s1, conv1


def kernel(x_prompt, x_sample, state_mlstm_C, state_mlstm_n, state_mlstm_m, state_gdn_S, state_gdn_conv,
           c_prompt, c_sample, w_ada, b_ada, w_in, m_i_bias, m_f_bias, m_norm_g, conv_w, g_dt_bias,
           g_A_log, g_norm_g, w_out, ln1_g, ln1_b, w_gu, w_down, ln2_g, ln2_b):
    bp, _, d = x_prompt.shape
    bs = x_sample.shape[0]

    ada = _ada(jnp.concatenate([c_prompt, c_sample], axis=0), w_ada, b_ada)
    ada6 = [ada[:, i * d:(i + 1) * d] for i in range(6)]

    gate_bias = jnp.zeros((GATE_PAD,), F32)
    gate_bias = gate_bias.at[GI0:GI0 + M_HEADS].set(m_i_bias).at[GF0:GF0 + M_HEADS].set(m_f_bias)
    gate_bias = gate_bias.at[GA0:GA0 + G_HEADS].set(g_dt_bias)
    a_log = jnp.zeros((GATE_PAD,), F32).at[GA0:GA0 + G_HEADS].set(g_A_log)
    weights = {
        "w_in_main": _cast_bf16(w_in.T, MAIN_COLS, d),
        "w_in_gate": jnp.pad(w_in.T[MAIN_COLS:], ((0, GATE_PAD - GATE_COLS), (0, 0))).astype(BF16),
        "gvecs": (gate_bias.reshape(1, GATE_PAD), gate_bias.reshape(GATE_PAD, 1),
                  a_log.reshape(1, GATE_PAD), a_log.reshape(GATE_PAD, 1)),
        "m_norm_g": m_norm_g, "conv_w": conv_w, "g_norm_g": g_norm_g,
        "w_out": w_out, "w_gu": w_gu, "w_down": w_down,
        "ln1_g": ln1_g, "ln1_b": ln1_b, "ln2_g": ln2_g, "ln2_b": ln2_b,
    }

    y_p, p_c, p_n, p_m, p_s, p_conv = _layer(x_prompt, [a[:bp] for a in ada6], weights, None, None,
                                             SEQS_PER_STEP_PROMPT)
    m0 = jnp.pad(state_mlstm_m, ((0, 0), (0, GATE_PAD - M_HEADS))).reshape(bs, 1, GATE_PAD)
    y_s, s_c, s_n, s_m, s_s, s_conv = _layer(
        x_sample, [a[bp:] for a in ada6], weights, (state_mlstm_C, state_mlstm_n, m0),
        (state_gdn_S, state_gdn_conv), SEQS_PER_STEP_SAMPLE)
    return (y_p, y_s, p_c, p_n, p_m, p_s, p_conv, s_c, s_n, s_m, s_s, s_conv)
```

```python
import functools

import jax
import jax.numpy as jnp
from jax import lax
from jax.experimental import pallas as pl
from jax.experimental.pallas import tpu as pltpu

F32 = jnp.float32
BF16 = jnp.bfloat16

D_MODEL = 2048
M_HEADS = 4
M_HEAD_DIM = 256
M_WIDTH = M_HEADS * M_HEAD_DIM
G_HEADS = 8
G_HEAD_DIM = 128
G_WIDTH = G_HEADS * G_HEAD_DIM
CONV_W = 4
CHUNK = 64
D_FF = 5632
MAIN_COLS = 4 * M_WIDTH + 3 * G_WIDTH + G_WIDTH
GATE_COLS = 2 * M_HEADS + 2 * G_HEADS
GATE_PAD = 128
DEEPNORM_ALPHA = 2.0 ** 0.25
LN_EPS = 1e-5
RMS_EPS = 1e-6
GI0, GF0, GB0, GA0 = 0, M_HEADS, 2 * M_HEADS, 2 * M_HEADS + G_HEADS

SUBLANES = 8
VMEM_LIMIT_BYTES = 56 * 1024 * 1024
TM_INPROJ, TM_OUTPROJ, TM_FFN = 1024, 512, 512
TN_INPROJ, TN_ADA, TF_FFN = 1024, 1024, 512
OUTPROJ_SLABS = 2
CAST_ROWS, CAST_COLS = 2048, 1024
SEQS_PER_STEP_PROMPT = {"mlstm": 1, "gdn": 2}
SEQS_PER_STEP_SAMPLE = {"mlstm": 4, "gdn": 8}
MLSTM_GROUP_LONG, MLSTM_GROUP_SHORT = 1, 2
SMALL_L = 8
PAIR_ROWS = 16


def _cparams(sem):
    return pltpu.CompilerParams(dimension_semantics=sem, vmem_limit_bytes=VMEM_LIMIT_BYTES)


def _dot(a, b):
    return jnp.dot(a, b, preferred_element_type=F32)


def _dot_nt(a, b):
    return lax.dot_general(a, b, (((1,), (1,)), ((), ())), preferred_element_type=F32)


def _dot_tn(a, b):
    return lax.dot_general(a, b, (((0,), (0,)), ((), ())), preferred_element_type=F32)


def _split2(x):
    hi = x.astype(BF16)
    return hi, (x - hi.astype(F32)).astype(BF16)


def _split3(x):
    hi = x.astype(BF16)
    r = x - hi.astype(F32)
    mid = r.astype(BF16)
    return hi, mid, (r - mid.astype(F32)).astype(BF16)


def _dot3_stacked(a, b):
    m = a[0].shape[0]
    both = _dot(jnp.concatenate([a[0], a[1]], axis=0), b[0])
    return both[0:m] + (_dot(a[0], b[1]) + both[m:2 * m])


def _mm_small(a, b):
    out = a[:, 0:1] * b[0:1, :]
    for i in range(1, a.shape[1]):
        out = out + a[:, i:i + 1] * b[i:i + 1, :]
    return out


def _tok_get(ref, b, L, cols):
    if len(ref.shape) == 3:
        return ref[b, :, cols]
    return ref[b * L:(b + 1) * L, cols]


def _tok_set(ref, b, L, cols, val):
    if len(ref.shape) == 3:
        ref[b, :, cols] = val
    else:
        ref[b * L:(b + 1) * L, cols] = val


ALL = slice(None)


def _cast_kernel(x_ref, o_ref):
    o_ref[...] = x_ref[...].astype(BF16)


def _cast_bf16(w, rows, cols):
    bc = min(cols, CAST_COLS)
    br = max(r for r in range(SUBLANES, min(rows, CAST_ROWS) + 1, SUBLANES) if rows % r == 0)
    return pl.pallas_call(
        _cast_kernel,
        grid=(rows // br, cols // bc),
        in_specs=[pl.BlockSpec((br, bc), lambda i, j: (i, j))],
        out_specs=pl.BlockSpec((br, bc), lambda i, j: (i, j)),
        out_shape=jax.ShapeDtypeStruct((rows, cols), BF16),
        compiler_params=_cparams(("arbitrary", "arbitrary")),
        name="cast",
    )(w)


def _ada_kernel(c_ref, w_ref, b_ref, o_ref):
    c = c_ref[...]
    a = (c * jax.nn.sigmoid(c)).astype(BF16)
    o_ref[...] = _dot(a, w_ref[...].astype(BF16)) + b_ref[...]


def _ada(c_all, w_ada, b_ada):
    n_rows, d = c_all.shape
    n_cols = w_ada.shape[1]
    tn = TN_ADA
    return pl.pallas_call(
        _ada_kernel,
        grid=(n_cols // tn,),
        in_specs=[pl.BlockSpec((n_rows, d), lambda j: (0, 0)),
                  pl.BlockSpec((d, tn), lambda j: (0, j)),
                  pl.BlockSpec((1, tn), lambda j: (0, j))],
        out_specs=pl.BlockSpec((n_rows, tn), lambda j: (0, j)),
        out_shape=jax.ShapeDtypeStruct((n_rows, n_cols), F32),
        compiler_params=_cparams(("arbitrary",)),
        name="ada",
    )(c_all, w_ada, b_ada.reshape(1, n_cols))


def _inproj_kernel(x_ref, sc_ref, sh_ref, wt_ref, wgt_ref, o_ref, og_ref, h_scr):
    @pl.when(pl.program_id(1) == 0)
    def _():
        h = (x_ref[...] * (1.0 + sc_ref[0]) + sh_ref[0]).astype(BF16)
        h_scr[...] = h
        og_ref[...] = _dot_nt(h, wgt_ref[...])

    o_ref[...] = _dot_nt(h_scr[...], wt_ref[...])


def _mod_spec(mod, tiles_per_mod, grid_rank):
    _, mod_rows, d = mod.shape
    if grid_rank == 1:
        index_map = lambda i: (i // tiles_per_mod, 0, 0)
    else:
        index_map = lambda i, j: (i // tiles_per_mod, 0, 0)
    if mod_rows == 1:
        return pl.BlockSpec((1, 1, d), index_map)
    return pl.BlockSpec((1, mod_rows, d), index_map, pipeline_mode=pl.Buffered(1))


def _inproj(x2d, sc, sh, w_main, w_gate, tm, tiles_per_mod):
    m, d = x2d.shape
    tn = TN_INPROJ
    mod_spec = _mod_spec(sc, tiles_per_mod, 2)
    return pl.pallas_call(
        _inproj_kernel,
        grid=(m // tm, MAIN_COLS // tn),
        in_specs=[pl.BlockSpec((tm, d), lambda i, j: (i, 0)),
                  mod_spec, mod_spec,
                  pl.BlockSpec((tn, d), lambda i, j: (j, 0)),
                  pl.BlockSpec((GATE_PAD, d), lambda i, j: (0, 0))],
        out_specs=[pl.BlockSpec((tm, tn), lambda i, j: (i, j)),
                   pl.BlockSpec((tm, GATE_PAD), lambda i, j: (i, 0))],
        out_shape=[jax.ShapeDtypeStruct((m, MAIN_COLS), F32),
                   jax.ShapeDtypeStruct((m, GATE_PAD), F32)],
        scratch_shapes=[pltpu.VMEM((tm, d), BF16)],
        compiler_params=_cparams(("arbitrary", "arbitrary")),
        name="inproj",
    )(x2d, sc, sh, w_main, w_gate)


def _gate_tables(g, bias, alog, gid, delta_rule):
    x = g + bias
    is_f = (gid >= GF0) & (gid < GB0)
    log_f = jax.nn.log_sigmoid(x)
    if not delta_rule:
        return x, jnp.where(is_f, log_f, 0.0), None
    is_a = (gid >= GA0) & (gid < GA0 + G_HEADS)
    log_a = -jnp.exp(alog) * jax.nn.softplus(x)
    inc = jnp.where(is_f, log_f, jnp.where(is_a, log_a, 0.0))
    return x, inc, jax.nn.sigmoid(x)


def _chunk_masks(L):
    row = lax.broadcasted_iota(jnp.int32, (L, L), 0)
    col = lax.broadcasted_iota(jnp.int32, (L, L), 1)
    return row >= col, row > col, row == col


def _gates_col_form(g_col, gb_row_ref, al_row_ref, L, delta_rule=True):
    causal, _, _ = _chunk_masks(L)
    tril = causal.astype(F32)
    gid_c = lax.broadcasted_iota(jnp.int32, (L, GATE_PAD), 1)
    x_c, inc_c, beta_c = _gate_tables(g_col, gb_row_ref[...], al_row_ref[...], gid_c, delta_rule)
    if L <= SMALL_L:
        cum_c = _mm_small(tril, inc_c)
    else:
        tril_b = tril.astype(BF16)
        c1, c2, c3 = _split3(inc_c)
        cum_c = _dot(tril_b, c1) + (_dot(tril_b, c2) + _dot(tril_b, c3))
    return x_c, cum_c, beta_c


def _gates_both_forms(g_col, g_row, gb_row_ref, gb_col_ref, al_row_ref, al_col_ref, L, delta_rule=True):
    x_c, cum_c, beta_c = _gates_col_form(g_col, gb_row_ref, al_row_ref, L, delta_rule)
    triu = (lax.broadcasted_iota(jnp.int32, (L, L), 0) <= lax.broadcasted_iota(jnp.int32, (L, L), 1)).astype(F32)
    n_rows = GATE_PAD if delta_rule else SUBLANES
    assert GB0 <= n_rows
    gid_r = lax.broadcasted_iota(jnp.int32, (n_rows, L), 0)
    x_r, inc_r, _ = _gate_tables(g_row[0:n_rows], gb_col_ref[0:n_rows, :], al_col_ref[0:n_rows, :], gid_r, delta_rule)
    if L <= SMALL_L:
        cum_r = _mm_small(inc_r, triu)
    else:
        triu_b = triu.astype(BF16)
        r1, r2, r3 = _split3(inc_r)
        cum_r = _dot(r1, triu_b) + (_dot(r2, triu_b) + _dot(r3, triu_b))
    return x_c, cum_c, beta_c, x_r, cum_r


def _rms_gate(h, gain, gate):
    return h * lax.rsqrt(jnp.mean(h * h, axis=-1, keepdims=True) + RMS_EPS) * gain * gate


def _mlstm_kernel(*refs, L, nb, nc, has_init, group):
    if has_init:
        (qkvo_ref, gcol_ref, grow_ref, gb_row_ref, gb_col_ref, al_row_ref, al_col_ref,
         ng_ref, c0_ref, n0_ref, m0_ref, h_ref, c_ref, n_ref, m_ref) = refs
    else:
        (qkvo_ref, gcol_ref, grow_ref, gb_row_ref, gb_col_ref, al_row_ref, al_col_ref,
         ng_ref, h_ref, c_ref, n_ref, m_ref) = refs

    if has_init and nc == 1:
        c_src, n_src, m_src = c0_ref, n0_ref, m0_ref
    else:
        c_src, n_src, m_src = c_ref, n_ref, m_ref

        @pl.when(pl.program_id(1) == 0)
        def _():
            if has_init:
                c_ref[...] = c0_ref[...]
                n_ref[...] = n0_ref[...]
                m_ref[...] = m0_ref[...]
            else:
                c_ref[...] = jnp.zeros_like(c_ref)
                n_ref[...] = jnp.zeros_like(n_ref)
                m_ref[...] = jnp.zeros_like(m_ref)

    for b0 in range(0, nb, group):
        _mlstm_group(range(b0, min(b0 + group, nb)), L, qkvo_ref, gcol_ref, grow_ref, gb_row_ref,
                     gb_col_ref, al_row_ref, al_col_ref, ng_ref, c_src, n_src, m_src, h_ref, c_ref, n_ref, m_ref)


def _mlstm_group(seqs, L, qkvo_ref, gcol_ref, grow_ref, gb_row_ref, gb_col_ref, al_row_ref,
                 al_col_ref, ng_ref, c_src, n_src, m_src, h_ref, c_ref, n_ref, m_ref):
    causal, _, _ = _chunk_masks(L)
    lane = lax.broadcasted_iota(jnp.int32, (1, GATE_PAD), 1)
    scale = M_HEAD_DIM ** -0.5
    gates = {b: _gates_both_forms(_tok_get(gcol_ref, b, L, ALL), grow_ref[b, 0, 0:SUBLANES, :], gb_row_ref,
                                  gb_col_ref, al_row_ref, al_col_ref, L, delta_rule=False) for b in seqs}
    m_all = {b: m_src[b] for b in seqs}
    probs = [(b, h) for b in seqs for h in range(M_HEADS)]

    def cols(h, part=0):
        return slice(part * M_WIDTH + h * M_HEAD_DIM, part * M_WIDTH + (h + 1) * M_HEAD_DIM)

    q = [_tok_get(qkvo_ref, b, L, cols(h, 0)) for b, h in probs]
    k = [_tok_get(qkvo_ref, b, L, cols(h, 1)) * scale for b, h in probs]
    vb = [_tok_get(qkvo_ref, b, L, cols(h, 2)).astype(BF16) for b, h in probs]
    qb = [x.astype(BF16) for x in q]
    kb = [x.astype(BF16) for x in k]
    ig_c = [gates[b][0][:, GI0 + h:GI0 + h + 1] for b, h in probs]
    ig_r = [gates[b][3][GI0 + h:GI0 + h + 1, :] for b, h in probs]
    bt_c = [gates[b][1][:, GF0 + h:GF0 + h + 1] for b, h in probs]
    bt_r = [gates[b][4][GF0 + h:GF0 + h + 1, :] for b, h in probs]
    m0 = [jnp.sum(jnp.where(lane == h, m_all[b], 0.0), axis=1, keepdims=True) for b, h in probs]
    n_p = len(probs)
    rng = range(n_p)

    log_d = [jnp.where(causal, bt_c[i] - bt_r[i] + ig_r[i], -jnp.inf) for i in rng]
    inter = [bt_c[i] + m0[i] for i in rng]
    m_t = [jnp.maximum(inter[i], jnp.max(log_d[i], axis=1, keepdims=True)) for i in rng]
    inter_w = [jnp.exp(inter[i] - m_t[i]) for i in rng]
    qk = [_dot_nt(qb[i], kb[i]) for i in rng]
    c_old = [c_src[b, h] for b, h in probs]
    n_old = [n_src[b, h:h + 1, :] for b, h in probs]
    qc = [_dot(qb[i], c_old[i].astype(BF16)) for i in rng]
    s = [qk[i] * jnp.exp(log_d[i] - m_t[i]) for i in rng]
    sv = [_dot(s[i].astype(BF16), vb[i]) for i in rng]

    b_last = [bt_c[i][L - 1:L, :] for i in rng]
    m_new = [jnp.maximum(b_last[i] + m0[i], jnp.max(b_last[i] - bt_r[i] + ig_r[i], axis=1, keepdims=True))
             for i in rng]
    kw = [k[i] * jnp.exp(b_last[i] - bt_c[i] + ig_c[i] - m_new[i]) for i in rng]
    decay = [jnp.exp(b_last[i] + m0[i] - m_new[i]) for i in rng]
    kv = [_dot_tn(kw[i].astype(BF16), vb[i]) for i in rng]

    m_next = dict(m_all)
    for i, (b, h) in enumerate(probs):
        num = inter_w[i] * qc[i] + sv[i]
        den = inter_w[i] * jnp.sum(q[i] * n_old[i], axis=1, keepdims=True) + jnp.sum(s[i], axis=1, keepdims=True)
        hh = num / jnp.maximum(jnp.abs(den), jnp.exp(-m_t[i]))
        c_ref[b, h] = decay[i] * c_old[i] + kv[i]
        n_ref[b, h:h + 1, :] = decay[i] * n_old[i] + jnp.sum(kw[i], axis=0, keepdims=True)
        m_next[b] = jnp.where(lane == h, m_new[i], m_next[b])
        gate = jax.nn.sigmoid(_tok_get(qkvo_ref, b, L, cols(h, 3)))
        _tok_set(h_ref, b, L, cols(h), _rms_gate(hh, ng_ref[:, cols(h)], gate).astype(h_ref.dtype))
    for b in seqs:
        m_ref[b] = m_next[b]


def _tok_spec(L, nb, nc, width, col_block):
    if nc == 1:
        return pl.BlockSpec((nb * L, width), lambda b, c: (b, col_block))
    return pl.BlockSpec((nb, L, width), lambda b, c: (b, c, col_block))


def _tok_view(x2d, batch, nc):
    return x2d if nc == 1 else x2d.reshape(batch, x2d.shape[0] // batch, x2d.shape[1])


def _gate_specs(L, nb, nc):
    vec_r = pl.BlockSpec((1, GATE_PAD), lambda b, c: (0, 0))
    vec_c = pl.BlockSpec((GATE_PAD, 1), lambda b, c: (0, 0))
    return [_tok_spec(L, nb, nc, GATE_PAD, 0),
            pl.BlockSpec((nb, 1, GATE_PAD, L), lambda b, c: (b, c, 0, 0)),
            vec_r, vec_c, vec_r, vec_c]


def _scan_out_dtype(L):
    return BF16 if L % 16 == 0 else F32


BF16_ROWS = 16


def _slab_plan(rows, n_steps):
    period = 1
    while (rows * period) % (n_steps * BF16_ROWS):
        period *= 2
        assert period <= n_steps
    return rows * period // n_steps, period


def _side_cast_specs(side_cast, n_steps, nc):
    specs, periods = [], []
    for w in side_cast:
        slab, period = _slab_plan(w.shape[0], n_steps)
        specs.append(pl.BlockSpec((slab, w.shape[1]), lambda b, c, period=period: ((b * nc + c) // period, 0)))
        periods.append(period)
    return specs, tuple(periods)


def _run_side_casts(cast_src, cast_dst, periods, nc):
    step = pl.program_id(0) * nc + pl.program_id(1)
    for src, dst, period in zip(cast_src, cast_dst, periods):
        if period == 1:
            dst[...] = src[...].astype(BF16)
        else:
            @pl.when(step % period == 0)
            def _(src=src, dst=dst):
                dst[...] = src[...].astype(BF16)


def _mlstm(proj, gates, gates_t, gvecs, norm_g, batch, nc, L, nb, init):
    has_init = init is not None
    in_specs = [_tok_spec(L, nb, nc, 4 * M_WIDTH, 0)] + _gate_specs(L, nb, nc)
    in_specs.append(pl.BlockSpec((1, M_WIDTH), lambda b, c: (0, 0)))
    c_spec = pl.BlockSpec((nb, M_HEADS, M_HEAD_DIM, M_HEAD_DIM), lambda b, c: (b, 0, 0, 0))
    n_spec = pl.BlockSpec((nb, M_HEADS, M_HEAD_DIM), lambda b, c: (b, 0, 0))
    m_spec = pl.BlockSpec((nb, 1, GATE_PAD), lambda b, c: (b, 0, 0))
    args = [proj, gates, gates_t, *gvecs, norm_g.reshape(1, M_WIDTH)]
    if has_init:
        in_specs += [c_spec, n_spec, m_spec]
        args += list(init)
    return pl.pallas_call(
        functools.partial(_mlstm_kernel, L=L, nb=nb, nc=nc, has_init=has_init,
                          group=MLSTM_GROUP_SHORT if L <= SMALL_L else MLSTM_GROUP_LONG),
        grid=(batch // nb, nc),
        in_specs=in_specs,
        out_specs=[_tok_spec(L, nb, nc, M_WIDTH, 0), c_spec, n_spec, m_spec],
        out_shape=[jax.ShapeDtypeStruct(proj.shape[:-1] + (M_WIDTH,), _scan_out_dtype(L)),
                   jax.ShapeDtypeStruct((batch, M_HEADS, M_HEAD_DIM, M_HEAD_DIM), F32),
                   jax.ShapeDtypeStruct((batch, M_HEADS, M_HEAD_DIM), F32),
                   jax.ShapeDtypeStruct((batch, 1, GATE_PAD), F32)],
        compiler_params=_cparams(("arbitrary", "arbitrary")),
        name="mlstm",
    )(*args)


def _inv_unit_lower_small(a_list, L):
    _, _, eye = _chunk_masks(L)
    p = [-a for a in a_list]
    t = [eye.astype(F32) + x for x in p]
    for _ in range(max(1, (L - 1).bit_length()) - 1):
        p = [_mm_small(x, x) for x in p]
        t = [ti + _mm_small(ti, pi) for ti, pi in zip(t, p)]
    return t


def _gate_rows_paired(gp, bias_p, alog_p, L):
    rid = lax.broadcasted_iota(jnp.int32, gp.shape, 0)
    x = gp + bias_p
    is_f = (rid >= GF0 // 2) & (rid < GB0 // 2)
    is_a = (rid >= GA0 // 2) & (rid < (GA0 + G_HEADS) // 2)
    inc = jnp.where(is_f, jax.nn.log_sigmoid(x), jnp.where(is_a, -jnp.exp(alog_p) * jax.nn.softplus(x), 0.0))
    r = lax.broadcasted_iota(jnp.int32, (2 * L, 2 * L), 0)
    c = lax.broadcasted_iota(jnp.int32, (2 * L, 2 * L), 1)
    triu2 = (((r < L) == (c < L)) & (r <= c)).astype(BF16)
    p1, p2, p3 = _split3(inc)
    return _dot(p1, triu2) + (_dot(p2, triu2) + _dot(p3, triu2))


def _gdn_heads_paired(qn, kn, vv, gates_c, cum_p, s_ref, xz_ref, h_ref, ng_ref, L, nb):
    dh = G_HEAD_DIM
    row =lax.broadcasted_iota(jnp.int32, (L, 2 * L), 0)
    lane = lax.broadcasted_iota(jnp.int32, (L, 2 * L), 1)
    left = lane < L
    colp = jnp.where(left, lane, lane - L)
    causal_p, strict_p, eye_p = row >= colp, row > colp, (row == colp).astype(F32)
    left2 = lax.broadcasted_iota(jnp.int32, (L, 2 * dh), 1) < dh
    lane4 = lax.broadcasted_iota(jnp.int32, (L, 4 * dh), 1)
    first4 = (lane4 // dh) % 2 == 0

    def bdiag(y, first):
        z = jnp.zeros_like(y)
        return jnp.concatenate([jnp.where(first, y, z), jnp.where(first, z, y)], axis=0)

    pairs = [(b, p) for b in range(nb) for p in range(G_HEADS // 2)]
    rng = range(len(pairs))
    q2 = [qn[b][:, 2 * p * dh:(2 * p + 2) * dh] for b, p in pairs]
    k2 = [kn[b][:, 2 * p * dh:(2 * p + 2) * dh] for b, p in pairs]
    v2 = [vv[b][:, 2 * p * dh:(2 * p + 2) * dh] for b, p in pairs]
    q2b = [x.astype(BF16) for x in q2]
    k2b = [x.astype(BF16) for x in k2]
    kbd = [bdiag(x, left2) for x in k2b]
    b_ca = [gates_c[b][1][:, GA0 + 2 * p:GA0 + 2 * p + 1] for b, p in pairs]
    b_cb = [gates_c[b][1][:, GA0 + 2 * p + 1:GA0 + 2 * p + 2] for b, p in pairs]
    bet_a = [gates_c[b][2][:, GB0 + 2 * p:GB0 + 2 * p + 1] for b, p in pairs]
    bet_b = [gates_c[b][2][:, GB0 + 2 * p + 1:GB0 + 2 * p + 2] for b, p in pairs]
    b_r = [cum_p[b][GA0 // 2 + p:GA0 // 2 + p + 1, :] for b, p in pairs]
    decay = [jnp.exp(jnp.where(causal_p, jnp.where(left, b_ca[i], b_cb[i]) - b_r[i], -jnp.inf)) for i in rng]
    kk = [_dot_nt(k2b[i], kbd[i]) for i in rng]
    qk = [(_dot_nt(q2b[i], kbd[i]) * decay[i]).astype(BF16) for i in rng]
    s_old = [(s_ref[b, 2 * p], s_ref[b, 2 * p + 1]) for b, p in pairs]
    zero = jnp.zeros((dh, dh), BF16)
    sbd = [jnp.concatenate([jnp.concatenate([sa.astype(BF16), zero], axis=1),
                            jnp.concatenate([zero, sb.astype(BF16)], axis=1)], axis=0) for sa, sb in s_old]
    qs = [_dot(q2b[i], sbd[i]) for i in rng]

    a = [jnp.where(strict_p, jnp.where(left, bet_a[i], bet_b[i]) * kk[i] * decay[i], 0.0) for i in rng]
    p = [-x for x in a]
    t = [eye_p + x for x in p]
    ps = [_split2(x) for x in p]
    p = [_dot3_stacked(ps[i], (bdiag(ps[i][0], left), bdiag(ps[i][1], left))) for i in rng]
    for lvl in range(1, (L - 1).bit_length()):
        last = lvl == (L - 1).bit_length() - 1
        ps = [_split2(x) for x in p]
        pbd = [(bdiag(hi, left), bdiag(lo, left)) for hi, lo in ps]
        if last:
            t = [t[i] + _dot3_stacked(_split2(t[i]), pbd[i]) for i in rng]
        else:
            ts = [_split2(x) for x in t]
            both = [_dot3_stacked((jnp.concatenate([ts[i][0], ps[i][0]], axis=0),
                                   jnp.concatenate([ts[i][1], ps[i][1]], axis=0)), pbd[i]) for i in rng]
            t = [t[i] + both[i][0:L] for i in rng]
            p = [both[i][L:2 * L] for i in rng]

    bet2 = [jnp.where(left2, bet_a[i], bet_b[i]) for i in rng]
    eb2 = [jnp.where(left2, jnp.exp(b_ca[i]), jnp.exp(b_cb[i])) for i in rng]
    rhs = [_split2(jnp.concatenate([bet2[i] * v2[i], (bet2[i] * eb2[i]) * k2[i]], axis=-1)) for i in rng]
    sol = [_dot3_stacked(_split2(t[i]), (bdiag(rhs[i][0], first4), bdiag(rhs[i][1], first4))) for i in rng]
    u = [sol[i][:, 0:2 * dh] - _dot(sol[i][:, 2 * dh:4 * dh].astype(BF16), sbd[i]) for i in rng]
    ub = [x.astype(BF16) for x in u]
    o = [eb2[i] * qs[i] + _dot(qk[i], bdiag(ub[i], left2)) for i in rng]
    bl_a = [b_ca[i][L - 1:L, :] for i in rng]
    bl_b = [b_cb[i][L - 1:L, :] for i in rng]
    wk = [(jnp.where(left2, jnp.exp(bl_a[i] - b_ca[i]), jnp.exp(bl_b[i] - b_cb[i])) * k2[i]).astype(BF16)
          for i in rng]
    ds = [_dot_tn(wk[i], ub[i]) for i in rng]
    for i, (b, p) in enumerate(pairs):
        for j, bl in enumerate((bl_a[i], bl_b[i])):
            h = 2 * p + j
            blk = slice(j * dh, (j + 1) * dh)
            cols = slice(h * dh, (h + 1) * dh)
            s_ref[b, h] = jnp.exp(bl) * s_old[i][j] + ds[i][blk, blk]
            z = _tok_get(xz_ref, b, L, slice(3 * G_WIDTH + h * dh, 3 * G_WIDTH + (h + 1) * dh))
            _tok_set(h_ref, b, L, cols,
                     _rms_gate(o[i][:, blk], ng_ref[:, cols], z * jax.nn.sigmoid(z)).astype(h_ref.dtype))


CONV_HIST = CONV_W - 1
CONV_BASE = SUBLANES - CONV_HIST


def _unit(x):
    return x * lax.rsqrt(jnp.sum(x * x, axis=-1, keepdims=True) + RMS_EPS)


def _gdn_conv_act(b, xz_ref, cw_ref, conv_ref, buf, L, carry):
    buf[b, SUBLANES:SUBLANES + L, :] = _tok_get(xz_ref, b, L, slice(0, 3 * G_WIDTH))
    if L % SUBLANES == 0:
        rows = buf[b, 0:SUBLANES + L, :]
        y = cw_ref[CONV_HIST:CONV_W, :] * rows[SUBLANES:, :]
        for shift in range(1, CONV_W):
            j = CONV_HIST - shift
            y = y + cw_ref[j:j + 1, :] * pltpu.roll(rows, shift, 0)[SUBLANES:, :]
    else:
        y = cw_ref[0:1, :] * buf[b, CONV_BASE:CONV_BASE + L, :]
        for j in range(1, CONV_W):
            y = y + cw_ref[j:j + 1, :] * buf[b, CONV_BASE + j:CONV_BASE + j + L, :]
    conv_ref[b] = buf[b, SUBLANES + L - CONV_HIST:SUBLANES + L, :]
    if carry:
        buf[b, 0:SUBLANES, :] = buf[b, L:L + SUBLANES, :]
    return y * jax.nn.sigmoid(y)


def _gdn_paired_kernel(*refs, L, nb, nc, cast_periods):
    xz_ref, gcol_ref, gb_row_ref, al_row_ref, gp_ref, gbp_ref, alp_ref, cw_ref, ng_ref = refs[:9]
    n_cast = len(cast_periods)
    cast_src = refs[9:9 + n_cast]
    h_ref, s_ref, conv_ref = refs[9 + n_cast:12 + n_cast]
    cast_dst = refs[12 + n_cast:12 + 2 * n_cast]
    buf = refs[12 + 2 * n_cast]
    _run_side_casts(cast_src, cast_dst, cast_periods, nc)

    @pl.when(pl.program_id(1) == 0)
    def _():
        buf[:, 0:SUBLANES, :] = jnp.zeros((nb, SUBLANES, 3 * G_WIDTH), F32)
        s_ref[...] = jnp.zeros_like(s_ref)

    scale = G_HEAD_DIM ** -0.5
    qn, kn, vv = [], [], []
    for b in range(nb):
        act = _gdn_conv_act(b, xz_ref, cw_ref, conv_ref, buf, L, True)
        qn.append(jnp.concatenate(
            [_unit(act[:, h * G_HEAD_DIM:(h + 1) * G_HEAD_DIM]) * scale for h in range(G_HEADS)], axis=-1))
        kn.append(jnp.concatenate(
            [_unit(act[:, G_WIDTH + h * G_HEAD_DIM:G_WIDTH + (h + 1) * G_HEAD_DIM]) for h in range(G_HEADS)],
            axis=-1))
        vv.append(act[:, 2 * G_WIDTH:3 * G_WIDTH])
    gates_c = [_gates_col_form(gcol_ref[b], gb_row_ref, al_row_ref, L) for b in range(nb)]
    cum_p = [_gate_rows_paired(gp_ref[b, 0], gbp_ref[...], alp_ref[...], L) for b in range(nb)]
    _gdn_heads_paired(qn, kn, vv, gates_c, cum_p, s_ref, xz_ref, h_ref, ng_ref, L, nb)


def _gdn_kernel(*refs, L, nb, nc, has_init):
    xz_ref, gcol_ref, grow_ref, gb_row_ref, gb_col_ref, al_row_ref, al_col_ref = refs[:7]
    if has_init:
        cw_ref, ng_ref, s0_ref, conv0_ref, h_ref, s_ref, conv_ref, buf = refs[7:]
    else:
        cw_ref, ng_ref, h_ref, s_ref, conv_ref, buf = refs[7:]

    def cols(h, off=0):
        return slice(off + h * G_HEAD_DIM, off + (h + 1) * G_HEAD_DIM)

    s_src = s0_ref if (has_init and nc == 1) else s_ref

    @pl.when(pl.program_id(1) == 0)
    def _():
        buf[:, 0:SUBLANES, :] = jnp.zeros((nb, SUBLANES, 3 * G_WIDTH), F32)
        if has_init:
            for b in range(nb):
                buf[b, CONV_BASE:SUBLANES, :] = conv0_ref[b]
            if nc > 1:
                s_ref[...] = s0_ref[...]
        else:
            s_ref[...] = jnp.zeros_like(s_ref)

    act = [_gdn_conv_act(b, xz_ref, cw_ref, conv_ref, buf, L, nc > 1) for b in range(nb)]
    causal, strict, _ = _chunk_masks(L)
    scale = G_HEAD_DIM ** -0.5
    gates = [_gates_both_forms(_tok_get(gcol_ref, b, L, ALL), grow_ref[b, 0], gb_row_ref, gb_col_ref,
                               al_row_ref, al_col_ref, L) for b in range(nb)]
    probs = [(b, h) for b in range(nb) for h in range(G_HEADS)]
    rng = range(len(probs))

    q = [_unit(act[b][:, h * G_HEAD_DIM:(h + 1) * G_HEAD_DIM]) * scale for b, h in probs]
    k = [_unit(act[b][:, G_WIDTH + h * G_HEAD_DIM:G_WIDTH + (h + 1) * G_HEAD_DIM]) for b, h in probs]
    v = [act[b][:, 2 * G_WIDTH + h * G_HEAD_DIM:2 * G_WIDTH + (h + 1) * G_HEAD_DIM] for b, h in probs]
    qb = [x.astype(BF16) for x in q]
    kb = [x.astype(BF16) for x in k]
    b_c = [gates[b][1][:, GA0 + h:GA0 + h + 1] for b, h in probs]
    b_r = [gates[b][4][GA0 + h:GA0 + h + 1, :] for b, h in probs]
    bet = [gates[b][2][:, GB0 + h:GB0 + h + 1] for b, h in probs]
    decay = [jnp.exp(jnp.where(causal, b_c[i] - b_r[i], -jnp.inf)) for i in rng]
    eb = [jnp.exp(b_c[i]) for i in rng]
    kk = [_dot_nt(kb[i], kb[i]) for i in rng]
    qk = [_dot_nt(qb[i], kb[i]) * decay[i] for i in rng]
    s_old = [s_src[b, h] for b, h in probs]
    sb = [x.astype(BF16) for x in s_old]
    qs = [_dot(qb[i], sb[i]) for i in rng]

    t = _inv_unit_lower_small([jnp.where(strict, bet[i] * kk[i] * decay[i], 0.0) for i in rng], L)
    rhs = [jnp.concatenate([bet[i] * v[i], (bet[i] * eb[i]) * k[i]], axis=-1) for i in rng]
    sol = [_mm_small(t[i], rhs[i]) for i in rng]
    u = [sol[i][:, 0:G_HEAD_DIM] - _dot(sol[i][:, G_HEAD_DIM:2 * G_HEAD_DIM].astype(BF16), sb[i]) for i in rng]
    ub = [x.astype(BF16) for x in u]
    o = [eb[i] * qs[i] + _dot(qk[i].astype(BF16), ub[i]) for i in rng]
    b_last = [b_c[i][L - 1:L, :] for i in rng]
    wk = [(jnp.exp(b_last[i] - b_c[i]) * k[i]).astype(BF16) for i in rng]
    ds = [_dot_tn(wk[i], ub[i]) for i in rng]
    for i, (b, h) in enumerate(probs):
        s_ref[b, h] = jnp.exp(b_last[i]) * s_old[i] + ds[i]
        z = _tok_get(xz_ref, b, L, cols(h, 3 * G_WIDTH))
        _tok_set(h_ref, b, L, cols(h),
                 _rms_gate(o[i], ng_ref[:, cols(h)], z * jax.nn.sigmoid(z)).astype(h_ref.dtype))


def _gdn_paired(proj, gates, gates_t, gvecs, conv_w, norm_g, batch, nc, L, nb, side_cast):
    assert 2 * L == GATE_PAD and nc > 1
    cast_specs, cast_periods = _side_cast_specs(side_cast, (batch // nb) * nc, nc)
    def tok(width, col_block):
        return pl.BlockSpec((nb, L, width), lambda b, c: (b, c, col_block))

    vec_r = pl.BlockSpec((1, GATE_PAD), lambda b, c: (0, 0))
    tab = pl.BlockSpec((PAIR_ROWS, GATE_PAD), lambda b, c: (0, 0))
    gb_row, _, al_row, _ = gvecs
    pair = lambda vec: jnp.repeat(vec.reshape(GATE_PAD)[:2 * PAIR_ROWS], L).reshape(PAIR_ROWS, GATE_PAD)
    in_specs = [tok(4 * G_WIDTH, 1)]
    in_specs += [tok(GATE_PAD, 0), vec_r, vec_r,
                 pl.BlockSpec((nb, 1, PAIR_ROWS, GATE_PAD), lambda b, c: (b, c, 0, 0)), tab, tab,
                 pl.BlockSpec((CONV_W, 3 * G_WIDTH), lambda b, c: (0, 0)),
                 pl.BlockSpec((1, G_WIDTH), lambda b, c: (0, 0))]
    args = [proj, gates, gb_row, al_row,
            gates_t.reshape(batch, nc, GATE_PAD // 2, GATE_PAD), pair(gb_row), pair(al_row),
            conv_w, norm_g.reshape(1, G_WIDTH)]
    s_spec = pl.BlockSpec((nb, G_HEADS, G_HEAD_DIM, G_HEAD_DIM), lambda b, c: (b, 0, 0, 0))
    conv_spec = pl.BlockSpec((nb, CONV_HIST, 3 * G_WIDTH), lambda b, c: (b, 0, 0))
    outs = pl.pallas_call(
        functools.partial(_gdn_paired_kernel, L=L, nb=nb, nc=nc, cast_periods=cast_periods),
        grid=(batch // nb, nc),
        in_specs=in_specs + cast_specs,
        out_specs=[tok(G_WIDTH, 0), s_spec, conv_spec] + cast_specs,
        out_shape=[jax.ShapeDtypeStruct(proj.shape[:-1] + (G_WIDTH,), _scan_out_dtype(L)),
                   jax.ShapeDtypeStruct((batch, G_HEADS, G_HEAD_DIM, G_HEAD_DIM), F32),
                   jax.ShapeDtypeStruct((batch, CONV_HIST, 3 * G_WIDTH), F32)]
        + [jax.ShapeDtypeStruct(w.shape, BF16) for w in side_cast],
        scratch_shapes=[pltpu.VMEM((nb, SUBLANES + L, 3 * G_WIDTH), F32)],
        compiler_params=_cparams(("arbitrary", "arbitrary")),
        name="gdn",
    )(*args, *side_cast)
    return outs[:3], outs[3:]


def _gdn(proj, gates, gates_t, gvecs, conv_w, norm_g, batch, nc, L, nb, init, side_cast=()):
    if init is None and L > SMALL_L:
        return _gdn_paired(proj, gates, gates_t, gvecs, conv_w, norm_g, batch, nc, L, nb, side_cast)
    assert L <= SMALL_L and not side_cast
    has_init = init is not None
    in_specs = [_tok_spec(L, nb, nc, 4 * G_WIDTH, 1)] + _gate_specs(L, nb, nc)
    in_specs += [pl.BlockSpec((CONV_W, 3 * G_WIDTH), lambda b, c: (0, 0)),
                 pl.BlockSpec((1, G_WIDTH), lambda b, c: (0, 0))]
    s_spec = pl.BlockSpec((nb, G_HEADS, G_HEAD_DIM, G_HEAD_DIM), lambda b, c: (b, 0, 0, 0))
    conv_spec = pl.BlockSpec((nb, CONV_HIST, 3 * G_WIDTH), lambda b, c: (b, 0, 0))
    args = [proj, gates, gates_t, *gvecs, conv_w, norm_g.reshape(1, G_WIDTH)]
    if has_init:
        in_specs += [s_spec, conv_spec]
        args += list(init)
    outs = pl.pallas_call(
        functools.partial(_gdn_kernel, L=L, nb=nb, nc=nc, has_init=has_init),
        grid=(batch // nb, nc),
        in_specs=in_specs,
        out_specs=[_tok_spec(L, nb, nc, G_WIDTH, 0), s_spec, conv_spec],
        out_shape=[jax.ShapeDtypeStruct(proj.shape[:-1] + (G_WIDTH,), _scan_out_dtype(L)),
                   jax.ShapeDtypeStruct((batch, G_HEADS, G_HEAD_DIM, G_HEAD_DIM), F32),
                   jax.ShapeDtypeStruct((batch, CONV_HIST, 3 * G_WIDTH), F32)],
        scratch_shapes=[pltpu.VMEM((nb, 2 * SUBLANES, 3 * G_WIDTH), F32)],
        compiler_params=_cparams(("arbitrary", "arbitrary")),
        name="gdn",
    )(*args)
    return outs, []


def _layer_norm(y, g, b):
    mu = jnp.mean(y, axis=-1, keepdims=True)
    yc = y - mu
    var = jnp.mean(yc * yc, axis=-1, keepdims=True)
    return yc * lax.rsqrt(var + LN_EPS) * g + b


def _outproj_kernel(hm_ref, hg_ref, x_ref, gt_ref, w_ref, g_ref, b_ref, o_ref):
    tm = x_ref.shape[0]
    half = tm // OUTPROJ_SLABS
    for r in range(OUTPROJ_SLABS):
        rows = slice(r * half, (r + 1) * half)
        mix = (_dot(hm_ref[rows, :].astype(BF16), w_ref[0:M_WIDTH, :])
               + _dot(hg_ref[rows, :].astype(BF16), w_ref[M_WIDTH:M_WIDTH + G_WIDTH, :]))
        gt = gt_ref[0] if gt_ref.shape[1] == 1 else gt_ref[0, rows, :]
        y = DEEPNORM_ALPHA * x_ref[rows, :] + (1.0 + gt) * mix
        o_ref[rows, :] = _layer_norm(y, g_ref[...], b_ref[...])


def _outproj(hm, hg, x2d, gt, w_out, ln_g, ln_b, tm, tiles_per_mod):
    m, d = x2d.shape
    vec = pl.BlockSpec((1, d), lambda i: (0, 0))
    return pl.pallas_call(
        _outproj_kernel,
        grid=(m // tm,),
        in_specs=[pl.BlockSpec((tm, M_WIDTH), lambda i: (i, 0)),
                  pl.BlockSpec((tm, G_WIDTH), lambda i: (i, 0)),
                  pl.BlockSpec((tm, d), lambda i: (i, 0)),
                  _mod_spec(gt, tiles_per_mod, 1),
                  pl.BlockSpec((d, d), lambda i: (0, 0)),
                  vec, vec],
        out_specs=pl.BlockSpec((tm, d), lambda i: (i, 0)),
        out_shape=jax.ShapeDtypeStruct((m, d), F32),
        compiler_params=_cparams(("arbitrary",)),
        name="outproj",
    )(hm, hg, x2d, gt, w_out, ln_g.reshape(1, d), ln_b.reshape(1, d))


def _ffn_kernel(x_ref, sc_ref, sh_ref, gt_ref, wg_ref, wu_ref, wd_ref, g_ref, b_ref, o_ref, h_scr, acc):
    f = pl.program_id(1)

    @pl.when(f == 0)
    def _():
        h_scr[...] = (x_ref[...] * (1.0 + sc_ref[0]) + sh_ref[0]).astype(BF16)
        acc[...] = jnp.zeros_like(acc)

    h = h_scr[...]
    gate = _dot(h, wg_ref[...])
    up = _dot(h, wu_ref[...])
    act = (gate * jax.nn.sigmoid(gate) * up).astype(BF16)
    acc[...] += _dot(act, wd_ref[...])

    @pl.when(f == pl.num_programs(1) - 1)
    def _():
        y = DEEPNORM_ALPHA * x_ref[...] + (1.0 + gt_ref[0]) * acc[...]
        o_ref[...] = _layer_norm(y, g_ref[...], b_ref[...])


def _ffn(x2d, sc, sh, gt, w_gu, w_down, ln_g, ln_b, tm, tiles_per_mod):
    m, d = x2d.shape
    tf = TF_FFN
    nf = D_FF // tf
    mod_spec = _mod_spec(sc, tiles_per_mod, 2)
    vec = pl.BlockSpec((1, d), lambda i, f: (0, 0))
    return pl.pallas_call(
        _ffn_kernel,
        grid=(m // tm, nf),
        in_specs=[pl.BlockSpec((tm, d), lambda i, f: (i, 0)),
                  mod_spec, mod_spec, mod_spec,
                  pl.BlockSpec((d, tf), lambda i, f: (0, f)),
                  pl.BlockSpec((d, tf), lambda i, f: (0, nf + f)),
                  pl.BlockSpec((tf, d), lambda i, f: (f, 0)),
                  vec, vec],
        out_specs=pl.BlockSpec((tm, d), lambda i, f: (i, 0)),
        out_shape=jax.ShapeDtypeStruct((m, d), F32),
        scratch_shapes=[pltpu.VMEM((tm, d), BF16), pltpu.VMEM((tm, d), F32)],
        compiler_params=_cparams(("arbitrary", "arbitrary")),
        name="ffn",
    )(x2d, sc, sh, gt, w_gu, w_gu, w_down, ln_g.reshape(1, d), ln_b.reshape(1, d))


def _layer(x, ada6, weights, init_m, init_g, nb):
    batch, seq, d = x.shape
    L = CHUNK if seq % CHUNK == 0 else seq
    nc = seq // L
    m = batch * seq
    x2d = x.reshape(m, d)

    def tiling(tm_cap):
        tm = min(tm_cap, m)
        if seq % tm == 0:
            return tm, seq // tm, lambda a: a.reshape(batch, 1, d)
        return tm, 1, lambda a: jnp.repeat(a, seq, axis=0).reshape(m // tm, tm, d)

    sh1, sc1, gt1, sh2, sc2, gt2 = ada6
    tm, tpm, mod = tiling(TM_INPROJ)
    proj, gates = _inproj(x2d, mod(sc1), mod(sh1), weights["w_in_main"], weights["w_in_gate"], tm, tpm)
    gates_t = jnp.swapaxes(gates.reshape(batch, nc, L, GATE_PAD), 2, 3)
    proj_v, gates_v = _tok_view(proj, batch, nc), _tok_view(gates, batch, nc)
    gvecs = weights["gvecs"]
    hm, c1, n1, m1 = _mlstm(proj_v, gates_v, gates_t, gvecs, weights["m_norm_g"], batch, nc, L, nb["mlstm"],
                            init_m)
    pending = [name for name in ("w_out", "w_gu", "w_down") if weights[name].dtype != BF16]
    (hg, s1, conv1), converted = _gdn(proj_v, gates_v, gates_t, gvecs, weights["conv_w"], weights["g_norm_g"],
                                      batch, nc, L, nb["gdn"], init_g, [weights[name] for name in pending])
    weights.update(zip(pending, converted))
    tm, tpm, mod = tiling(TM_OUTPROJ)
    x1 = _outproj(hm.reshape(m, M_WIDTH), hg.reshape(m, G_WIDTH), x2d, mod(gt1), weights["w_out"],
                  weights["ln1_g"], weights["ln1_b"], tm, tpm)
    tm, tpm, mod = tiling(TM_FFN)
    y = _ffn(x1, mod(sc2), mod(sh2), mod(gt2), weights["w_gu"], weights["w_down"], weights["ln2_g"],
             weights["ln2_b"], tm, tpm)
    return y.reshape(batch, seq, d), c1, n1, m1[:, 0, :M_HEADS], s1, conv1


def kernel(x_prompt, x_sample, state_mlstm_C, state_mlstm_n, state_mlstm_m, state_gdn_S, state_gdn_conv,
           c_prompt, c_sample, w_ada, b_ada, w_in, m_i_bias, m_f_bias, m_norm_g, conv_w, g_dt_bias,
           g_A_log, g_norm_g, w_out, ln1_g, ln1_b, w_gu, w_down, ln2_g, ln2_b):
    bp, _, d = x_prompt.shape
    bs = x_sample.shape[0]

    ada = _ada(jnp.concatenate([c_prompt, c_sample], axis=0), w_ada, b_ada)
    ada6 = [ada[:, i * d:(i + 1) * d] for i in range(6)]

    gate_bias = jnp.zeros((GATE_PAD,), F32)
    gate_bias = gate_bias.at[GI0:GI0 + M_HEADS].set(m_i_bias).at[GF0:GF0 + M_HEADS].set(m_f_bias)
    gate_bias = gate_bias.at[GA0:GA0 + G_HEADS].set(g_dt_bias)
    a_log = jnp.zeros((GATE_PAD,), F32).at[GA0:GA0 + G_HEADS].set(g_A_log)
    weights = {
        "w_in_main": _cast_bf16(w_in.T, MAIN_COLS, d),
        "w_in_gate": jnp.pad(w_in.T[MAIN_COLS:], ((0, GATE_PAD - GATE_COLS), (0, 0))).astype(BF16),
        "gvecs": (gate_bias.reshape(1, GATE_PAD), gate_bias.reshape(GATE_PAD, 1),
                  a_log.reshape(1, GATE_PAD), a_log.reshape(GATE_PAD, 1)),
        "m_norm_g": m_norm_g, "conv_w": conv_w, "g_norm_g": g_norm_g,
        "w_out": w_out, "w_gu": w_gu, "w_down": w_down,
        "ln1_g": ln1_g, "ln1_b": ln1_b, "ln2_g": ln2_g, "ln2_b": ln2_b,
    }

    y_p, p_c, p_n, p_m, p_s, p_conv = _layer(x_prompt, [a[:bp] for a in ada6], weights, None, None,
                                             SEQS_PER_STEP_PROMPT)
    m0 = jnp.pad(state_mlstm_m, ((0, 0), (0, GATE_PAD - M_HEADS))).reshape(bs, 1, GATE_PAD)
    y_s, s_c, s_n, s_m, s_s, s_conv = _layer(
        x_sample, [a[bp:] for a in ada6], weights, (state_mlstm_C, state_mlstm_n, m0),
        (state_gdn_S, state_gdn_conv), SEQS_PER_STEP_SAMPLE)
    return (y_p, y_s, p_c, p_n, p_m, p_s, p_conv, s_c, s_n, s_m, s_s, s_conv)
```

```python
import functools

import jax
import jax.numpy as jnp
from jax import lax
from jax.experimental import pallas as pl
from jax.experimental.pallas import tpu as pltpu

F32 = jnp.float32
BF16 = jnp.bfloat16

D_MODEL = 2048
M_HEADS = 4
M_HEAD_DIM = 256
M_WIDTH = M_HEADS * M_HEAD_DIM
G_HEADS = 8
G_HEAD_DIM = 128
G_WIDTH = G_HEADS * G_HEAD_DIM
CONV_W = 4
CHUNK = 64
D_FF = 5632
MAIN_COLS = 4 * M_WIDTH + 3 * G_WIDTH + G_WIDTH
GATE_COLS = 2 * M_HEADS + 2 * G_HEADS
GATE_PAD = 128
DEEPNORM_ALPHA = 2.0 ** 0.25
LN_EPS = 1e-5
RMS_EPS = 1e-6
GI0, GF0, GB0, GA0 = 0, M_HEADS, 2 * M_HEADS, 2 * M_HEADS + G_HEADS

SUBLANES = 8
VMEM_LIMIT_BYTES = 56 * 1024 * 1024
TM_INPROJ, TM_OUTPROJ, TM_FFN = 1024, 512, 512
TN_INPROJ, TN_ADA, TF_FFN = 1024, 1024, 512
OUTPROJ_SLABS = 2
SEQS_PER_STEP_PROMPT = {"mlstm": 1, "gdn": 2}
SEQS_PER_STEP_SAMPLE = {"mlstm": 4, "gdn": 8}
MLSTM_GROUP_LONG, MLSTM_GROUP_SHORT = 1, 2
SMALL_L = 8
PAIR_ROWS = 16


def _cparams(sem):
    return pltpu.CompilerParams(dimension_semantics=sem, vmem_limit_bytes=VMEM_LIMIT_BYTES)


def _dot(a, b):
    return jnp.dot(a, b, preferred_element_type=F32)


def _dot_nt(a, b):
    return lax.dot_general(a, b, (((1,), (1,)), ((), ())), preferred_element_type=F32)


def _dot_tn(a, b):
    return lax.dot_general(a, b, (((0,), (0,)), ((), ())), preferred_element_type=F32)


def _split2(x):
    hi = x.astype(BF16)
    return hi, (x - hi.astype(F32)).astype(BF16)


def _split3(x):
    hi = x.astype(BF16)
    r = x - hi.astype(F32)
    mid = r.astype(BF16)
    return hi, mid, (r - mid.astype(F32)).astype(BF16)


def _dot3_stacked(a, b):
    m = a[0].shape[0]
    both = _dot(jnp.concatenate([a[0], a[1]], axis=0), b[0])
    return both[0:m] + (_dot(a[0], b[1]) + both[m:2 * m])


def _mm_small(a, b):
    out = a[:, 0:1] * b[0:1, :]
    for i in range(1, a.shape[1]):
        out = out + a[:, i:i + 1] * b[i:i + 1, :]
    return out


def _tok_get(ref, b, L, cols):
    if len(ref.shape) == 3:
        return ref[b, :, cols]
    return ref[b * L:(b + 1) * L, cols]


def _tok_set(ref, b, L, cols, val):
    if len(ref.shape) == 3:
        ref[b, :, cols] = val
    else:
        ref[b * L:(b + 1) * L, cols] = val


ALL = slice(None)


def _ada_kernel(c_ref, w_ref, b_ref, o_ref):
    c = c_ref[...]
    a = (c * jax.nn.sigmoid(c)).astype(BF16)
    o_ref[...] = _dot(a, w_ref[...].astype(BF16)) + b_ref[...]


def _ada(c_all, w_ada, b_ada):
    n_rows, d = c_all.shape
    n_cols = w_ada.shape[1]
    tn = TN_ADA
    return pl.pallas_call(
        _ada_kernel,
        grid=(n_cols // tn,),
        in_specs=[pl.BlockSpec((n_rows, d), lambda j: (0, 0)),
                  pl.BlockSpec((d, tn), lambda j: (0, j)),
                  pl.BlockSpec((1, tn), lambda j: (0, j))],
        out_specs=pl.BlockSpec((n_rows, tn), lambda j: (0, j)),
        out_shape=jax.ShapeDtypeStruct((n_rows, n_cols), F32),
        compiler_params=_cparams(("arbitrary",)),
        name="ada",
    )(c_all, w_ada, b_ada.reshape(1, n_cols))


def _inproj_kernel(x_ref, sc_ref, sh_ref, wt_ref, wgt_ref, o_ref, og_ref, h_scr):
    @pl.when(pl.program_id(1) == 0)
    def _():
        h = (x_ref[...] * (1.0 + sc_ref[0]) + sh_ref[0]).astype(BF16)
        h_scr[...] = h
        og_ref[...] = _dot_nt(h, wgt_ref[...])

    o_ref[...] = _dot_nt(h_scr[...], wt_ref[...].astype(BF16))


def _mod_spec(mod, tiles_per_mod, grid_rank):
    _, mod_rows, d = mod.shape
    if grid_rank == 1:
        index_map = lambda i: (i // tiles_per_mod, 0, 0)
    else:
        index_map = lambda i, j: (i // tiles_per_mod, 0, 0)
    if mod_rows == 1:
        return pl.BlockSpec((1, 1, d), index_map)
    return pl.BlockSpec((1, mod_rows, d), index_map, pipeline_mode=pl.Buffered(1))


def _inproj(x2d, sc, sh, w_main, w_gate, tm, tiles_per_mod):
    m, d = x2d.shape
    tn = TN_INPROJ
    mod_spec = _mod_spec(sc, tiles_per_mod, 2)
    return pl.pallas_call(
        _inproj_kernel,
        grid=(m // tm, MAIN_COLS // tn),
        in_specs=[pl.BlockSpec((tm, d), lambda i, j: (i, 0)),
                  mod_spec, mod_spec,
                  pl.BlockSpec((tn, d), lambda i, j: (j, 0)),
                  pl.BlockSpec((GATE_PAD, d), lambda i, j: (0, 0))],
        out_specs=[pl.BlockSpec((tm, tn), lambda i, j: (i, j)),
                   pl.BlockSpec((tm, GATE_PAD), lambda i, j: (i, 0))],
        out_shape=[jax.ShapeDtypeStruct((m, MAIN_COLS), F32),
                   jax.ShapeDtypeStruct((m, GATE_PAD), F32)],
        scratch_shapes=[pltpu.VMEM((tm, d), BF16)],
        compiler_params=_cparams(("arbitrary", "arbitrary")),
        name="inproj",
    )(x2d, sc, sh, w_main, w_gate)


def _gate_tables(g, bias, alog, gid, delta_rule):
    x = g + bias
    is_f = (gid >= GF0) & (gid < GB0)
    log_f = jax.nn.log_sigmoid(x)
    if not delta_rule:
        return x, jnp.where(is_f, log_f, 0.0), None
    is_a = (gid >= GA0) & (gid < GA0 + G_HEADS)
    log_a = -jnp.exp(alog) * jax.nn.softplus(x)
    inc = jnp.where(is_f, log_f, jnp.where(is_a, log_a, 0.0))
    return x, inc, jax.nn.sigmoid(x)


def _chunk_masks(L):
    row = lax.broadcasted_iota(jnp.int32, (L, L), 0)
    col = lax.broadcasted_iota(jnp.int32, (L, L), 1)
    return row >= col, row > col, row == col


def _gates_col_form(g_col, gb_row_ref, al_row_ref, L, delta_rule=True):
    causal, _, _ = _chunk_masks(L)
    tril = causal.astype(F32)
    gid_c = lax.broadcasted_iota(jnp.int32, (L, GATE_PAD), 1)
    x_c, inc_c, beta_c = _gate_tables(g_col, gb_row_ref[...], al_row_ref[...], gid_c, delta_rule)
    if L <= SMALL_L:
        cum_c = _mm_small(tril, inc_c)
    else:
        tril_b = tril.astype(BF16)
        c1, c2, c3 = _split3(inc_c)
        cum_c = _dot(tril_b, c1) + (_dot(tril_b, c2) + _dot(tril_b, c3))
    return x_c, cum_c, beta_c


def _gates_both_forms(g_col, g_row, gb_row_ref, gb_col_ref, al_row_ref, al_col_ref, L, delta_rule=True):
    x_c, cum_c, beta_c = _gates_col_form(g_col, gb_row_ref, al_row_ref, L, delta_rule)
    triu = (lax.broadcasted_iota(jnp.int32, (L, L), 0) <= lax.broadcasted_iota(jnp.int32, (L, L), 1)).astype(F32)
    n_rows = GATE_PAD if delta_rule else SUBLANES
    assert GB0 <= n_rows
    gid_r = lax.broadcasted_iota(jnp.int32, (n_rows, L), 0)
    x_r, inc_r, _ = _gate_tables(g_row[0:n_rows], gb_col_ref[0:n_rows, :], al_col_ref[0:n_rows, :], gid_r, delta_rule)
    if L <= SMALL_L:
        cum_r = _mm_small(inc_r, triu)
    else:
        triu_b = triu.astype(BF16)
        r1, r2, r3 = _split3(inc_r)
        cum_r = _dot(r1, triu_b) + (_dot(r2, triu_b) + _dot(r3, triu_b))
    return x_c, cum_c, beta_c, x_r, cum_r


def _rms_gate(h, gain, gate):
    return h * lax.rsqrt(jnp.mean(h * h, axis=-1, keepdims=True) + RMS_EPS) * gain * gate


def _mlstm_kernel(*refs, L, nb, nc, has_init, group):
    if has_init:
        (qkvo_ref, gcol_ref, grow_ref, gb_row_ref, gb_col_ref, al_row_ref, al_col_ref,
         ng_ref, c0_ref, n0_ref, m0_ref, h_ref, c_ref, n_ref, m_ref) = refs
    else:
        (qkvo_ref, gcol_ref, grow_ref, gb_row_ref, gb_col_ref, al_row_ref, al_col_ref,
         ng_ref, h_ref, c_ref, n_ref, m_ref) = refs

    if has_init and nc == 1:
        c_src, n_src, m_src = c0_ref, n0_ref, m0_ref
    else:
        c_src, n_src, m_src = c_ref, n_ref, m_ref

        @pl.when(pl.program_id(1) == 0)
        def _():
            if has_init:
                c_ref[...] = c0_ref[...]
                n_ref[...] = n0_ref[...]
                m_ref[...] = m0_ref[...]
            else:
                c_ref[...] = jnp.zeros_like(c_ref)
                n_ref[...] = jnp.zeros_like(n_ref)
                m_ref[...] = jnp.zeros_like(m_ref)

    for b0 in range(0, nb, group):
        _mlstm_group(range(b0, min(b0 + group, nb)), L, qkvo_ref, gcol_ref, grow_ref, gb_row_ref,
                     gb_col_ref, al_row_ref, al_col_ref, ng_ref, c_src, n_src, m_src, h_ref, c_ref, n_ref, m_ref)


def _mlstm_group(seqs, L, qkvo_ref, gcol_ref, grow_ref, gb_row_ref, gb_col_ref, al_row_ref,
                 al_col_ref, ng_ref, c_src, n_src, m_src, h_ref, c_ref, n_ref, m_ref):
    causal, _, _ = _chunk_masks(L)
    lane = lax.broadcasted_iota(jnp.int32, (1, GATE_PAD), 1)
    scale = M_HEAD_DIM ** -0.5
    gates = {b: _gates_both_forms(_tok_get(gcol_ref, b, L, ALL), grow_ref[b, 0, 0:SUBLANES, :], gb_row_ref,
                                  gb_col_ref, al_row_ref, al_col_ref, L, delta_rule=False) for b in seqs}
    m_all = {b: m_src[b] for b in seqs}
    probs = [(b, h) for b in seqs for h in range(M_HEADS)]

    def cols(h, part=0):
        return slice(part * M_WIDTH + h * M_HEAD_DIM, part * M_WIDTH + (h + 1) * M_HEAD_DIM)

    q = [_tok_get(qkvo_ref, b, L, cols(h, 0)) for b, h in probs]
    k = [_tok_get(qkvo_ref, b, L, cols(h, 1)) * scale for b, h in probs]
    vb = [_tok_get(qkvo_ref, b, L, cols(h, 2)).astype(BF16) for b, h in probs]
    qb = [x.astype(BF16) for x in q]
    kb = [x.astype(BF16) for x in k]
    ig_c = [gates[b][0][:, GI0 + h:GI0 + h + 1] for b, h in probs]
    ig_r = [gates[b][3][GI0 + h:GI0 + h + 1, :] for b, h in probs]
    bt_c = [gates[b][1][:, GF0 + h:GF0 + h + 1] for b, h in probs]
    bt_r = [gates[b][4][GF0 + h:GF0 + h + 1, :] for b, h in probs]
    m0 = [jnp.sum(jnp.where(lane == h, m_all[b], 0.0), axis=1, keepdims=True) for b, h in probs]
    n_p = len(probs)
    rng = range(n_p)

    log_d = [jnp.where(causal, bt_c[i] - bt_r[i] + ig_r[i], -jnp.inf) for i in rng]
    inter = [bt_c[i] + m0[i] for i in rng]
    m_t = [jnp.maximum(inter[i], jnp.max(log_d[i], axis=1, keepdims=True)) for i in rng]
    inter_w = [jnp.exp(inter[i] - m_t[i]) for i in rng]
    qk = [_dot_nt(qb[i], kb[i]) for i in rng]
    c_old = [c_src[b, h] for b, h in probs]
    n_old = [n_src[b, h:h + 1, :] for b, h in probs]
    qc = [_dot(qb[i], c_old[i].astype(BF16)) for i in rng]
    s = [qk[i] * jnp.exp(log_d[i] - m_t[i]) for i in rng]
    sv = [_dot(s[i].astype(BF16), vb[i]) for i in rng]

    b_last = [bt_c[i][L - 1:L, :] for i in rng]
    m_new = [jnp.maximum(b_last[i] + m0[i], jnp.max(b_last[i] - bt_r[i] + ig_r[i], axis=1, keepdims=True))
             for i in rng]
    kw = [k[i] * jnp.exp(b_last[i] - bt_c[i] + ig_c[i] - m_new[i]) for i in rng]
    decay = [jnp.exp(b_last[i] + m0[i] - m_new[i]) for i in rng]
    kv = [_dot_tn(kw[i].astype(BF16), vb[i]) for i in rng]

    m_next = dict(m_all)
    for i, (b, h) in enumerate(probs):
        num = inter_w[i] * qc[i] + sv[i]
        den = inter_w[i] * jnp.sum(q[i] * n_old[i], axis=1, keepdims=True) + jnp.sum(s[i], axis=1, keepdims=True)
        hh = num / jnp.maximum(jnp.abs(den), jnp.exp(-m_t[i]))
        c_ref[b, h] = decay[i] * c_old[i] + kv[i]
        n_ref[b, h:h + 1, :] = decay[i] * n_old[i] + jnp.sum(kw[i], axis=0, keepdims=True)
        m_next[b] = jnp.where(lane == h, m_new[i], m_next[b])
        gate = jax.nn.sigmoid(_tok_get(qkvo_ref, b, L, cols(h, 3)))
        _tok_set(h_ref, b, L, cols(h), _rms_gate(hh, ng_ref[:, cols(h)], gate).astype(h_ref.dtype))
    for b in seqs:
        m_ref[b] = m_next[b]


def _tok_spec(L, nb, nc, width, col_block):
    if nc == 1:
        return pl.BlockSpec((nb * L, width), lambda b, c: (b, col_block))
    return pl.BlockSpec((nb, L, width), lambda b, c: (b, c, col_block))


def _tok_view(x2d, batch, nc):
    return x2d if nc == 1 else x2d.reshape(batch, x2d.shape[0] // batch, x2d.shape[1])


def _gate_specs(L, nb, nc):
    vec_r = pl.BlockSpec((1, GATE_PAD), lambda b, c: (0, 0))
    vec_c = pl.BlockSpec((GATE_PAD, 1), lambda b, c: (0, 0))
    return [_tok_spec(L, nb, nc, GATE_PAD, 0),
            pl.BlockSpec((nb, 1, GATE_PAD, L), lambda b, c: (b, c, 0, 0)),
            vec_r, vec_c, vec_r, vec_c]


def _scan_out_dtype(L):
    return BF16 if L % 16 == 0 else F32


BF16_ROWS = 16


def _slab_plan(rows, n_steps):
    period = 1
    while (rows * period) % (n_steps * BF16_ROWS):
        period *= 2
        assert period <= n_steps
    return rows * period // n_steps, period


def _side_cast_specs(side_cast, n_steps, nc):
    specs, periods = [], []
    for w in side_cast:
        slab, period = _slab_plan(w.shape[0], n_steps)
        specs.append(pl.BlockSpec((slab, w.shape[1]), lambda b, c, period=period: ((b * nc + c) // period, 0)))
        periods.append(period)
    return specs, tuple(periods)


def _run_side_casts(cast_src, cast_dst, periods, nc):
    step = pl.program_id(0) * nc + pl.program_id(1)
    for src, dst, period in zip(cast_src, cast_dst, periods):
        if period == 1:
            dst[...] = src[...].astype(BF16)
        else:
            @pl.when(step % period == 0)
            def _(src=src, dst=dst):
                dst[...] = src[...].astype(BF16)


def _mlstm(proj, gates, gates_t, gvecs, norm_g, batch, nc, L, nb, init):
    has_init = init is not None
    in_specs = [_tok_spec(L, nb, nc, 4 * M_WIDTH, 0)] + _gate_specs(L, nb, nc)
    in_specs.append(pl.BlockSpec((1, M_WIDTH), lambda b, c: (0, 0)))
    c_spec = pl.BlockSpec((nb, M_HEADS, M_HEAD_DIM, M_HEAD_DIM), lambda b, c: (b, 0, 0, 0))
    n_spec = pl.BlockSpec((nb, M_HEADS, M_HEAD_DIM), lambda b, c: (b, 0, 0))
    m_spec = pl.BlockSpec((nb, 1, GATE_PAD), lambda b, c: (b, 0, 0))
    args = [proj, gates, gates_t, *gvecs, norm_g.reshape(1, M_WIDTH)]
    if has_init:
        in_specs += [c_spec, n_spec, m_spec]
        args += list(init)
    return pl.pallas_call(
        functools.partial(_mlstm_kernel, L=L, nb=nb, nc=nc, has_init=has_init,
                          group=MLSTM_GROUP_SHORT if L <= SMALL_L else MLSTM_GROUP_LONG),
        grid=(batch // nb, nc),
        in_specs=in_specs,
        out_specs=[_tok_spec(L, nb, nc, M_WIDTH, 0), c_spec, n_spec, m_spec],
        out_shape=[jax.ShapeDtypeStruct(proj.shape[:-1] + (M_WIDTH,), _scan_out_dtype(L)),
                   jax.ShapeDtypeStruct((batch, M_HEADS, M_HEAD_DIM, M_HEAD_DIM), F32),
                   jax.ShapeDtypeStruct((batch, M_HEADS, M_HEAD_DIM), F32),
                   jax.ShapeDtypeStruct((batch, 1, GATE_PAD), F32)],
        compiler_params=_cparams(("arbitrary", "arbitrary")),
        name="mlstm",
    )(*args)


def _inv_unit_lower_small(a_list, L):
    _, _, eye = _chunk_masks(L)
    p = [-a for a in a_list]
    t = [eye.astype(F32) + x for x in p]
    for _ in range(max(1, (L - 1).bit_length()) - 1):
        p = [_mm_small(x, x) for x in p]
        t = [ti + _mm_small(ti, pi) for ti, pi in zip(t, p)]
    return t


def _gate_rows_paired(gp, bias_p, alog_p, L):
    rid = lax.broadcasted_iota(jnp.int32, gp.shape, 0)
    x = gp + bias_p
    is_f = (rid >= GF0 // 2) & (rid < GB0 // 2)
    is_a = (rid >= GA0 // 2) & (rid < (GA0 + G_HEADS) // 2)
    inc = jnp.where(is_f, jax.nn.log_sigmoid(x), jnp.where(is_a, -jnp.exp(alog_p) * jax.nn.softplus(x), 0.0))
    r = lax.broadcasted_iota(jnp.int32, (2 * L, 2 * L), 0)
    c = lax.broadcasted_iota(jnp.int32, (2 * L, 2 * L), 1)
    triu2 = (((r < L) == (c < L)) & (r <= c)).astype(BF16)
    p1, p2, p3 = _split3(inc)
    return _dot(p1, triu2) + (_dot(p2, triu2) + _dot(p3, triu2))


def _gdn_heads_paired(qn, kn, vv, gates_c, cum_p, s_ref, xz_ref, h_ref, ng_ref, L, nb):
    dh = G_HEAD_DIM
    row =lax.broadcasted_iota(jnp.int32, (L, 2 * L), 0)
    lane = lax.broadcasted_iota(jnp.int32, (L, 2 * L), 1)
    left = lane < L
    colp = jnp.where(left, lane, lane - L)
    causal_p, strict_p, eye_p = row >= colp, row > colp, (row == colp).astype(F32)
    left2 = lax.broadcasted_iota(jnp.int32, (L, 2 * dh), 1) < dh
    lane4 = lax.broadcasted_iota(jnp.int32, (L, 4 * dh), 1)
    first4 = (lane4 // dh) % 2 == 0

    def bdiag(y, first):
        z = jnp.zeros_like(y)
        return jnp.concatenate([jnp.where(first, y, z), jnp.where(first, z, y)], axis=0)

    pairs = [(b, p) for b in range(nb) for p in range(G_HEADS // 2)]
    rng = range(len(pairs))
    q2 = [qn[b][:, 2 * p * dh:(2 * p + 2) * dh] for b, p in pairs]
    k2 = [kn[b][:, 2 * p * dh:(2 * p + 2) * dh] for b, p in pairs]
    v2 = [vv[b][:, 2 * p * dh:(2 * p + 2) * dh] for b, p in pairs]
    q2b = [x.astype(BF16) for x in q2]
    k2b = [x.astype(BF16) for x in k2]
    kbd = [bdiag(x, left2) for x in k2b]
    b_ca = [gates_c[b][1][:, GA0 + 2 * p:GA0 + 2 * p + 1] for b, p in pairs]
    b_cb = [gates_c[b][1][:, GA0 + 2 * p + 1:GA0 + 2 * p + 2] for b, p in pairs]
    bet_a = [gates_c[b][2][:, GB0 + 2 * p:GB0 + 2 * p + 1] for b, p in pairs]
    bet_b = [gates_c[b][2][:, GB0 + 2 * p + 1:GB0 + 2 * p + 2] for b, p in pairs]
    b_r = [cum_p[b][GA0 // 2 + p:GA0 // 2 + p + 1, :] for b, p in pairs]
    decay = [jnp.exp(jnp.where(causal_p, jnp.where(left, b_ca[i], b_cb[i]) - b_r[i], -jnp.inf)) for i in rng]
    kk = [_dot_nt(k2b[i], kbd[i]) for i in rng]
    qk = [(_dot_nt(q2b[i], kbd[i]) * decay[i]).astype(BF16) for i in rng]
    s_old = [(s_ref[b, 2 * p], s_ref[b, 2 * p + 1]) for b, p in pairs]
    zero = jnp.zeros((dh, dh), BF16)
    sbd = [jnp.concatenate([jnp.concatenate([sa.astype(BF16), zero], axis=1),
                            jnp.concatenate([zero, sb.astype(BF16)], axis=1)], axis=0) for sa, sb in s_old]
    qs = [_dot(q2b[i], sbd[i]) for i in rng]

    a = [jnp.where(strict_p, jnp.where(left, bet_a[i], bet_b[i]) * kk[i] * decay[i], 0.0) for i in rng]
    p = [-x for x in a]
    t = [eye_p + x for x in p]
    ps = [_split2(x) for x in p]
    p = [_dot3_stacked(ps[i], (bdiag(ps[i][0], left), bdiag(ps[i][1], left))) for i in rng]
    for lvl in range(1, (L - 1).bit_length()):
        last = lvl == (L - 1).bit_length() - 1
        ps = [_split2(x) for x in p]
        pbd = [(bdiag(hi, left), bdiag(lo, left)) for hi, lo in ps]
        if last:
            t = [t[i] + _dot3_stacked(_split2(t[i]), pbd[i]) for i in rng]
        else:
            ts = [_split2(x) for x in t]
            both = [_dot3_stacked((jnp.concatenate([ts[i][0], ps[i][0]], axis=0),
                                   jnp.concatenate([ts[i][1], ps[i][1]], axis=0)), pbd[i]) for i in rng]
            t = [t[i] + both[i][0:L] for i in rng]
            p = [both[i][L:2 * L] for i in rng]

    bet2 = [jnp.where(left2, bet_a[i], bet_b[i]) for i in rng]
    eb2 = [jnp.where(left2, jnp.exp(b_ca[i]), jnp.exp(b_cb[i])) for i in rng]
    rhs = [_split2(jnp.concatenate([bet2[i] * v2[i], (bet2[i] * eb2[i]) * k2[i]], axis=-1)) for i in rng]
    sol = [_dot3_stacked(_split2(t[i]), (bdiag(rhs[i][0], first4), bdiag(rhs[i][1], first4))) for i in rng]
    u = [sol[i][:, 0:2 * dh] - _dot(sol[i][:, 2 * dh:4 * dh].astype(BF16), sbd[i]) for i in rng]
    ub = [x.astype(BF16) for x in u]
    o = [eb2[i] * qs[i] + _dot(qk[i], bdiag(ub[i], left2)) for i in rng]
    bl_a = [b_ca[i][L - 1:L, :] for i in rng]
    bl_b = [b_cb[i][L - 1:L, :] for i in rng]
    wk = [(jnp.where(left2, jnp.exp(bl_a[i] - b_ca[i]), jnp.exp(bl_b[i] - b_cb[i])) * k2[i]).astype(BF16)
          for i in rng]
    ds = [_dot_tn(wk[i], ub[i]) for i in rng]
    for i, (b, p) in enumerate(pairs):
        for j, bl in enumerate((bl_a[i], bl_b[i])):
            h = 2 * p + j
            blk = slice(j * dh, (j + 1) * dh)
            cols = slice(h * dh, (h + 1) * dh)
            s_ref[b, h] = jnp.exp(bl) * s_old[i][j] + ds[i][blk, blk]
            z = _tok_get(xz_ref, b, L, slice(3 * G_WIDTH + h * dh, 3 * G_WIDTH + (h + 1) * dh))
            _tok_set(h_ref, b, L, cols,
                     _rms_gate(o[i][:, blk], ng_ref[:, cols], z * jax.nn.sigmoid(z)).astype(h_ref.dtype))


CONV_HIST = CONV_W - 1
CONV_BASE = SUBLANES - CONV_HIST


def _unit(x):
    return x * lax.rsqrt(jnp.sum(x * x, axis=-1, keepdims=True) + RMS_EPS)


def _gdn_conv_act(b, xz_ref, cw_ref, conv_ref, buf, L, carry):
    buf[b, SUBLANES:SUBLANES + L, :] = _tok_get(xz_ref, b, L, slice(0, 3 * G_WIDTH))
    if L % SUBLANES == 0:
        rows = buf[b, 0:SUBLANES + L, :]
        y = cw_ref[CONV_HIST:CONV_W, :] * rows[SUBLANES:, :]
        for shift in range(1, CONV_W):
            j = CONV_HIST - shift
            y = y + cw_ref[j:j + 1, :] * pltpu.roll(rows, shift, 0)[SUBLANES:, :]
    else:
        y = cw_ref[0:1, :] * buf[b, CONV_BASE:CONV_BASE + L, :]
        for j in range(1, CONV_W):
            y = y + cw_ref[j:j + 1, :] * buf[b, CONV_BASE + j:CONV_BASE + j + L, :]
    conv_ref[b] = buf[b, SUBLANES + L - CONV_HIST:SUBLANES + L, :]
    if carry:
        buf[b, 0:SUBLANES, :] = buf[b, L:L + SUBLANES, :]
    return y * jax.nn.sigmoid(y)


def _gdn_paired_kernel(*refs, L, nb, nc, cast_periods):
    xz_ref, gcol_ref, gb_row_ref, al_row_ref, gp_ref, gbp_ref, alp_ref, cw_ref, ng_ref = refs[:9]
    n_cast = len(cast_periods)
    cast_src = refs[9:9 + n_cast]
    h_ref, s_ref, conv_ref = refs[9 + n_cast:12 + n_cast]
    cast_dst = refs[12 + n_cast:12 + 2 * n_cast]
    buf = refs[12 + 2 * n_cast]
    _run_side_casts(cast_src, cast_dst, cast_periods, nc)

    @pl.when(pl.program_id(1) == 0)
    def _():
        buf[:, 0:SUBLANES, :] = jnp.zeros((nb, SUBLANES, 3 * G_WIDTH), F32)
        s_ref[...] = jnp.zeros_like(s_ref)

    scale = G_HEAD_DIM ** -0.5
    qn, kn, vv = [], [], []
    for b in range(nb):
        act = _gdn_conv_act(b, xz_ref, cw_ref, conv_ref, buf, L, True)
        qn.append(jnp.concatenate(
            [_unit(act[:, h * G_HEAD_DIM:(h + 1) * G_HEAD_DIM]) * scale for h in range(G_HEADS)], axis=-1))
        kn.append(jnp.concatenate(
            [_unit(act[:, G_WIDTH + h * G_HEAD_DIM:G_WIDTH + (h + 1) * G_HEAD_DIM]) for h in range(G_HEADS)],
            axis=-1))
        vv.append(act[:, 2 * G_WIDTH:3 * G_WIDTH])
    gates_c = [_gates_col_form(gcol_ref[b], gb_row_ref, al_row_ref, L) for b in range(nb)]
    cum_p = [_gate_rows_paired(gp_ref[b, 0], gbp_ref[...], alp_ref[...], L) for b in range(nb)]
    _gdn_heads_paired(qn, kn, vv, gates_c, cum_p, s_ref, xz_ref, h_ref, ng_ref, L, nb)


def _gdn_kernel(*refs, L, nb, nc, has_init):
    xz_ref, gcol_ref, grow_ref, gb_row_ref, gb_col_ref, al_row_ref, al_col_ref = refs[:7]
    if has_init:
        cw_ref, ng_ref, s0_ref, conv0_ref, h_ref, s_ref, conv_ref, buf = refs[7:]
    else:
        cw_ref, ng_ref, h_ref, s_ref, conv_ref, buf = refs[7:]

    def cols(h, off=0):
        return slice(off + h * G_HEAD_DIM, off + (h + 1) * G_HEAD_DIM)

    s_src = s0_ref if (has_init and nc == 1) else s_ref

    @pl.when(pl.program_id(1) == 0)
    def _():
        buf[:, 0:SUBLANES, :] = jnp.zeros((nb, SUBLANES, 3 * G_WIDTH), F32)
        if has_init:
            for b in range(nb):
                buf[b, CONV_BASE:SUBLANES, :] = conv0_ref[b]
            if nc > 1:
                s_ref[...] = s0_ref[...]
        else:
            s_ref[...] = jnp.zeros_like(s_ref)

    act = [_gdn_conv_act(b, xz_ref, cw_ref, conv_ref, buf, L, nc > 1) for b in range(nb)]
    causal, strict, _ = _chunk_masks(L)
    scale = G_HEAD_DIM ** -0.5
    gates = [_gates_both_forms(_tok_get(gcol_ref, b, L, ALL), grow_ref[b, 0], gb_row_ref, gb_col_ref,
                               al_row_ref, al_col_ref, L) for b in range(nb)]
    probs = [(b, h) for b in range(nb) for h in range(G_HEADS)]
    rng = range(len(probs))

    q = [_unit(act[b][:, h * G_HEAD_DIM:(h + 1) * G_HEAD_DIM]) * scale for b, h in probs]
    k = [_unit(act[b][:, G_WIDTH + h * G_HEAD_DIM:G_WIDTH + (h + 1) * G_HEAD_DIM]) for b, h in probs]
    v = [act[b][:, 2 * G_WIDTH + h * G_HEAD_DIM:2 * G_WIDTH + (h + 1) * G_HEAD_DIM] for b, h in probs]
    qb = [x.astype(BF16) for x in q]
    kb = [x.astype(BF16) for x in k]
    b_c = [gates[b][1][:, GA0 + h:GA0 + h + 1] for b, h in probs]
    b_r = [gates[b][4][GA0 + h:GA0 + h + 1, :] for b, h in probs]
    bet = [gates[b][2][:, GB0 + h:GB0 + h + 1] for b, h in probs]
    decay = [jnp.exp(jnp.where(causal, b_c[i] - b_r[i], -jnp.inf)) for i in rng]
    eb = [jnp.exp(b_c[i]) for i in rng]
    kk = [_dot_nt(kb[i], kb[i]) for i in rng]
    qk = [_dot_nt(qb[i], kb[i]) * decay[i] for i in rng]
    s_old = [s_src[b, h] for b, h in probs]
    sb = [x.astype(BF16) for x in s_old]
    qs = [_dot(qb[i], sb[i]) for i in rng]

    t = _inv_unit_lower_small([jnp.where(strict, bet[i] * kk[i] * decay[i], 0.0) for i in rng], L)
    rhs = [jnp.concatenate([bet[i] * v[i], (bet[i] * eb[i]) * k[i]], axis=-1) for i in rng]
    sol = [_mm_small(t[i], rhs[i]) for i in rng]
    u = [sol[i][:, 0:G_HEAD_DIM] - _dot(sol[i][:, G_HEAD_DIM:2 * G_HEAD_DIM].astype(BF16), sb[i]) for i in rng]
    ub = [x.astype(BF16) for x in u]
    o = [eb[i] * qs[i] + _dot(qk[i].astype(BF16), ub[i]) for i in rng]
    b_last = [b_c[i][L - 1:L, :] for i in rng]
    wk = [(jnp.exp(b_last[i] - b_c[i]) * k[i]).astype(BF16) for i in rng]
    ds = [_dot_tn(wk[i], ub[i]) for i in rng]
    for i, (b, h) in enumerate(probs):
        s_ref[b, h] = jnp.exp(b_last[i]) * s_old[i] + ds[i]
        z = _tok_get(xz_ref, b, L, cols(h, 3 * G_WIDTH))
        _tok_set(h_ref, b, L, cols(h),
                 _rms_gate(o[i], ng_ref[:, cols(h)], z * jax.nn.sigmoid(z)).astype(h_ref.dtype))


def _gdn_paired(proj, gates, gates_t, gvecs, conv_w, norm_g, batch, nc, L, nb, side_cast):
    assert 2 * L == GATE_PAD and nc > 1
    cast_specs, cast_periods = _side_cast_specs(side_cast, (batch // nb) * nc, nc)
    def tok(width, col_block):
        return pl.BlockSpec((nb, L, width), lambda b, c: (b, c, col_block))

    vec_r = pl.BlockSpec((1, GATE_PAD), lambda b, c: (0, 0))
    tab = pl.BlockSpec((PAIR_ROWS, GATE_PAD), lambda b, c: (0, 0))
    gb_row, _, al_row, _ = gvecs
    pair = lambda vec: jnp.repeat(vec.reshape(GATE_PAD)[:2 * PAIR_ROWS], L).reshape(PAIR_ROWS, GATE_PAD)
    in_specs = [tok(4 * G_WIDTH, 1)]
    in_specs += [tok(GATE_PAD, 0), vec_r, vec_r,
                 pl.BlockSpec((nb, 1, PAIR_ROWS, GATE_PAD), lambda b, c: (b, c, 0, 0)), tab, tab,
                 pl.BlockSpec((CONV_W, 3 * G_WIDTH), lambda b, c: (0, 0)),
                 pl.BlockSpec((1, G_WIDTH), lambda b, c: (0, 0))]
    args = [proj, gates, gb_row, al_row,
            gates_t.reshape(batch, nc, GATE_PAD // 2, GATE_PAD), pair(gb_row), pair(al_row),
            conv_w, norm_g.reshape(1, G_WIDTH)]
    s_spec = pl.BlockSpec((nb, G_HEADS, G_HEAD_DIM, G_HEAD_DIM), lambda b, c: (b, 0, 0, 0))
    conv_spec = pl.BlockSpec((nb, CONV_HIST, 3 * G_WIDTH), lambda b, c: (b, 0, 0))
    outs = pl.pallas_call(
        functools.partial(_gdn_paired_kernel, L=L, nb=nb, nc=nc, cast_periods=cast_periods),
        grid=(batch // nb, nc),
        in_specs=in_specs + cast_specs,
        out_specs=[tok(G_WIDTH, 0), s_spec, conv_spec] + cast_specs,
        out_shape=[jax.ShapeDtypeStruct(proj.shape[:-1] + (G_WIDTH,), _scan_out_dtype(L)),
                   jax.ShapeDtypeStruct((batch, G_HEADS, G_HEAD_DIM, G_HEAD_DIM), F32),
                   jax.ShapeDtypeStruct((batch, CONV_HIST, 3 * G_WIDTH), F32)]
        + [jax.ShapeDtypeStruct(w.shape, BF16) for w in side_cast],
        scratch_shapes=[pltpu.VMEM((nb, SUBLANES + L, 3 * G_WIDTH), F32)],
        compiler_params=_cparams(("arbitrary", "arbitrary")),
        name="gdn",
    )(*args, *side_cast)
    return outs[:3], outs[3:]


def _gdn(proj, gates, gates_t, gvecs, conv_w, norm_g, batch, nc, L, nb, init, side_cast=()):
    if init is None and L > SMALL_L:
        return _gdn_paired(proj, gates, gates_t, gvecs, conv_w, norm_g, batch, nc, L, nb, side_cast)
    assert L <= SMALL_L and not side_cast
    has_init = init is not None
    in_specs = [_tok_spec(L, nb, nc, 4 * G_WIDTH, 1)] + _gate_specs(L, nb, nc)
    in_specs += [pl.BlockSpec((CONV_W, 3 * G_WIDTH), lambda b, c: (0, 0)),
                 pl.BlockSpec((1, G_WIDTH), lambda b, c: (0, 0))]
    s_spec = pl.BlockSpec((nb, G_HEADS, G_HEAD_DIM, G_HEAD_DIM), lambda b, c: (b, 0, 0, 0))
    conv_spec = pl.BlockSpec((nb, CONV_HIST, 3 * G_WIDTH), lambda b, c: (b, 0, 0))
    args = [proj, gates, gates_t, *gvecs, conv_w, norm_g.reshape(1, G_WIDTH)]
    if has_init:
        in_specs += [s_spec, conv_spec]
        args += list(init)
    outs = pl.pallas_call(
        functools.partial(_gdn_kernel, L=L, nb=nb, nc=nc, has_init=has_init),
        grid=(batch // nb, nc),
        in_specs=in_specs,
        out_specs=[_tok_spec(L, nb, nc, G_WIDTH, 0), s_spec, conv_spec],
        out_shape=[jax.ShapeDtypeStruct(proj.shape[:-1] + (G_WIDTH,), _scan_out_dtype(L)),
                   jax.ShapeDtypeStruct((batch, G_HEADS, G_HEAD_DIM, G_HEAD_DIM), F32),
                   jax.ShapeDtypeStruct((batch, CONV_HIST, 3 * G_WIDTH), F32)],
        scratch_shapes=[pltpu.VMEM((nb, 2 * SUBLANES, 3 * G_WIDTH), F32)],
        compiler_params=_cparams(("arbitrary", "arbitrary")),
        name="gdn",
    )(*args)
    return outs, []


def _layer_norm(y, g, b):
    mu = jnp.mean(y, axis=-1, keepdims=True)
    yc = y - mu
    var = jnp.mean(yc * yc, axis=-1, keepdims=True)
    return yc * lax.rsqrt(var + LN_EPS) * g + b


def _outproj_kernel(hm_ref, hg_ref, x_ref, gt_ref, w_ref, g_ref, b_ref, o_ref):
    tm = x_ref.shape[0]
    half = tm // OUTPROJ_SLABS
    for r in range(OUTPROJ_SLABS):
        rows = slice(r * half, (r + 1) * half)
        mix = (_dot(hm_ref[rows, :].astype(BF16), w_ref[0:M_WIDTH, :])
               + _dot(hg_ref[rows, :].astype(BF16), w_ref[M_WIDTH:M_WIDTH + G_WIDTH, :]))
        gt = gt_ref[0] if gt_ref.shape[1] == 1 else gt_ref[0, rows, :]
        y = DEEPNORM_ALPHA * x_ref[rows, :] + (1.0 + gt) * mix
        o_ref[rows, :] = _layer_norm(y, g_ref[...], b_ref[...])


def _outproj(hm, hg, x2d, gt, w_out, ln_g, ln_b, tm, tiles_per_mod):
    m, d = x2d.shape
    vec = pl.BlockSpec((1, d), lambda i: (0, 0))
    return pl.pallas_call(
        _outproj_kernel,
        grid=(m // tm,),
        in_specs=[pl.BlockSpec((tm, M_WIDTH), lambda i: (i, 0)),
                  pl.BlockSpec((tm, G_WIDTH), lambda i: (i, 0)),
                  pl.BlockSpec((tm, d), lambda i: (i, 0)),
                  _mod_spec(gt, tiles_per_mod, 1),
                  pl.BlockSpec((d, d), lambda i: (0, 0)),
                  vec, vec],
        out_specs=pl.BlockSpec((tm, d), lambda i: (i, 0)),
        out_shape=jax.ShapeDtypeStruct((m, d), F32),
        compiler_params=_cparams(("arbitrary",)),
        name="outproj",
    )(hm, hg, x2d, gt, w_out, ln_g.reshape(1, d), ln_b.reshape(1, d))


def _ffn_kernel(x_ref, sc_ref, sh_ref, gt_ref, wg_ref, wu_ref, wd_ref, g_ref, b_ref, o_ref, h_scr, acc):
    f = pl.program_id(1)

    @pl.when(f == 0)
    def _():
        h_scr[...] = (x_ref[...] * (1.0 + sc_ref[0]) + sh_ref[0]).astype(BF16)
        acc[...] = jnp.zeros_like(acc)

    h = h_scr[...]
    gate = _dot(h, wg_ref[...])
    up = _dot(h, wu_ref[...])
    act = (gate * jax.nn.sigmoid(gate) * up).astype(BF16)
    acc[...] += _dot(act, wd_ref[...])

    @pl.when(f == pl.num_programs(1) - 1)
    def _():
        y = DEEPNORM_ALPHA * x_ref[...] + (1.0 + gt_ref[0]) * acc[...]
        o_ref[...] = _layer_norm(y, g_ref[...], b_ref[...])


def _ffn(x2d, sc, sh, gt, w_gu, w_down, ln_g, ln_b, tm, tiles_per_mod):
    m, d = x2d.shape
    tf = TF_FFN
    nf = D_FF // tf
    mod_spec = _mod_spec(sc, tiles_per_mod, 2)
    vec = pl.BlockSpec((1, d), lambda i, f: (0, 0))
    return pl.pallas_call(
        _ffn_kernel,
        grid=(m // tm, nf),
        in_specs=[pl.BlockSpec((tm, d), lambda i, f: (i, 0)),
                  mod_spec, mod_spec, mod_spec,
                  pl.BlockSpec((d, tf), lambda i, f: (0, f)),
                  pl.BlockSpec((d, tf), lambda i, f: (0, nf + f)),
                  pl.BlockSpec((tf, d), lambda i, f: (f, 0)),
                  vec, vec],
        out_specs=pl.BlockSpec((tm, d), lambda i, f: (i, 0)),
        out_shape=jax.ShapeDtypeStruct((m, d), F32),
        scratch_shapes=[pltpu.VMEM((tm, d), BF16), pltpu.VMEM((tm, d), F32)],
        compiler_params=_cparams(("arbitrary", "arbitrary")),
        name="ffn",
    )(x2d, sc, sh, gt, w_gu, w_gu, w_down, ln_g.reshape(1, d), ln_b.reshape(1, d))


def _layer(x, ada6, weights, init_m, init_g, nb):
    batch, seq, d = x.shape
    L = CHUNK if seq % CHUNK == 0 else seq
    nc = seq // L
    m = batch * seq
    x2d = x.reshape(m, d)

    def tiling(tm_cap):
        tm = min(tm_cap, m)
        if seq % tm == 0:
            return tm, seq // tm, lambda a: a.reshape(batch, 1, d)
        return tm, 1, lambda a: jnp.repeat(a, seq, axis=0).reshape(m // tm, tm, d)

    sh1, sc1, gt1, sh2, sc2, gt2 = ada6
    tm, tpm, mod = tiling(TM_INPROJ)
    proj, gates = _inproj(x2d, mod(sc1), mod(sh1), weights["w_in_main"], weights["w_in_gate"], tm, tpm)
    gates_t = jnp.swapaxes(gates.reshape(batch, nc, L, GATE_PAD), 2, 3)
    proj_v, gates_v = _tok_view(proj, batch, nc), _tok_view(gates, batch, nc)
    gvecs = weights["gvecs"]
    hm, c1, n1, m1 = _mlstm(proj_v, gates_v, gates_t, gvecs, weights["m_norm_g"], batch, nc, L, nb["mlstm"],
                            init_m)
    pending = [name for name in ("w_out", "w_gu", "w_down") if weights[name].dtype != BF16]
    (hg, s1, conv1), converted = _gdn(proj_v, gates_v, gates_t, gvecs, weights["conv_w"], weights["g_norm_g"],
                                      batch, nc, L, nb["gdn"], init_g, [weights[name] for name in pending])
    weights.update(zip(pending, converted))
    tm, tpm, mod = tiling(TM_OUTPROJ)
    x1 = _outproj(hm.reshape(m, M_WIDTH), hg.reshape(m, G_WIDTH), x2d, mod(gt1), weights["w_out"],
                  weights["ln1_g"], weights["ln1_b"], tm, tpm)
    tm, tpm, mod = tiling(TM_FFN)
    y = _ffn(x1, mod(sc2), mod(sh2), mod(gt2), weights["w_gu"], weights["w_down"], weights["ln2_g"],
             weights["ln2_b"], tm, tpm)
    return y.reshape(batch, seq, d), c1, n1, m1[:, 0, :M_HEADS], s1, conv1


def kernel(x_prompt, x_sample, state_mlstm_C, state_mlstm_n, state_mlstm_m, state_gdn_S, state_gdn_conv,
           c_prompt, c_sample, w_ada, b_ada, w_in, m_i_bias, m_f_bias, m_norm_g, conv_w, g_dt_bias,
           g_A_log, g_norm_g, w_out, ln1_g, ln1_b, w_gu, w_down, ln2_g, ln2_b):
    bp, _, d = x_prompt.shape
    bs = x_sample.shape[0]

    ada = _ada(jnp.concatenate([c_prompt, c_sample], axis=0), w_ada, b_ada)
    ada6 = [ada[:, i * d:(i + 1) * d] for i in range(6)]

    gate_bias = jnp.zeros((GATE_PAD,), F32)
    gate_bias = gate_bias.at[GI0:GI0 + M_HEADS].set(m_i_bias).at[GF0:GF0 + M_HEADS].set(m_f_bias)
    gate_bias = gate_bias.at[GA0:GA0 + G_HEADS].set(g_dt_bias)
    a_log = jnp.zeros((GATE_PAD,), F32).at[GA0:GA0 + G_HEADS].set(g_A_log)
    weights = {
        "w_in_main": w_in.T,
        "w_in_gate": jnp.pad(w_in.T[MAIN_COLS:], ((0, GATE_PAD - GATE_COLS), (0, 0))).astype(BF16),
        "gvecs": (gate_bias.reshape(1, GATE_PAD), gate_bias.reshape(GATE_PAD, 1),
                  a_log.reshape(1, GATE_PAD), a_log.reshape(GATE_PAD, 1)),
        "m_norm_g": m_norm_g, "conv_w": conv_w, "g_norm_g": g_norm_g,
        "w_out": w_out, "w_gu": w_gu, "w_down": w_down,
        "ln1_g": ln1_g, "ln1_b": ln1_b, "ln2_g": ln2_g, "ln2_b": ln2_b,
    }

    y_p, p_c, p_n, p_m, p_s, p_conv = _layer(x_prompt, [a[:bp] for a in ada6], weights, None, None,
                                             SEQS_PER_STEP_PROMPT)
    m0 = jnp.pad(state_mlstm_m, ((0, 0), (0, GATE_PAD - M_HEADS))).reshape(bs, 1, GATE_PAD)
    y_s, s_c, s_n, s_m, s_s, s_conv = _layer(
        x_sample, [a[bp:] for a in ada6], weights, (state_mlstm_C, state_mlstm_n, m0),
        (state_gdn_S, state_gdn_conv), SEQS_PER_STEP_SAMPLE)
    return (y_p, y_s, p_c, p_n, p_m, p_s, p_conv, s_c, s_n, s_m, s_s, s_conv)
```
